```python
import jax, jax.numpy as jnp
from jax import lax
import numpy as np

D_MODEL = 1024
BATCH = 8
SEQ = 2048
DEPTH = 4
DEC_BATCH = 128
DEC_SEQ = 8
PAST_LEN = 8192
PAGE_SIZE = 128

N_A = DEPTH // 2
N_B = DEPTH - N_A
MEM_WIDTH = D_MODEL // 4
MIX_A = D_MODEL - MEM_WIDTH
HEAD_A = 64
H_A = MIX_A // HEAD_A
DECAY_LORA = 64
A_LORA = 64
VRES_LORA = 32
GATE_LORA = 160
GN_EPS = 64e-5
C_RWKV = 3 * MIX_A + DECAY_LORA + A_LORA + GATE_LORA
C_A = C_RWKV + MEM_WIDTH
N_MEM = 256
MEM_HEADS = 4
MEM_HEAD_DIM = MEM_WIDTH // MEM_HEADS
MEM_SCALE = MEM_HEAD_DIM ** -0.5
NOPE_DIM = 128
ROPE_DIM = 64
V_DIM = 128
H_B = MIX_A // V_DIM
Q_LORA = 256
KV_LORA = 256
C_B = Q_LORA + MEM_WIDTH
ROPE_BASE = 10000.0
Q_BLOCK = 128
ATTN_SCALE = (NOPE_DIM + ROPE_DIM) ** -0.5
D_FF = 2816
CONV_W = 3
RMS_EPS = 1e-6

kernel_name = 'yoco_rwkv7_mla_memxattn_convglu_step'


def rms_norm(x, g):
    xf = x.astype(jnp.float32)
    y = xf * lax.rsqrt(jnp.mean(xf * xf, axis=-1, keepdims=True) + RMS_EPS)
    return (y * g.astype(jnp.float32)).astype(x.dtype)


def rope(x, pos):
    half = ROPE_DIM // 2
    inv = ROPE_BASE ** (-jnp.arange(half, dtype=jnp.float32) / half)
    ang = pos.astype(jnp.float32)[:, None] * inv[None, :]
    cos = jnp.cos(ang)[None, :, None, :]
    sin = jnp.sin(ang)[None, :, None, :]
    xf = x.astype(jnp.float32)
    x1, x2 = xf[..., :half], xf[..., half:]
    return jnp.concatenate([x1 * cos - x2 * sin, x1 * sin + x2 * cos], axis=-1).astype(x.dtype)


def wkv7_scan(s0, r, decay, k, v, kk, a):
    def step(s, inp):
        r_t, w_t, k_t, v_t, kk_t, a_t = inp
        s_kk = jnp.einsum('bhvk,bhk->bhv', s, kk_t)
        s = (s * w_t[:, :, None, :] - s_kk[..., None] * (kk_t * a_t)[:, :, None, :]
             + v_t[..., None] * k_t[:, :, None, :])
        return s, jnp.einsum('bhvk,bhk->bhv', s, r_t)
    seq = (r, decay, k, v, kk, a)
    seq = tuple(jnp.swapaxes(t, 0, 1) for t in seq)
    s, y = lax.scan(step, s0, seq)
    return s, jnp.swapaxes(y, 0, 1)


def rwkv7_mix(xn, shift_prev, wkv_prev, v_first, w_in, mu, w_decay_up, w0, w_a_up, a0,
              w_g_up, k_k, k_a, r_k, lnx_w, lnx_b, vres):
    B, T, _ = xn.shape
    f32 = jnp.float32
    xs = jnp.concatenate([shift_prev[:, None, :].astype(xn.dtype), xn], axis=1)
    proj = xs @ w_in
    cur, prev = proj[:, 1:], proj[:, :-1]
    mixed = cur[..., :C_RWKV] + (prev[..., :C_RWKV] - cur[..., :C_RWKV]) * mu
    q_mem = cur[..., C_RWKV:C_A]
    cuts = [MIX_A, MIX_A + DECAY_LORA, 2 * MIX_A + DECAY_LORA, 3 * MIX_A + DECAY_LORA,
            3 * MIX_A + DECAY_LORA + A_LORA]
    r, wl, k, v, al, gl = jnp.split(mixed, cuts, axis=-1)
    w_log = -jax.nn.softplus(-(w0 + jnp.tanh(wl) @ w_decay_up).astype(f32)) - 0.5
    decay = jnp.exp(-jnp.exp(w_log))
    a = jax.nn.sigmoid((a0 + al @ w_a_up).astype(f32))
    g = jax.nn.sigmoid(gl) @ w_g_up
    if vres is not None:
        mu_v, w_v_up, v0 = vres
        vl = cur[..., C_A:] + (prev[..., C_A:] - cur[..., C_A:]) * mu_v
        v = v + (v_first - v) * jax.nn.sigmoid(v0 + vl @ w_v_up)
    heads = lambda t: t.reshape(B, T, H_A, HEAD_A)
    rf, kf, vf, af = heads(r.astype(f32)), heads(k.astype(f32)), heads(v.astype(f32)), heads(a)
    kk = kf * k_k.astype(f32).reshape(H_A, HEAD_A)
    kk = kk * lax.rsqrt(jnp.maximum(jnp.sum(kk * kk, axis=-1, keepdims=True), 1e-24))
    kf = kf * (1.0 + (af - 1.0) * k_a.astype(f32).reshape(H_A, HEAD_A))
    s_new, y = wkv7_scan(wkv_prev.astype(f32), rf, heads(decay), kf, vf, kk, af)
    mean = jnp.mean(y, axis=-1, keepdims=True)
    var = jnp.mean(jnp.square(y - mean), axis=-1, keepdims=True)
    y = ((y - mean) * lax.rsqrt(var + GN_EPS)).reshape(B, T, MIX_A)
    y = y * lnx_w.astype(f32) + lnx_b.astype(f32)
    bonus = jnp.sum(rf * kf * r_k.astype(f32), axis=-1, keepdims=True) * vf
    o = (y + bonus.reshape(B, T, MIX_A)) * g.astype(f32)
    return o.astype(xn.dtype), q_mem, xn[:, -1], s_new.astype(wkv_prev.dtype), v


def memory_kv(mem, norm_mem, w_mem_kv):
    B, M, _ = mem.shape
    mf = mem.astype(jnp.float32)
    mhat = (mf * lax.rsqrt(jnp.mean(mf * mf, axis=-1, keepdims=True) + RMS_EPS)).astype(mem.dtype)
    kv = jnp.einsum('bmd,ld,ldc->lbmc', mhat, norm_mem, w_mem_kv)
    kv = kv.reshape(DEPTH, B, M, 2, MEM_HEADS, MEM_HEAD_DIM)
    return kv[:, :, :, 0], kv[:, :, :, 1]


def memory_attention(q, mem_k, mem_v):
    B, T, _ = q.shape
    qh = q.reshape(B, T, MEM_HEADS, MEM_HEAD_DIM)
    s = jnp.einsum('bshd,bmhd->bhsm', qh, mem_k).astype(jnp.float32) * MEM_SCALE
    p = jax.nn.softmax(s, axis=-1).astype(mem_v.dtype)
    return jnp.einsum('bhsm,bmhd->bshd', p, mem_v).reshape(B, T, MEM_WIDTH)


def shared_latent(x, pos, norm_kv, w_kv_a, norm_ckv):
    h = rms_norm(x, norm_kv) @ w_kv_a
    ckv = rms_norm(h[..., :KV_LORA], norm_ckv)
    kpe = rope(h[..., None, KV_LORA:], pos)[:, :, 0]
    return ckv, kpe


def mla_queries(xn, pos, w_in, norm_q, w_q_b):
    B, T, _ = xn.shape
    proj = xn @ w_in
    q_a, q_mem = proj[..., :Q_LORA], proj[..., Q_LORA:]
    q = (rms_norm(q_a, norm_q) @ w_q_b).reshape(B, T, H_B, NOPE_DIM + ROPE_DIM)
    return q[..., :NOPE_DIM], rope(q[..., NOPE_DIM:], pos), q_mem


def causal_block_attention(q, k, v):
    T = q.shape[1]
    blk = min(Q_BLOCK, T)
    outs = []
    for i in range(T // blk):
        lo, hi = i * blk, (i + 1) * blk
        s = jnp.einsum('bqhd,bkhd->bhqk', q[:, lo:hi], k[:, :hi]).astype(jnp.float32) * ATTN_SCALE
        mask = jnp.arange(hi)[None, :] <= jnp.arange(lo, hi)[:, None]
        p = jax.nn.softmax(jnp.where(mask, s, -jnp.inf), axis=-1).astype(v.dtype)
        outs.append(jnp.einsum('bhqk,bkhd->bqhd', p, v[:, :hi]))
    return jnp.concatenate(outs, axis=1)


def latent_decode_attention(q_nope, q_pe, w_uk, w_uv, ckv_past, kpe_past, ckv_new, kpe_new):
    T = q_nope.shape[1]
    Tp = ckv_past.shape[1]
    q_lat = jnp.einsum('bshd,chd->bshc', q_nope, w_uk)
    s_past = jnp.einsum('bshc,btc->bhst', q_lat, ckv_past) + jnp.einsum('bshr,btr->bhst', q_pe, kpe_past)
    s_new = jnp.einsum('bshc,btc->bhst', q_lat, ckv_new) + jnp.einsum('bshr,btr->bhst', q_pe, kpe_new)
    causal = jnp.tril(jnp.ones((T, T), dtype=bool))
    s_new = jnp.where(causal, s_new.astype(jnp.float32) * ATTN_SCALE, -jnp.inf)
    s = jnp.concatenate([s_past.astype(jnp.float32) * ATTN_SCALE, s_new], axis=-1)
    p = jax.nn.softmax(s, axis=-1).astype(ckv_new.dtype)
    o_lat = (jnp.einsum('bhst,btc->bshc', p[..., :Tp], ckv_past)
             + jnp.einsum('bhst,btc->bshc', p[..., Tp:], ckv_new))
    return jnp.einsum('bshc,chd->bshd', o_lat, w_uv)


def conv_glu(xn, conv_prev, w_up, conv_w, conv_b, w_down):
    T = xn.shape[1]
    u = xn @ w_up
    gate, val = u[..., :D_FF], u[..., D_FF:]
    ext = jnp.concatenate([conv_prev.astype(gate.dtype), gate], axis=1)
    conv = conv_b
    for j in range(CONV_W):
        conv = conv + ext[:, j:j + T] * conv_w[j]
    h = jax.nn.silu(conv) * val
    return h @ w_down, ext[:, T:]


def trunk(x, pos, mem_k, mem_v, wkv0, shift0, conv0, past, P):
    B, T, _ = x.shape
    new_wkv, new_shift, new_conv = [], [], []
    v_first = None
    for l in range(DEPTH):
        xn = rms_norm(x, P['norm_mix'][l])
        if l < N_A:
            if l == 0:
                w_in, vres = P['w_in_a'][0], None
            else:
                w_in = jnp.concatenate([P['w_in_a'][l], P['w_vres_in'][l - 1]], axis=1)
                vres = (P['mu_vres'][l - 1], P['w_vres_up'][l - 1], P['v0'][l - 1])
            o_tok, q_mem, s_shift, s_wkv, v = rwkv7_mix(
                xn, shift0[l], wkv0[l], v_first, w_in, P['mu_a'][l], P['w_decay_up'][l], P['w0'][l],
                P['w_a_up'][l], P['a0'][l], P['w_g_up'][l], P['k_k'][l], P['k_a'][l], P['r_k'][l],
                P['lnx_w'][l], P['lnx_b'][l], vres)
            if l == 0:
                v_first = v
            new_shift.append(s_shift)
            new_wkv.append(s_wkv)
        else:
            if l == N_A:
                ckv, kpe = shared_latent(x, pos, P['norm_kv'], P['w_kv_a'], P['norm_ckv'])
                w_uk, w_uv = P['w_kv_b'][..., :NOPE_DIM], P['w_kv_b'][..., NOPE_DIM:]
                if past is None:
                    k_full = jnp.concatenate(
                        [jnp.einsum('btc,chd->bthd', ckv, w_uk),
                         jnp.broadcast_to(kpe[:, :, None, :], (B, T, H_B, ROPE_DIM))], axis=-1)
                    v_full = jnp.einsum('btc,chd->bthd', ckv, w_uv)
            j = l - N_A
            q_nope, q_pe, q_mem = mla_queries(xn, pos, P['w_in_b'][j], P['norm_q'][j], P['w_q_b'][j])
            if past is None:
                o = causal_block_attention(jnp.concatenate([q_nope, q_pe], axis=-1), k_full, v_full)
            else:
                o = latent_decode_attention(q_nope, q_pe, w_uk, w_uv, past[0], past[1], ckv, kpe)
            o_tok = o.reshape(B, T, H_B * V_DIM)
        o_mem = memory_attention(q_mem, mem_k[l], mem_v[l])
        x = x + jnp.concatenate([o_tok, o_mem], axis=-1) @ P['w_o'][l]
        f, c = conv_glu(rms_norm(x, P['norm_ffn'][l]), conv0[l], P['w_ffn_up'][l],
                        P['conv_w'][l], P['conv_b'][l], P['w_ffn_down'][l])
        new_conv.append(c)
        x = x + f
    y = rms_norm(x, P['final_norm'])
    return y, ckv, kpe, jnp.stack(new_wkv), jnp.stack(new_shift), jnp.stack(new_conv)


def setup_inputs(seed: int = 0) -> dict:
    key = jax.random.key(seed)
    ks = iter(jax.random.split(key, 64))
    f32 = jnp.float32
    def nrm(shape, scale):
        return jax.random.normal(next(ks), shape, f32) * scale
    def uni(shape):
        return jax.random.uniform(next(ks), shape, f32)
    def gain(shape):
        return 1.0 + nrm(shape, 0.05)
    n_pages = PAST_LEN // PAGE_SIZE
    n_used = DEC_BATCH * n_pages
    n_phys = (n_used * 5 + 3) // 4
    perm = jax.random.permutation(next(ks), n_phys)
    page_table = perm[:n_used].reshape(DEC_BATCH, n_pages).astype(jnp.int32)
    na1 = max(N_A - 1, 0)
    return {
        'x_prompt': nrm((BATCH, SEQ, D_MODEL), 1.0),
        'x_sample': nrm((DEC_BATCH, DEC_SEQ, D_MODEL), 1.0),
        'cache_ckv': nrm((n_phys, PAGE_SIZE, KV_LORA), 1.0),
        'cache_kpe': nrm((n_phys, PAGE_SIZE, ROPE_DIM), 1.0),
        'cache_mem_k': nrm((DEPTH, DEC_BATCH, N_MEM, MEM_HEADS, MEM_HEAD_DIM), 1.0),
        'cache_mem_v': nrm((DEPTH, DEC_BATCH, N_MEM, MEM_HEADS, MEM_HEAD_DIM), 1.0),
        'state_wkv': nrm((N_A, DEC_BATCH, H_A, HEAD_A, HEAD_A), 0.3),
        'state_shift': nrm((N_A, DEC_BATCH, D_MODEL), 1.0),
        'state_conv': nrm((DEPTH, DEC_BATCH, CONV_W - 1, D_FF), 1.0),
        'page_table': page_table,
        'mem_prompt': nrm((BATCH, N_MEM, D_MODEL), 1.0),
        'norm_mix': gain((DEPTH, D_MODEL)),
        'norm_ffn': gain((DEPTH, D_MODEL)),
        'norm_mem': gain((DEPTH, D_MODEL)),
        'w_mem_kv': nrm((DEPTH, D_MODEL, 2 * MEM_WIDTH), D_MODEL ** -0.5),
        'w_o': nrm((DEPTH, MIX_A + MEM_WIDTH, D_MODEL), (MIX_A + MEM_WIDTH) ** -0.5),
        'w_in_a': nrm((N_A, D_MODEL, C_A), D_MODEL ** -0.5),
        'mu_a': uni((N_A, C_RWKV)),
        'w_vres_in': nrm((na1, D_MODEL, VRES_LORA), D_MODEL ** -0.5),
        'mu_vres': uni((na1, VRES_LORA)),
        'w_decay_up': nrm((N_A, DECAY_LORA, MIX_A), 0.5 * DECAY_LORA ** -0.5),
        'w0': -2.5 + nrm((N_A, MIX_A), 0.5),
        'w_a_up': nrm((N_A, A_LORA, MIX_A), A_LORA ** -0.5),
        'a0': nrm((N_A, MIX_A), 0.5),
        'w_g_up': nrm((N_A, GATE_LORA, MIX_A), GATE_LORA ** -0.5),
        'w_vres_up': nrm((na1, VRES_LORA, MIX_A), VRES_LORA ** -0.5),
        'v0': nrm((na1, MIX_A), 0.5),
        'k_k': 0.85 + nrm((N_A, MIX_A), 0.1),
        'k_a': 1.0 + nrm((N_A, MIX_A), 0.1),
        'r_k': nrm((N_A, H_A, HEAD_A), 0.1),
        'lnx_w': gain((N_A, MIX_A)),
        'lnx_b': nrm((N_A, MIX_A), 0.02),
        'norm_kv': gain((D_MODEL,)),
        'w_kv_a': nrm((D_MODEL, KV_LORA + ROPE_DIM), D_MODEL ** -0.5),
        'norm_ckv': gain((KV_LORA,)),
        'w_kv_b': nrm((KV_LORA, H_B, NOPE_DIM + V_DIM), KV_LORA ** -0.5),
        'w_in_b': nrm((N_B, D_MODEL, C_B), D_MODEL ** -0.5),
        'norm_q': gain((N_B, Q_LORA)),
        'w_q_b': nrm((N_B, Q_LORA, H_B * (NOPE_DIM + ROPE_DIM)), Q_LORA ** -0.5),
        'w_ffn_up': nrm((DEPTH, D_MODEL, 2 * D_FF), D_MODEL ** -0.5),
        'conv_w': nrm((DEPTH, CONV_W, D_FF), 0.5),
        'conv_b': nrm((DEPTH, D_FF), 0.02),
        'w_ffn_down': nrm((DEPTH, D_FF, D_MODEL), D_FF ** -0.5),
        'final_norm': gain((D_MODEL,)),
    }


def reference(x_prompt, x_sample, cache_ckv, cache_kpe, cache_mem_k, cache_mem_v,
              state_wkv, state_shift, state_conv, page_table, mem_prompt,
              norm_mix, norm_ffn, norm_mem, w_mem_kv, w_o,
              w_in_a, mu_a, w_vres_in, mu_vres, w_decay_up, w0, w_a_up, a0,
              w_g_up, w_vres_up, v0, k_k, k_a, r_k, lnx_w, lnx_b,
              norm_kv, w_kv_a, norm_ckv, w_kv_b, w_in_b, norm_q, w_q_b,
              w_ffn_up, conv_w, conv_b, w_ffn_down, final_norm):
    P = dict(norm_mix=norm_mix, norm_ffn=norm_ffn, w_o=w_o,
             w_in_a=w_in_a, mu_a=mu_a, w_vres_in=w_vres_in, mu_vres=mu_vres,
             w_decay_up=w_decay_up, w0=w0, w_a_up=w_a_up, a0=a0, w_g_up=w_g_up,
             w_vres_up=w_vres_up, v0=v0, k_k=k_k, k_a=k_a, r_k=r_k, lnx_w=lnx_w, lnx_b=lnx_b,
             norm_kv=norm_kv, w_kv_a=w_kv_a, norm_ckv=norm_ckv, w_kv_b=w_kv_b,
             w_in_b=w_in_b, norm_q=norm_q, w_q_b=w_q_b,
             w_ffn_up=w_ffn_up, conv_w=conv_w, conv_b=conv_b, w_ffn_down=w_ffn_down,
             final_norm=final_norm)
    dt = x_prompt.dtype
    B, T = x_prompt.shape[:2]
    mem_k_p, mem_v_p = memory_kv(mem_prompt, norm_mem, w_mem_kv)
    y_p, ckv_p, kpe_p, wkv_p, shift_p, conv_p = trunk(
        x_prompt, jnp.arange(T), mem_k_p, mem_v_p,
        jnp.zeros((N_A, B, H_A, HEAD_A, HEAD_A), dt), jnp.zeros((N_A, B, D_MODEL), dt),
        jnp.zeros((DEPTH, B, CONV_W - 1, D_FF), dt), None, P)
    DB, TS = x_sample.shape[:2]
    past_len = page_table.shape[1] * PAGE_SIZE
    ckv_past = cache_ckv[page_table].reshape(DB, past_len, KV_LORA)
    kpe_past = cache_kpe[page_table].reshape(DB, past_len, ROPE_DIM)
    y_s, ckv_s, kpe_s, wkv_s, shift_s, conv_s = trunk(
        x_sample, past_len + jnp.arange(TS), cache_mem_k, cache_mem_v,
        state_wkv, state_shift, state_conv, (ckv_past, kpe_past), P)
    return (y_p, y_s, ckv_p, kpe_p, mem_k_p, mem_v_p, wkv_p, shift_p, conv_p,
            ckv_s, kpe_s, wkv_s, shift_s, conv_s)
```

```python
import functools
import math

import jax
import jax.numpy as jnp
from jax import lax
from jax.experimental import pallas as pl
from jax.experimental.pallas import tpu as pltpu

F32 = jnp.float32
BF16 = jnp.bfloat16

D_MODEL = 1024
DEPTH = 4
N_A = 2
N_B = 2
MEM_WIDTH = 256
MIX_A = 768
HEAD_A = 64
H_A = 12
DECAY_LORA = 64
A_LORA = 64
VRES_LORA = 32
GATE_LORA = 160
GN_EPS = 64e-5
C_RWKV = 3 * MIX_A + DECAY_LORA + A_LORA + GATE_LORA
C_A = C_RWKV + MEM_WIDTH
MEM_HEADS = 4
MEM_HEAD_DIM = 64
MEM_SCALE = MEM_HEAD_DIM ** -0.5
NOPE_DIM = 128
ROPE_DIM = 64
V_DIM = 128
H_B = 6
Q_LORA = 256
KV_LORA = 256
ROPE_BASE = 10000.0
ATTN_SCALE = (NOPE_DIM + ROPE_DIM) ** -0.5
D_FF = 2816
CONV_W = 3
RMS_EPS = 1e-6
PAGE_SIZE = 128

PA_R, PA_K, PA_V = 0, 768, 1536
PA_LORA = 2304
PA_GATE = 2432
PA_QMEM = 2688
PA_COLS = 3072

VMEM_LIMIT_BYTES = 52 * 1024 * 1024

NN = ((1,), (0,))
NT = ((1,), (1,))
TN = ((0,), (0,))


def _dg(a, b, dims=NN):
    return lax.dot_general(a, b, (dims, ((), ())), preferred_element_type=F32)


def _split(x):
    hi = x.astype(BF16)
    lo = (x - hi.astype(F32)).astype(BF16)
    return hi, lo


def _dot3(a, b, dims=NN):
    ah, al = _split(a)
    bh, bl = _split(b)
    return _dg(ah, bh, dims) + (_dg(ah, bl, dims) + _dg(al, bh, dims))


def _cparams(n_axes):
    return pltpu.CompilerParams(dimension_semantics=("arbitrary",) * n_axes,
                                vmem_limit_bytes=VMEM_LIMIT_BYTES)


def _rms(x, g, eps=RMS_EPS):
    return x * lax.rsqrt(jnp.mean(x * x, axis=-1, keepdims=True) + eps) * g


def _pick_tile(n, candidates):
    for c in candidates:
        if n % c == 0:
            return c
    return n


ROW_TILES = (512, 256, 128, 64, 32, 16, 8)


def _pos_tile(M, T):
    return _pick_tile(T, ROW_TILES) if T >= ROW_TILES[0] else _pick_tile(M, ROW_TILES)


def _mm_kernel(*refs, n_in, has_norm, has_res, f32_in):
    xs = refs[:n_in]
    ws = refs[n_in:2 * n_in]
    k = 2 * n_in
    g_ref = refs[k] if has_norm else None
    k += int(has_norm)
    r_ref = refs[k] if has_res else None
    k += int(has_res)
    o_ref = refs[k]
    scr = list(refs[k + 1:])

    lhs = []
    si = 0
    for idx in range(n_in):
        if f32_in[idx]:
            s_ref = scr[si]
            si += 1

            @pl.when(pl.program_id(1) == 0)
            def _(x_ref=xs[idx], s_ref=s_ref, idx=idx):
                x = x_ref[...]
                if has_norm and idx == 0:
                    x = _rms(x, g_ref[...])
                s_ref[...] = x.astype(BF16)

            lhs.append(s_ref)
        else:
            lhs.append(xs[idx])
    acc = _dg(lhs[0][...], ws[0][...])
    for idx in range(1, n_in):
        acc = acc + _dg(lhs[idx][...], ws[idx][...])
    if has_res:
        acc = acc + r_ref[...]
    o_ref[...] = acc.astype(o_ref.dtype)


def _mm(xs, ws, *, g=None, res=None, out_dtype=F32, x_col_blocks=None, ks=None, name="mm"):
    n_in = len(xs)
    M = xs[0].shape[0]
    N = ws[0].shape[1]
    ks = [w.shape[0] for w in ws]
    x_col_blocks = x_col_blocks or [0] * n_in
    tm = _pick_tile(M, (512, 256, 128, 64, 32, 16, 8))
    tn = _pick_tile(N, (1024, 768, 512, 384, 256, 128))
    f32_in = tuple(x.dtype == F32 for x in xs)
    in_specs = []
    for x, kk, cb in zip(xs, ks, x_col_blocks):
        in_specs.append(pl.BlockSpec((tm, kk), lambda i, j, cb=cb: (i, cb)))
    for w, kk in zip(ws, ks):
        in_specs.append(pl.BlockSpec((kk, tn), lambda i, j: (0, j)))
    args = list(xs) + list(ws)
    if g is not None:
        in_specs.append(pl.BlockSpec((1, ks[0]), lambda i, j: (0, 0)))
        args.append(g.reshape(1, ks[0]).astype(F32))
    if res is not None:
        in_specs.append(pl.BlockSpec((tm, tn), lambda i, j: (i, j)))
        args.append(res)
    scratch = [pltpu.VMEM((tm, kk), BF16) for kk, f in zip(ks, f32_in) if f]
    return pl.pallas_call(
        functools.partial(_mm_kernel, n_in=n_in, has_norm=g is not None, has_res=res is not None,
                          f32_in=f32_in),
        out_shape=jax.ShapeDtypeStruct((M, N), out_dtype),
        grid=(M // tm, N // tn),
        in_specs=in_specs,
        out_specs=pl.BlockSpec((tm, tn), lambda i, j: (i, j)),
        scratch_shapes=scratch,
        compiler_params=_cparams(2),
        name=name,
    )(*args)


def _rmsnorm_kernel(x_ref, g_ref, o_ref):
    o_ref[...] = _rms(x_ref[...], g_ref[...])


def _rmsnorm(x, g, name="rmsnorm"):
    M, D = x.shape
    tm = _pick_tile(M, (512, 256, 128, 64, 32, 16, 8))
    return pl.pallas_call(
        _rmsnorm_kernel,
        out_shape=jax.ShapeDtypeStruct((M, D), F32),
        grid=(M // tm,),
        in_specs=[pl.BlockSpec((tm, D), lambda i: (i, 0)), pl.BlockSpec((1, D), lambda i: (0, 0))],
        out_specs=pl.BlockSpec((tm, D), lambda i: (i, 0)),
        compiler_params=_cparams(1),
        name=name,
    )(x, g.reshape(1, D).astype(F32))


def _segsum64(x):
    r = lax.broadcasted_iota(jnp.int32, (128, 128), 0) >> 6
    c = lax.broadcasted_iota(jnp.int32, (128, 128), 1) >> 6
    ones = jnp.where(r == c, 1.0, 0.0).astype(BF16)
    outs = []
    for j in range(x.shape[1] // 128):
        hi, lo = _split(x[:, j * 128:(j + 1) * 128])
        outs.append(_dg(hi, ones) + _dg(lo, ones))
    return jnp.concatenate(outs, axis=1)


def _softplus(x):
    return jnp.maximum(x, 0.0) + jnp.log1p(jnp.exp(-jnp.abs(x)))


def _prep_kernel(*refs, has_vres):
    (cur_ref, prev_ref, mu_ref, wd_ref, wa_ref, wg_ref, wv_ref, w0_ref, a0_ref, v0_ref, kk_ref, ka_ref) = refs[:12]
    k = 12
    vf_ref = refs[k] if has_vres else None
    k += int(has_vres)
    r_o, lw_o, k_o, v_o, kn_o, b_o, g_o, qm_o = refs[k:k + 8]

    def mix(lo, hi):
        c = cur_ref[:, lo:hi]
        return c + (prev_ref[:, lo:hi] - c) * mu_ref[:, lo:hi]

    lora = mix(PA_LORA, PA_LORA + 128)
    zw = _dg(jnp.tanh(lora).astype(BF16), wd_ref[...]) + w0_ref[...]
    w_log = -_softplus(-zw) - 0.5
    lw_o[...] = -jnp.exp(w_log)
    a = jax.nn.sigmoid(_dg(lora.astype(BF16), wa_ref[...]) + a0_ref[...])
    gin = mix(PA_GATE, PA_GATE + 256)
    g_o[...] = _dg(jax.nn.sigmoid(gin).astype(BF16), wg_ref[...])
    r_o[...] = mix(PA_R, PA_R + MIX_A)
    kx = mix(PA_K, PA_K + MIX_A)
    v = mix(PA_V, PA_V + MIX_A)
    if has_vres:
        sv = jax.nn.sigmoid(_dg(gin.astype(BF16), wv_ref[...]) + v0_ref[...])
        v = v + (vf_ref[...] - v) * sv
    v_o[...] = v
    kk = kx * kk_ref[...]
    kk = kk * lax.rsqrt(jnp.maximum(_segsum64(kk * kk), 1e-24))
    kn_o[...] = kk
    b_o[...] = kk * a
    k_o[...] = kx * (1.0 + (a - 1.0) * ka_ref[...])
    qm_o[...] = cur_ref[:, PA_QMEM:PA_QMEM + MEM_WIDTH].astype(BF16)


def _rwkv_prep(cur, prev, pw, v_first):
    M = cur.shape[0]
    tm = _pick_tile(M, (256, 128, 64, 32, 16, 8))
    has_vres = v_first is not None
    row = lambda w: pl.BlockSpec((tm, w), lambda i: (i, 0))
    full = lambda a: pl.BlockSpec(a.shape, lambda i: (0,) * a.ndim)
    consts = [pw["mu"], pw["wd"], pw["wa"], pw["wg"], pw["wv"], pw["w0"], pw["a0"], pw["v0"], pw["k_k"], pw["k_a"]]
    args = [cur, prev] + consts
    in_specs = [row(PA_COLS), row(PA_COLS)] + [full(a) for a in consts]
    if has_vres:
        args.append(v_first)
        in_specs.append(row(MIX_A))
    outs = [jax.ShapeDtypeStruct((M, MIX_A), F32)] * 7 + [jax.ShapeDtypeStruct((M, MEM_WIDTH), BF16)]
    out_specs = [row(MIX_A)] * 7 + [row(MEM_WIDTH)]
    return pl.pallas_call(
        functools.partial(_prep_kernel, has_vres=has_vres),
        out_shape=outs,
        grid=(M // tm,),
        in_specs=in_specs,
        out_specs=out_specs,
        compiler_params=_cparams(1),
        name="rwkv_prep",
    )(*args)


def _wkv_kernel(r_ref, lw_ref, k_ref, v_ref, kn_ref, b_ref, g_ref, rk_ref, lnw_ref, lnb_ref, h0_ref,
                o_ref, hT_ref, h_scr, *, C, H, N, nchunks):
    c_idx = pl.program_id(1)

    @pl.when(c_idx == 0)
    def _():
        h_scr[...] = h0_ref[0]

    lw = lw_ref[...]
    ti = lax.broadcasted_iota(jnp.int32, (C, C), 0)
    tj = lax.broadcasted_iota(jnp.int32, (C, C), 1)
    ltri = jnp.where(tj <= ti, 1.0, 0.0).astype(BF16)
    lw_hi, lw_lo = _split(lw)
    cum = _dg(ltri, lw_hi) + _dg(ltri, lw_lo)
    cum_last = jnp.sum(lw, axis=0, keepdims=True)
    e_neg = jnp.exp(-cum)
    r_all = r_ref[...]
    k_all = k_ref[...]
    v_all = v_ref[...]
    kt_all = kn_ref[...] * jnp.exp(cum - lw)
    rt_all = r_all * jnp.exp(cum)
    kh_all = k_all * e_neg
    bh_all = b_ref[...] * e_neg
    e_rem = jnp.exp(cum_last - cum)
    kb_all = k_all * e_rem
    bb_all = b_ref[...] * e_rem
    p_last = jnp.exp(cum_last)
    rkr = r_all * k_all * rk_ref[...]

    strict = tj < ti
    incl = tj <= ti
    ni = lax.broadcasted_iota(jnp.int32, (N, N), 0)
    nj = lax.broadcasted_iota(jnp.int32, (N, N), 1)
    eye_n = ni == nj
    eye_c = jnp.where(ti == tj, 1.0, 0.0)

    for h in range(H):
        sl = slice(h * N, (h + 1) * N)
        kt, rt, kh, bh = kt_all[:, sl], rt_all[:, sl], kh_all[:, sl], bh_all[:, sl]
        kb, bb, vv = kb_all[:, sl], bb_all[:, sl], v_all[:, sl]
        a_k = jnp.where(strict, _dot3(kt, kh, NT), 0.0)
        a_b = jnp.where(strict, _dot3(kt, bh, NT), 0.0)
        q_k = jnp.where(incl, _dot3(rt, kh, NT), 0.0)
        q_b = jnp.where(incl, _dot3(rt, bh, NT), 0.0)
        tinv = eye_c - jnp.where(((ti >> 1) == (tj >> 1)), a_b, 0.0)
        blk = 2
        while blk < C:
            sh = blk.bit_length() - 1
            m_b = jnp.where(((ti >> (sh + 1)) == (tj >> (sh + 1))) & (((ti >> sh) & 1) == 1) & (((tj >> sh) & 1) == 0),
                            a_b, 0.0)
            tinv = tinv - _dot3(tinv, _dot3(m_b, tinv))
            blk *= 2
        x = _dot3(tinv, jnp.concatenate([kt, _dot3(a_k, vv)], axis=1))
        qw = _dot3(q_b, x)
        r2 = rt - qw[:, :N]
        y0 = _dot3(q_k, vv) - qw[:, N:]
        bw = _dot3(bb, x, TN)
        gmat = jnp.where(eye_n, jnp.broadcast_to(p_last[:, sl], (N, N)), 0.0) - bw[:, :N]
        h_add = _dot3(kb, vv, TN) - bw[:, N:]
        h0 = h_scr[h]
        y = _dot3(r2, h0) + y0
        h_scr[h] = _dot3(gmat, h0) + h_add
        mean = jnp.mean(y, axis=-1, keepdims=True)
        yc = y - mean
        var = jnp.mean(yc * yc, axis=-1, keepdims=True)
        yn = yc * lax.rsqrt(var + GN_EPS)
        yn = yn * lnw_ref[:, sl] + lnb_ref[:, sl]
        bonus = jnp.sum(rkr[:, sl], axis=-1, keepdims=True) * vv
        o_ref[:, sl] = ((yn + bonus) * g_ref[:, sl]).astype(o_ref.dtype)

    @pl.when(c_idx == nchunks - 1)
    def _():
        hT_ref[0] = h_scr[...]


def _wkv(r, lw, k, v, kn, b, g, r_k, lnx_w, lnx_b, h0, *, nseq, T, C):
    M = r.shape[0]
    nchunks = T // C
    HN = MIX_A
    row = pl.BlockSpec((C, HN), lambda s, c: (s * nchunks + c, 0))
    vec = pl.BlockSpec((1, HN), lambda s, c: (0, 0))
    st = pl.BlockSpec((1, H_A, HEAD_A, HEAD_A), lambda s, c: (s, 0, 0, 0))
    return pl.pallas_call(
        functools.partial(_wkv_kernel, C=C, H=H_A, N=HEAD_A, nchunks=nchunks),
        out_shape=(jax.ShapeDtypeStruct((M, HN), BF16),
                   jax.ShapeDtypeStruct((nseq, H_A, HEAD_A, HEAD_A), F32)),
        grid=(nseq, nchunks),
        in_specs=[row] * 7 + [vec] * 3 + [st],
        out_specs=(row, st),
        scratch_shapes=[pltpu.VMEM((H_A, HEAD_A, HEAD_A), F32)],
        compiler_params=_cparams(2),
        name="wkv7",
    )(r, lw, k, v, kn, b, g, r_k.reshape(1, HN), lnx_w.reshape(1, HN), lnx_b.reshape(1, HN), h0)


def _mem_attn_kernel(q_ref, k_ref, v_ref, o_ref):
    q = q_ref[...]
    kk = k_ref[0].astype(BF16)
    vv = v_ref[0].astype(BF16)
    for h in range(MEM_HEADS):
        sl = slice(h * MEM_HEAD_DIM, (h + 1) * MEM_HEAD_DIM)
        s = _dg(q[:, sl], kk[:, sl], NT) * MEM_SCALE
        m = jnp.max(s, axis=-1, keepdims=True)
        p = jnp.exp(s - m)
        p = p / jnp.sum(p, axis=-1, keepdims=True)
        o_ref[:, sl] = _dg(p.astype(BF16), vv[:, sl]).astype(o_ref.dtype)


def _mem_attn(q, mem_k, mem_v, *, nb, T):
    M = q.shape[0]
    tq = _pick_tile(T, (512, 256, 128, 64, 32, 16, 8))
    nq = T // tq
    n_mem = mem_k.shape[1]
    return pl.pallas_call(
        _mem_attn_kernel,
        out_shape=jax.ShapeDtypeStruct((M, MEM_WIDTH), BF16),
        grid=(nb, nq),
        in_specs=[pl.BlockSpec((tq, MEM_WIDTH), lambda b, i: (b * nq + i, 0)),
                  pl.BlockSpec((1, n_mem, MEM_WIDTH), lambda b, i: (b, 0, 0)),
                  pl.BlockSpec((1, n_mem, MEM_WIDTH), lambda b, i: (b, 0, 0))],
        out_specs=pl.BlockSpec((tq, MEM_WIDTH), lambda b, i: (b * nq + i, 0)),
        compiler_params=_cparams(2),
        name="mem_attn",
    )(q, mem_k, mem_v)


def _ffn_kernel(*refs, tm, T, long_mode, col_chunks):
    if long_mode:
        (x_ref, g_ref, wg_ref, wv_ref, cw_ref, cb_ref, wd_ref, cp_ref, o_ref, tail_ref, xn_scr, carry_scr) = refs
    else:
        (x_ref, g_ref, wg_ref, wv_ref, cw_ref, cb_ref, wd_ref, e1_ref, e2_ref, o_ref, gate_ref, xn_scr) = refs
    xn_scr[...] = _rms(x_ref[...], g_ref[...]).astype(BF16)
    xn = xn_scr[...]
    row = lax.broadcasted_iota(jnp.int32, (tm, 1), 0)
    if long_mode:
        nblk = T // tm

        @pl.when(pl.program_id(0) % nblk == 0)
        def _():
            carry_scr[...] = cp_ref[0]
    else:
        t_in_seq = row % T
    acc = x_ref[...]
    for (lo, w) in col_chunks:
        cs = slice(lo, lo + w)
        gate = _dg(xn, wg_ref[:, cs])
        val = _dg(xn, wv_ref[:, cs])
        r1 = pltpu.roll(gate, 1, 0)
        r2 = pltpu.roll(gate, 2, 0)
        if long_mode:
            c0 = carry_scr[6:7, cs]
            c1 = carry_scr[7:8, cs]
            p1 = jnp.where(row == 0, c1, r1)
            p2 = jnp.where(row == 0, c0, jnp.where(row == 1, c1, r2))
            tail = gate[tm - 8:tm, :]
            carry_scr[:, cs] = tail
            tail_ref[0, :, cs] = tail
        else:
            p1 = jnp.where(t_in_seq >= 1, r1, 0.0) + e1_ref[:, cs]
            p2 = jnp.where(t_in_seq >= 2, r2, 0.0) + e2_ref[:, cs]
            gate_ref[:, cs] = gate
        conv = cb_ref[:, cs] + p2 * cw_ref[0:1, cs]
        conv = conv + p1 * cw_ref[1:2, cs]
        conv = conv + gate * cw_ref[2:3, cs]
        hmid = (conv * jax.nn.sigmoid(conv) * val).astype(BF16)
        acc = acc + _dg(hmid, wd_ref[cs, :])
    o_ref[...] = acc


def _ffn(x, g, w_gate, w_val, conv_w, conv_b, w_down, conv_prev, *, nseq, T):
    M = x.shape[0]
    long_mode = T >= 512
    tm = 512 if long_mode else 256
    assert (T % tm == 0) if long_mode else (tm % T == 0 and M % tm == 0)
    col_chunks = tuple((lo, min(256, D_FF - lo)) for lo in range(0, D_FF, 256))
    const = lambda a: pl.BlockSpec(a.shape, lambda i: (0,) * a.ndim, pipeline_mode=pl.Buffered(1))
    small = lambda a: pl.BlockSpec(a.shape, lambda i: (0,) * a.ndim)
    g2 = g.reshape(1, D_MODEL).astype(F32)
    cb2 = conv_b.reshape(1, D_FF).astype(F32)
    common_specs = [pl.BlockSpec((tm, D_MODEL), lambda i: (i, 0)), small(g2), const(w_gate), const(w_val),
                    small(conv_w), small(cb2), const(w_down)]
    common_args = [x, g2, w_gate, w_val, conv_w, cb2, w_down]
    kern = functools.partial(_ffn_kernel, tm=tm, T=T, long_mode=long_mode, col_chunks=col_chunks)
    if long_mode:
        nblk = T // tm
        cp = jnp.concatenate([jnp.zeros((nseq, 6, D_FF), F32), conv_prev.astype(F32)], axis=1)
        out, tails = pl.pallas_call(
            kern,
            out_shape=(jax.ShapeDtypeStruct((M, D_MODEL), F32), jax.ShapeDtypeStruct((M // tm, 8, D_FF), F32)),
            grid=(M // tm,),
            in_specs=common_specs + [pl.BlockSpec((1, 8, D_FF), lambda i: (i // nblk, 0, 0))],
            out_specs=(pl.BlockSpec((tm, D_MODEL), lambda i: (i, 0)),
                       pl.BlockSpec((1, 8, D_FF), lambda i: (i, 0, 0))),
            scratch_shapes=[pltpu.VMEM((tm, D_MODEL), BF16), pltpu.VMEM((8, D_FF), F32)],
            compiler_params=_cparams(1),
            name="convglu_long",
        )(*common_args, cp)
        new_conv = tails.reshape(nseq, nblk, 8, D_FF)[:, -1, 6:8, :]
        return out, new_conv
    cpf = conv_prev.astype(F32)
    zeros = jnp.zeros((nseq, T, D_FF), F32)
    e1 = zeros.at[:, 0].set(cpf[:, 1]).reshape(M, D_FF)
    e2 = zeros.at[:, 0].set(cpf[:, 0]).at[:, 1].set(cpf[:, 1]).reshape(M, D_FF)
    out, gate = pl.pallas_call(
        kern,
        out_shape=(jax.ShapeDtypeStruct((M, D_MODEL), F32), jax.ShapeDtypeStruct((M, D_FF), F32)),
        grid=(M // tm,),
        in_specs=common_specs + [pl.BlockSpec((tm, D_FF), lambda i: (i, 0))] * 2,
        out_specs=(pl.BlockSpec((tm, D_MODEL), lambda i: (i, 0)), pl.BlockSpec((tm, D_FF), lambda i: (i, 0))),
        scratch_shapes=[pltpu.VMEM((tm, D_MODEL), BF16)],
        compiler_params=_cparams(1),
        name="convglu_short",
    )(*common_args, e1, e2)
    new_conv = gate.reshape(nseq, T, D_FF)[:, T - 2:, :]
    return out, new_conv


def _rope128(x128, cs):
    a = x128 * cs
    s = a + pltpu.roll(a, 64, 1)
    lane = lax.broadcasted_iota(jnp.int32, a.shape, 1)
    return jnp.where(lane < 64, s, 0.0)


def _latent_kernel(h_ref, g_ref, cs_ref, ckv_ref, kpe_ref, ckpe_ref):
    ckv = _rms(h_ref[:, :KV_LORA], g_ref[...])
    rp = _rope128(h_ref[:, KV_LORA:KV_LORA + 128], cs_ref[...])
    ckv_ref[...] = ckv
    kpe_ref[...] = rp[:, :ROPE_DIM]
    ckpe_ref[:, :KV_LORA] = ckv.astype(BF16)
    ckpe_ref[:, KV_LORA:] = rp.astype(BF16)


def _latent_post(h, norm_ckv, cs, *, T):
    M = h.shape[0]
    tm = _pos_tile(M, T)
    ncs = cs.shape[0] // tm
    return pl.pallas_call(
        _latent_kernel,
        out_shape=(jax.ShapeDtypeStruct((M, KV_LORA), F32), jax.ShapeDtypeStruct((M, ROPE_DIM), F32),
                   jax.ShapeDtypeStruct((M, KV_LORA + 128), BF16)),
        grid=(M // tm,),
        in_specs=[pl.BlockSpec((tm, KV_LORA + 128), lambda i: (i, 0)),
                  pl.BlockSpec((1, KV_LORA), lambda i: (0, 0)),
                  pl.BlockSpec((tm, 128), lambda i: (i % ncs, 0))],
        out_specs=(pl.BlockSpec((tm, KV_LORA), lambda i: (i, 0)), pl.BlockSpec((tm, ROPE_DIM), lambda i: (i, 0)),
                   pl.BlockSpec((tm, KV_LORA + 128), lambda i: (i, 0))),
        compiler_params=_cparams(1),
        name="latent_post",
    )(h, norm_ckv.reshape(1, KV_LORA).astype(F32), cs)


def _qpost_kernel(q_ref, cs_ref, o_ref):
    cs = cs_ref[...]
    for h in range(H_B):
        base = h * 256
        o_ref[:, base:base + 128] = q_ref[:, base:base + 128].astype(o_ref.dtype)
        o_ref[:, base + 128:base + 256] = _rope128(q_ref[:, base + 128:base + 256], cs).astype(o_ref.dtype)


def _q_post(q, cs, *, T, out_dtype):
    M = q.shape[0]
    tm = _pos_tile(M, T)
    ncs = cs.shape[0] // tm
    return pl.pallas_call(
        _qpost_kernel,
        out_shape=jax.ShapeDtypeStruct((M, H_B * 256), out_dtype),
        grid=(M // tm,),
        in_specs=[pl.BlockSpec((tm, H_B * 256), lambda i: (i, 0)),
                  pl.BlockSpec((tm, 128), lambda i: (i % ncs, 0))],
        out_specs=pl.BlockSpec((tm, H_B * 256), lambda i: (i, 0)),
        compiler_params=_cparams(1),
        name="q_post",
    )(q, cs)


def _flash_kernel(q_ref, k_ref, v_ref, o_ref, m_scr, l_scr, acc_scr, *, tq, tk):
    i = pl.program_id(2)
    j = pl.program_id(3)

    @pl.when(j == 0)
    def _():
        m_scr[...] = jnp.full(m_scr.shape, -jnp.inf, F32)
        l_scr[...] = jnp.zeros(l_scr.shape, F32)
        acc_scr[...] = jnp.zeros(acc_scr.shape, F32)

    @pl.when(j <= i)
    def _():
        s = _dg(q_ref[...], k_ref[...], NT) * ATTN_SCALE
        rows = i * tq + lax.broadcasted_iota(jnp.int32, (tq, tk), 0)
        cols = j * tk + lax.broadcasted_iota(jnp.int32, (tq, tk), 1)
        s = jnp.where(cols <= rows, s, -jnp.inf)
        m_prev = m_scr[...]
        m_new = jnp.maximum(m_prev, jnp.max(s, axis=-1, keepdims=True))
        alpha = jnp.exp(m_prev - m_new)
        p = jnp.exp(s - m_new)
        l_scr[...] = alpha * l_scr[...] + jnp.sum(p, axis=-1, keepdims=True)
        acc_scr[...] = alpha * acc_scr[...] + _dg(p.astype(BF16), v_ref[...])
        m_scr[...] = m_new

    @pl.when(j == i)
    def _():
        o_ref[...] = (acc_scr[...] / l_scr[...]).astype(o_ref.dtype)


def _flash_causal(q, kv, *, nb, T):
    M = q.shape[0]
    tq = tk = 512
    nq = T // tq
    v_blk0 = (H_B * 256) // V_DIM
    return pl.pallas_call(
        functools.partial(_flash_kernel, tq=tq, tk=tk),
        out_shape=jax.ShapeDtypeStruct((M, H_B * V_DIM), BF16),
        grid=(nb, H_B, nq, nq),
        in_specs=[pl.BlockSpec((tq, 256), lambda b, h, i, j: (b * nq + i, h)),
                  pl.BlockSpec((tk, 256), lambda b, h, i, j: (b * nq + jnp.minimum(i, j), h)),
                  pl.BlockSpec((tk, V_DIM), lambda b, h, i, j: (b * nq + jnp.minimum(i, j), v_blk0 + h))],
        out_specs=pl.BlockSpec((tq, V_DIM), lambda b, h, i, j: (b * nq + i, h)),
        scratch_shapes=[pltpu.VMEM((tq, 1), F32), pltpu.VMEM((tq, 1), F32), pltpu.VMEM((tq, V_DIM), F32)],
        compiler_params=_cparams(4),
        name="mla_causal",
    )(q, kv, kv)


def _decode_kernel(pt_ref, q_ref, *refs, PP, n_steps, TS):
    ck_refs = refs[:PP]
    kp_refs = refs[PP:2 * PP]
    ckn_ref, kpn_ref, wuv_ref, o_ref, m_scr, l_scr, acc_scr = refs[2 * PP:]
    j = pl.program_id(1)
    R = q_ref.shape[1]

    @pl.when(j == 0)
    def _():
        m_scr[...] = jnp.full(m_scr.shape, -jnp.inf, F32)
        l_scr[...] = jnp.zeros(l_scr.shape, F32)
        acc_scr[...] = jnp.zeros(acc_scr.shape, F32)

    q_lat = q_ref[0, :, :KV_LORA]
    q_pe = q_ref[0, :, KV_LORA:]

    def update(s_list, v_list):
        m_prev = m_scr[...]
        m_new = m_prev
        for s in s_list:
            m_new = jnp.maximum(m_new, jnp.max(s, axis=-1, keepdims=True))
        alpha = jnp.exp(m_prev - m_new)
        l_new = alpha * l_scr[...]
        acc = alpha * acc_scr[...]
        for s, vv in zip(s_list, v_list):
            p = jnp.exp(s - m_new)
            l_new = l_new + jnp.sum(p, axis=-1, keepdims=True)
            acc = acc + _dg(p.astype(BF16), vv)
        m_scr[...] = m_new
        l_scr[...] = l_new
        acc_scr[...] = acc

    s_list, v_list = [], []
    for p in range(PP):
        ck = ck_refs[p][0].astype(BF16)
        kp = kp_refs[p][0].astype(BF16)
        s_list.append((_dg(q_lat, ck, NT) + _dg(q_pe, kp, NT)) * ATTN_SCALE)
        v_list.append(ck)
    update(s_list, v_list)

    @pl.when(j == n_steps - 1)
    def _():
        pad = PAGE_SIZE - TS
        ckn = jnp.concatenate([ckn_ref[0], jnp.zeros((pad, KV_LORA), F32)], axis=0).astype(BF16)
        kpn = jnp.concatenate([kpn_ref[0], jnp.zeros((pad, ROPE_DIM), F32)], axis=0).astype(BF16)
        s = (_dg(q_lat, ckn, NT) + _dg(q_pe, kpn, NT)) * ATTN_SCALE
        t_q = lax.broadcasted_iota(jnp.int32, (R, PAGE_SIZE), 0) % TS
        col = lax.broadcasted_iota(jnp.int32, (R, PAGE_SIZE), 1)
        s = jnp.where(col <= t_q, s, -jnp.inf)
        update([s], [ckn])
        o_lat = acc_scr[...] / l_scr[...]
        for h in range(H_B):
            o_h = o_lat[h * TS:(h + 1) * TS, :].astype(BF16)
            o_ref[0, :, h * V_DIM:(h + 1) * V_DIM] = _dg(o_h, wuv_ref[h]).astype(o_ref.dtype)


def _decode_attn(page_table, qd, cache_ckv, cache_kpe, ckv_new, kpe_new, w_uv, *, PP=8):
    nb, n_pages = page_table.shape
    TS = ckv_new.shape[1]
    R = qd.shape[1]
    n_steps = n_pages // PP
    pt = page_table.reshape(-1).astype(jnp.int32)

    def page_spec(width, p):
        return pl.BlockSpec((1, PAGE_SIZE, width), lambda b, j, pt_ref: (pt_ref[b * n_pages + j * PP + p], 0, 0))

    in_specs = ([pl.BlockSpec((1, R, KV_LORA + ROPE_DIM), lambda b, j, pt_ref: (b, 0, 0))]
                + [page_spec(KV_LORA, p) for p in range(PP)]
                + [page_spec(ROPE_DIM, p) for p in range(PP)]
                + [pl.BlockSpec((1, TS, KV_LORA), lambda b, j, pt_ref: (b, 0, 0)),
                   pl.BlockSpec((1, TS, ROPE_DIM), lambda b, j, pt_ref: (b, 0, 0)),
                   pl.BlockSpec((H_B, KV_LORA, V_DIM), lambda b, j, pt_ref: (0, 0, 0))])
    grid_spec = pltpu.PrefetchScalarGridSpec(
        num_scalar_prefetch=1,
        grid=(nb, n_steps),
        in_specs=in_specs,
        out_specs=pl.BlockSpec((1, TS, H_B * V_DIM), lambda b, j, pt_ref: (b, 0, 0)),
        scratch_shapes=[pltpu.VMEM((R, 1), F32), pltpu.VMEM((R, 1), F32), pltpu.VMEM((R, KV_LORA), F32)],
    )
    return pl.pallas_call(
        functools.partial(_decode_kernel, PP=PP, n_steps=n_steps, TS=TS),
        out_shape=jax.ShapeDtypeStruct((nb, TS, H_B * V_DIM), BF16),
        grid_spec=grid_spec,
        compiler_params=_cparams(2),
        name="mla_decode",
    )(pt, qd, *([cache_ckv] * PP), *([cache_kpe] * PP), ckv_new, kpe_new, w_uv)


def _rope_table(pos):
    half = ROPE_DIM // 2
    inv = ROPE_BASE ** (-jnp.arange(half, dtype=F32) / half)
    ang = pos.astype(F32)[:, None] * inv[None, :]
    cos, sin = jnp.cos(ang), jnp.sin(ang)
    return jnp.concatenate([cos, cos, -sin, sin], axis=-1)


def _swap_halves(w):
    half = ROPE_DIM // 2
    return jnp.concatenate([w[..., half:], w[..., :half]], axis=-1)


def _prep_rwkv_weights(l, P):
    w_in = P["w_in_a"][l]
    mu = P["mu_a"][l]
    cuts = [0, MIX_A, MIX_A + DECAY_LORA, 2 * MIX_A + DECAY_LORA, 3 * MIX_A + DECAY_LORA,
            3 * MIX_A + DECAY_LORA + A_LORA, C_RWKV, C_A]
    seg = lambda a, i: a[..., cuts[i]:cuts[i + 1]]
    r_w, wl_w, k_w, v_w, al_w, gl_w, qm_w = [seg(w_in, i) for i in range(7)]
    r_m, wl_m, k_m, v_m, al_m, gl_m = [seg(mu, i) for i in range(6)]
    zc = lambda n: jnp.zeros((D_MODEL, n), F32)
    zm = lambda n: jnp.zeros((n,), F32)
    if l > 0:
        vl_w, vl_m = P["w_vres_in"][l - 1], P["mu_vres"][l - 1]
        wv_up, v0 = P["w_vres_up"][l - 1], P["v0"][l - 1]
    else:
        vl_w, vl_m = zc(VRES_LORA), zm(VRES_LORA)
        wv_up, v0 = jnp.zeros((VRES_LORA, MIX_A), F32), zm(MIX_A)
    w_full = jnp.concatenate([r_w, k_w, v_w, wl_w, al_w, gl_w, vl_w, zc(64), qm_w, zc(128)], axis=1)
    mu_full = jnp.concatenate([r_m, k_m, v_m, wl_m, al_m, gl_m, vl_m, zm(64), zm(MEM_WIDTH), zm(128)])
    zr = lambda n: jnp.zeros((n, MIX_A), F32)
    return dict(
        w_in=w_full.astype(BF16),
        mu=mu_full.reshape(1, PA_COLS),
        wd=jnp.concatenate([P["w_decay_up"][l], zr(A_LORA)], axis=0).astype(BF16),
        wa=jnp.concatenate([zr(DECAY_LORA), P["w_a_up"][l]], axis=0).astype(BF16),
        wg=jnp.concatenate([P["w_g_up"][l], zr(256 - GATE_LORA)], axis=0).astype(BF16),
        wv=jnp.concatenate([zr(GATE_LORA), wv_up, zr(256 - GATE_LORA - VRES_LORA)], axis=0).astype(BF16),
        w0=P["w0"][l].reshape(1, MIX_A), a0=P["a0"][l].reshape(1, MIX_A), v0=v0.reshape(1, MIX_A),
        k_k=P["k_k"][l].reshape(1, MIX_A), k_a=P["k_a"][l].reshape(1, MIX_A),
    )


def _prep_mla_weights(P):
    w_kv_a = P["w_kv_a"]
    w_kva = jnp.concatenate([w_kv_a, _swap_halves(w_kv_a[:, KV_LORA:])], axis=1).astype(BF16)
    w_kv_b = P["w_kv_b"]
    w_uk, w_uv = w_kv_b[..., :NOPE_DIM], w_kv_b[..., NOPE_DIM:]
    k_part = jnp.zeros((KV_LORA + 128, H_B, 256), F32)
    k_part = k_part.at[:KV_LORA, :, :NOPE_DIM].set(w_uk)
    eye = jnp.eye(ROPE_DIM, dtype=F32)
    k_part = k_part.at[KV_LORA:KV_LORA + ROPE_DIM, :, NOPE_DIM:NOPE_DIM + ROPE_DIM].set(
        jnp.broadcast_to(eye[:, None, :], (ROPE_DIM, H_B, ROPE_DIM)))
    v_part = jnp.zeros((KV_LORA + 128, H_B, V_DIM), F32).at[:KV_LORA].set(w_uv)
    w_kvx = jnp.concatenate([k_part.reshape(KV_LORA + 128, H_B * 256), v_part.reshape(KV_LORA + 128, H_B * V_DIM)],
                            axis=1).astype(BF16)
    w_q = []
    for j in range(N_B):
        wq = P["w_q_b"][j].reshape(Q_LORA, H_B, NOPE_DIM + ROPE_DIM)
        rope_w = wq[..., NOPE_DIM:]
        w_q.append(jnp.concatenate([wq[..., :NOPE_DIM], rope_w, _swap_halves(rope_w)], axis=-1)
                   .reshape(Q_LORA, H_B * 256).astype(BF16))
    return dict(
        w_kva=w_kva, w_kvx=w_kvx, w_q=w_q,
        w_ukT=jnp.transpose(w_uk, (1, 2, 0)).astype(BF16),
        w_uv=jnp.transpose(w_uv, (1, 0, 2)).astype(BF16),
    )


def _trunk(x, pos, mem_k, mem_v, wkv0, shift0, conv0, past, P, W, *, nseq, T):
    M = nseq * T
    x = x.reshape(M, D_MODEL)
    cs = _rope_table(pos)
    if T < ROW_TILES[0]:
        cs = jnp.tile(cs, (nseq, 1))
    chunk = 64 if T >= 64 else T
    new_wkv, new_shift, new_conv = [], [], []
    v_first = None
    ckv = kpe = ckpe = kvx = None
    for l in range(DEPTH):
        g_mix = P["norm_mix"][l]
        if l < N_A:
            pw = W["rwkv"][l]
            proj = _mm([x], [pw["w_in"]], g=g_mix, name="rwkv_in")
            x_last = x.reshape(nseq, T, D_MODEL)[:, -1]
            new_shift.append(_rmsnorm(x_last, g_mix, name="shift_norm"))
            sp = _mm([shift0[l].astype(F32)], [pw["w_in"]], name="rwkv_in_shift")
            prev = jnp.concatenate([sp[:, None, :], proj.reshape(nseq, T, PA_COLS)[:, :-1]], axis=1).reshape(M, PA_COLS)
            r, lw, k, v, kn, b, g, q_mem = _rwkv_prep(proj, prev, pw, v_first if l > 0 else None)
            if l == 0:
                v_first = v
            h0 = jnp.swapaxes(wkv0[l].astype(F32), -1, -2)
            o_tok, h_new = _wkv(r, lw, k, v, kn, b, g, P["r_k"][l], P["lnx_w"][l], P["lnx_b"][l], h0,
                                nseq=nseq, T=T, C=chunk)
            new_wkv.append(jnp.swapaxes(h_new, -1, -2))
        else:
            j = l - N_A
            if l == N_A:
                hkv = _mm([x], [W["mla"]["w_kva"]], g=P["norm_kv"], name="kv_a")
                ckv, kpe, ckpe = _latent_post(hkv, P["norm_ckv"], cs, T=T)
                if past is None:
                    kvx = _mm([ckpe], [W["mla"]["w_kvx"]], out_dtype=BF16, name="kv_expand")
            proj = _mm([x], [W["w_in_b"][j]], g=g_mix, name="mla_in")
            qf = _mm([proj], [W["mla"]["w_q"][j]], g=P["norm_q"][j], name="q_b")
            q_mem = proj[:, Q_LORA:].astype(BF16)
            if past is None:
                q = _q_post(qf, cs, T=T, out_dtype=BF16)
                o_tok = _flash_causal(q, kvx, nb=nseq, T=T)
            else:
                q = _q_post(qf, cs, T=T, out_dtype=BF16).reshape(M, H_B, 256)
                q_lat = jnp.stack([_mm([q[:, h, :NOPE_DIM]], [W["mla"]["w_ukT"][h]], out_dtype=BF16, name="q_absorb")
                                   for h in range(H_B)], axis=1)
                qd = jnp.concatenate([q_lat, q[:, :, NOPE_DIM:NOPE_DIM + ROPE_DIM]], axis=-1)
                qd = qd.reshape(nseq, T, H_B, KV_LORA + ROPE_DIM).transpose(0, 2, 1, 3).reshape(nseq, H_B * T, -1)
                o_tok = _decode_attn(past[2], qd, past[0], past[1], ckv.reshape(nseq, T, KV_LORA),
                                     kpe.reshape(nseq, T, ROPE_DIM), W["mla"]["w_uv"]).reshape(M, H_B * V_DIM)
        o_mem = _mem_attn(q_mem, mem_k[l], mem_v[l], nb=nseq, T=T)
        x = _mm([o_tok, o_mem], [W["w_o_tok"][l], W["w_o_mem"][l]], res=x, name="w_o")
        x, c_new = _ffn(x, P["norm_ffn"][l], W["w_gate"][l], W["w_val"][l], P["conv_w"][l], P["conv_b"][l],
                        W["w_down"][l], conv0[l], nseq=nseq, T=T)
        new_conv.append(c_new)
    y = _rmsnorm(x, P["final_norm"], name="final_norm")
    return (y.reshape(nseq, T, D_MODEL), ckv.reshape(nseq, T, KV_LORA), kpe.reshape(nseq, T, ROPE_DIM),
            jnp.stack(new_wkv), jnp.stack(new_shift), jnp.stack(new_conv))


def kernel(x_prompt, x_sample, cache_ckv, cache_kpe, cache_mem_k, cache_mem_v, state_wkv, state_shift, state_conv, page_table, mem_prompt, norm_mix, norm_ffn, norm_mem, w_mem_kv, w_o, w_in_a, mu_a, w_vres_in, mu_vres, w_decay_up, w0, w_a_up, a0, w_g_up, w_vres_up, v0, k_k, k_a, r_k, lnx_w, lnx_b, norm_kv, w_kv_a, norm_ckv, w_kv_b, w_in_b, norm_q, w_q_b, w_ffn_up, conv_w, conv_b, w_ffn_down, final_norm):
    P = dict(norm_mix=norm_mix, norm_ffn=norm_ffn, w_o=w_o, w_in_a=w_in_a, mu_a=mu_a, w_vres_in=w_vres_in,
             mu_vres=mu_vres, w_decay_up=w_decay_up, w0=w0, w_a_up=w_a_up, a0=a0, w_g_up=w_g_up,
             w_vres_up=w_vres_up, v0=v0, k_k=k_k, k_a=k_a, r_k=r_k, lnx_w=lnx_w, lnx_b=lnx_b,
             norm_kv=norm_kv, w_kv_a=w_kv_a, norm_ckv=norm_ckv, w_kv_b=w_kv_b, w_in_b=w_in_b, norm_q=norm_q,
             w_q_b=w_q_b, conv_w=conv_w, conv_b=conv_b, final_norm=final_norm)
    W = dict(
        rwkv=[_prep_rwkv_weights(l, P) for l in range(N_A)],
        mla=_prep_mla_weights(P),
        w_in_b=[w_in_b[j].astype(BF16) for j in range(N_B)],
        w_o_tok=[w_o[l, :MIX_A].astype(BF16) for l in range(DEPTH)],
        w_o_mem=[w_o[l, MIX_A:].astype(BF16) for l in range(DEPTH)],
        w_gate=[w_ffn_up[l, :, :D_FF].astype(BF16) for l in range(DEPTH)],
        w_val=[w_ffn_up[l, :, D_FF:].astype(BF16) for l in range(DEPTH)],
        w_down=[w_ffn_down[l].astype(BF16) for l in range(DEPTH)],
    )
    dt = x_prompt.dtype
    B, T = x_prompt.shape[:2]
    n_mem = mem_prompt.shape[1]
    mem2d = mem_prompt.reshape(B * n_mem, D_MODEL)
    mem_k_p, mem_v_p = [], []
    for l in range(DEPTH):
        kv = _mm([mem2d], [w_mem_kv[l].astype(BF16)], g=norm_mem[l], name="mem_kv")
        mem_k_p.append(kv[:, :MEM_WIDTH].reshape(B, n_mem, MEM_WIDTH))
        mem_v_p.append(kv[:, MEM_WIDTH:].reshape(B, n_mem, MEM_WIDTH))
    y_p, ckv_p, kpe_p, wkv_p, shift_p, conv_p = _trunk(
        x_prompt, jnp.arange(T), mem_k_p, mem_v_p,
        jnp.zeros((N_A, B, H_A, HEAD_A, HEAD_A), dt), jnp.zeros((N_A, B, D_MODEL), dt),
        jnp.zeros((DEPTH, B, CONV_W - 1, D_FF), dt), None, P, W, nseq=B, T=T)
    DB, TS = x_sample.shape[:2]
    past_len = page_table.shape[1] * PAGE_SIZE
    mk_s = [cache_mem_k[l].reshape(DB, n_mem, MEM_WIDTH) for l in range(DEPTH)]
    mv_s = [cache_mem_v[l].reshape(DB, n_mem, MEM_WIDTH) for l in range(DEPTH)]
    y_s, ckv_s, kpe_s, wkv_s, shift_s, conv_s = _trunk(
        x_sample, past_len + jnp.arange(TS), mk_s, mv_s, state_wkv, state_shift, state_conv,
        (cache_ckv, cache_kpe, page_table), P, W, nseq=DB, T=TS)
    mem_k_out = jnp.stack(mem_k_p).reshape(DEPTH, B, n_mem, MEM_HEADS, MEM_HEAD_DIM)
    mem_v_out = jnp.stack(mem_v_p).reshape(DEPTH, B, n_mem, MEM_HEADS, MEM_HEAD_DIM)
    return (y_p, y_s, ckv_p, kpe_p, mem_k_out, mem_v_out, wkv_p, shift_p, conv_p,
            ckv_s, kpe_s, wkv_s, shift_s, conv_s)
```

```python
import functools
import math

import jax
import jax.numpy as jnp
from jax import lax
from jax.experimental import pallas as pl
from jax.experimental.pallas import tpu as pltpu

F32 = jnp.float32
BF16 = jnp.bfloat16

D_MODEL = 1024
DEPTH = 4
N_A = 2
N_B = 2
MEM_WIDTH = 256
MIX_A = 768
HEAD_A = 64
H_A = 12
DECAY_LORA = 64
A_LORA = 64
VRES_LORA = 32
GATE_LORA = 160
GN_EPS = 64e-5
C_RWKV = 3 * MIX_A + DECAY_LORA + A_LORA + GATE_LORA
C_A = C_RWKV + MEM_WIDTH
MEM_HEADS = 4
MEM_HEAD_DIM = 64
MEM_SCALE = MEM_HEAD_DIM ** -0.5
NOPE_DIM = 128
ROPE_DIM = 64
V_DIM = 128
H_B = 6
Q_LORA = 256
KV_LORA = 256
ROPE_BASE = 10000.0
ATTN_SCALE = (NOPE_DIM + ROPE_DIM) ** -0.5
D_FF = 2816
CONV_W = 3
RMS_EPS = 1e-6
PAGE_SIZE = 128

PA_R, PA_K, PA_V = 0, 768, 1536
PA_LORA = 2304
PA_GATE = 2432
PA_QMEM = 2688
PA_COLS = 3072

VMEM_LIMIT_BYTES = 52 * 1024 * 1024

NN = ((1,), (0,))
NT = ((1,), (1,))
TN = ((0,), (0,))


def _dg(a, b, dims=NN):
    return lax.dot_general(a, b, (dims, ((), ())), preferred_element_type=F32)


def _split(x):
    hi = x.astype(BF16)
    lo = (x - hi.astype(F32)).astype(BF16)
    return hi, lo


def _dot3(a, b, dims=NN):
    ah, al = _split(a)
    bh, bl = _split(b)
    return _dg(ah, bh, dims) + (_dg(ah, bl, dims) + _dg(al, bh, dims))


def _cparams(n_axes):
    return pltpu.CompilerParams(dimension_semantics=("arbitrary",) * n_axes,
                                vmem_limit_bytes=VMEM_LIMIT_BYTES)


def _rms(x, g, eps=RMS_EPS):
    return x * lax.rsqrt(jnp.mean(x * x, axis=-1, keepdims=True) + eps) * g


def _pick_tile(n, candidates):
    for c in candidates:
        if n % c == 0:
            return c
    return n


ROW_TILES = (512, 256, 128, 64, 32, 16, 8)


def _pos_tile(M, T):
    return _pick_tile(T, ROW_TILES) if T >= ROW_TILES[0] else _pick_tile(M, ROW_TILES)


def _mm_kernel(*refs, n_in, has_norm, has_res, f32_in):
    xs = refs[:n_in]
    ws = refs[n_in:2 * n_in]
    k = 2 * n_in
    g_ref = refs[k] if has_norm else None
    k += int(has_norm)
    r_ref = refs[k] if has_res else None
    k += int(has_res)
    o_ref = refs[k]
    scr = list(refs[k + 1:])

    lhs = []
    si = 0
    for idx in range(n_in):
        if f32_in[idx]:
            s_ref = scr[si]
            si += 1

            @pl.when(pl.program_id(1) == 0)
            def _(x_ref=xs[idx], s_ref=s_ref, idx=idx):
                x = x_ref[...]
                if has_norm and idx == 0:
                    x = _rms(x, g_ref[...])
                s_ref[...] = x.astype(BF16)

            lhs.append(s_ref)
        else:
            lhs.append(xs[idx])
    acc = _dg(lhs[0][...], ws[0][...])
    for idx in range(1, n_in):
        acc = acc + _dg(lhs[idx][...], ws[idx][...])
    if has_res:
        acc = acc + r_ref[...]
    o_ref[...] = acc.astype(o_ref.dtype)


def _mm(xs, ws, *, g=None, res=None, out_dtype=F32, x_col_blocks=None, ks=None, name="mm"):
    n_in = len(xs)
    M = xs[0].shape[0]
    N = ws[0].shape[1]
    ks = [w.shape[0] for w in ws]
    x_col_blocks = x_col_blocks or [0] * n_in
    tm = _pick_tile(M, (512, 256, 128, 64, 32, 16, 8))
    tn = _pick_tile(N, (1024, 768, 512, 384, 256, 128))
    f32_in = tuple(x.dtype == F32 for x in xs)
    in_specs = []
    for x, kk, cb in zip(xs, ks, x_col_blocks):
        in_specs.append(pl.BlockSpec((tm, kk), lambda i, j, cb=cb: (i, cb)))
    for w, kk in zip(ws, ks):
        in_specs.append(pl.BlockSpec((kk, tn), lambda i, j: (0, j)))
    args = list(xs) + list(ws)
    if g is not None:
        in_specs.append(pl.BlockSpec((1, ks[0]), lambda i, j: (0, 0)))
        args.append(g.reshape(1, ks[0]).astype(F32))
    if res is not None:
        in_specs.append(pl.BlockSpec((tm, tn), lambda i, j: (i, j)))
        args.append(res)
    scratch = [pltpu.VMEM((tm, kk), BF16) for kk, f in zip(ks, f32_in) if f]
    return pl.pallas_call(
        functools.partial(_mm_kernel, n_in=n_in, has_norm=g is not None, has_res=res is not None,
                          f32_in=f32_in),
        out_shape=jax.ShapeDtypeStruct((M, N), out_dtype),
        grid=(M // tm, N // tn),
        in_specs=in_specs,
        out_specs=pl.BlockSpec((tm, tn), lambda i, j: (i, j)),
        scratch_shapes=scratch,
        compiler_params=_cparams(2),
        name=name,
    )(*args)


def _rmsnorm_kernel(x_ref, g_ref, o_ref):
    o_ref[...] = _rms(x_ref[...], g_ref[...])


def _rmsnorm(x, g, name="rmsnorm"):
    M, D = x.shape
    tm = _pick_tile(M, (512, 256, 128, 64, 32, 16, 8))
    return pl.pallas_call(
        _rmsnorm_kernel,
        out_shape=jax.ShapeDtypeStruct((M, D), F32),
        grid=(M // tm,),
        in_specs=[pl.BlockSpec((tm, D), lambda i: (i, 0)), pl.BlockSpec((1, D), lambda i: (0, 0))],
        out_specs=pl.BlockSpec((tm, D), lambda i: (i, 0)),
        compiler_params=_cparams(1),
        name=name,
    )(x, g.reshape(1, D).astype(F32))


def _segsum64(x):
    r = lax.broadcasted_iota(jnp.int32, (128, 128), 0) >> 6
    c = lax.broadcasted_iota(jnp.int32, (128, 128), 1) >> 6
    ones = jnp.where(r == c, 1.0, 0.0).astype(BF16)
    outs = []
    for j in range(x.shape[1] // 128):
        hi, lo = _split(x[:, j * 128:(j + 1) * 128])
        outs.append(_dg(hi, ones) + _dg(lo, ones))
    return jnp.concatenate(outs, axis=1)


def _softplus(x):
    return jnp.maximum(x, 0.0) + jnp.log1p(jnp.exp(-jnp.abs(x)))


def _prep_kernel(*refs, has_vres):
    (cur_ref, prev_ref, mu_ref, wd_ref, wa_ref, wg_ref, wv_ref, w0_ref, a0_ref, v0_ref, kk_ref, ka_ref) = refs[:12]
    k = 12
    vf_ref = refs[k] if has_vres else None
    k += int(has_vres)
    r_o, lw_o, k_o, v_o, kn_o, b_o, g_o, qm_o = refs[k:k + 8]

    def mix(lo, hi):
        c = cur_ref[:, lo:hi]
        return c + (prev_ref[:, lo:hi] - c) * mu_ref[:, lo:hi]

    lora = mix(PA_LORA, PA_LORA + 128)
    zw = _dg(jnp.tanh(lora).astype(BF16), wd_ref[...]) + w0_ref[...]
    w_log = -_softplus(-zw) - 0.5
    lw_o[...] = -jnp.exp(w_log)
    a = jax.nn.sigmoid(_dg(lora.astype(BF16), wa_ref[...]) + a0_ref[...])
    gin = mix(PA_GATE, PA_GATE + 256)
    g_o[...] = _dg(jax.nn.sigmoid(gin).astype(BF16), wg_ref[...])
    r_o[...] = mix(PA_R, PA_R + MIX_A)
    kx = mix(PA_K, PA_K + MIX_A)
    v = mix(PA_V, PA_V + MIX_A)
    if has_vres:
        sv = jax.nn.sigmoid(_dg(gin.astype(BF16), wv_ref[...]) + v0_ref[...])
        v = v + (vf_ref[...] - v) * sv
    v_o[...] = v
    kk = kx * kk_ref[...]
    kk = kk * lax.rsqrt(jnp.maximum(_segsum64(kk * kk), 1e-24))
    kn_o[...] = kk
    b_o[...] = kk * a
    k_o[...] = kx * (1.0 + (a - 1.0) * ka_ref[...])
    qm_o[...] = cur_ref[:, PA_QMEM:PA_QMEM + MEM_WIDTH].astype(BF16)


def _rwkv_prep(cur, prev, pw, v_first):
    M = cur.shape[0]
    tm = _pick_tile(M, (256, 128, 64, 32, 16, 8))
    has_vres = v_first is not None
    row = lambda w: pl.BlockSpec((tm, w), lambda i: (i, 0))
    full = lambda a: pl.BlockSpec(a.shape, lambda i: (0,) * a.ndim)
    consts = [pw["mu"], pw["wd"], pw["wa"], pw["wg"], pw["wv"], pw["w0"], pw["a0"], pw["v0"], pw["k_k"], pw["k_a"]]
    args = [cur, prev] + consts
    in_specs = [row(PA_COLS), row(PA_COLS)] + [full(a) for a in consts]
    if has_vres:
        args.append(v_first)
        in_specs.append(row(MIX_A))
    outs = [jax.ShapeDtypeStruct((M, MIX_A), F32)] * 7 + [jax.ShapeDtypeStruct((M, MEM_WIDTH), BF16)]
    out_specs = [row(MIX_A)] * 7 + [row(MEM_WIDTH)]
    return pl.pallas_call(
        functools.partial(_prep_kernel, has_vres=has_vres),
        out_shape=outs,
        grid=(M // tm,),
        in_specs=in_specs,
        out_specs=out_specs,
        compiler_params=_cparams(1),
        name="rwkv_prep",
    )(*args)


def _wkv_kernel(r_ref, lw_ref, k_ref, v_ref, kn_ref, b_ref, g_ref, rk_ref, lnw_ref, lnb_ref, h0_ref,
                o_ref, hT_ref, h_scr, *, C, H, N, nchunks):
    c_idx = pl.program_id(1)

    @pl.when(c_idx == 0)
    def _():
        h_scr[...] = h0_ref[0]

    lw = lw_ref[...]
    ti = lax.broadcasted_iota(jnp.int32, (C, C), 0)
    tj = lax.broadcasted_iota(jnp.int32, (C, C), 1)
    ltri = jnp.where(tj <= ti, 1.0, 0.0).astype(BF16)
    lw_hi, lw_lo = _split(lw)
    cum = _dg(ltri, lw_hi) + _dg(ltri, lw_lo)
    cum_last = jnp.sum(lw, axis=0, keepdims=True)
    e_neg = jnp.exp(-cum)
    r_all = r_ref[...]
    k_all = k_ref[...]
    v_all = v_ref[...]
    kt_all = kn_ref[...] * jnp.exp(cum - lw)
    rt_all = r_all * jnp.exp(cum)
    kh_all = k_all * e_neg
    bh_all = b_ref[...] * e_neg
    e_rem = jnp.exp(cum_last - cum)
    kb_all = k_all * e_rem
    bb_all = b_ref[...] * e_rem
    p_last = jnp.exp(cum_last)
    rkr = r_all * k_all * rk_ref[...]

    strict = tj < ti
    incl = tj <= ti
    ni = lax.broadcasted_iota(jnp.int32, (N, N), 0)
    nj = lax.broadcasted_iota(jnp.int32, (N, N), 1)
    eye_n = ni == nj
    eye_c = jnp.where(ti == tj, 1.0, 0.0)

    hs = range(H)
    sls = [slice(h * N, (h + 1) * N) for h in hs]
    cut = lambda a: [a[:, sl].astype(BF16) for sl in sls]
    kt, rt, kh, bh, kb, bb, vv = (cut(a) for a in (kt_all, rt_all, kh_all, bh_all, kb_all, bb_all, v_all))
    a_k = [jnp.where(strict, _dg(kt[h], kh[h], NT), 0.0).astype(BF16) for h in hs]
    a_b = [jnp.where(strict, _dg(kt[h], bh[h], NT), 0.0) for h in hs]
    q_k = [jnp.where(incl, _dg(rt[h], kh[h], NT), 0.0).astype(BF16) for h in hs]
    q_b = [jnp.where(incl, _dg(rt[h], bh[h], NT), 0.0).astype(BF16) for h in hs]
    tinv = [eye_c - jnp.where(((ti >> 1) == (tj >> 1)), a_b[h], 0.0) for h in hs]
    blk = 2
    while blk < C:
        sh = blk.bit_length() - 1
        lvl = ((ti >> (sh + 1)) == (tj >> (sh + 1))) & (((ti >> sh) & 1) == 1) & (((tj >> sh) & 1) == 0)
        t16 = [tinv[h].astype(BF16) for h in hs]
        mt = [_dg(jnp.where(lvl, a_b[h], 0.0).astype(BF16), t16[h]).astype(BF16) for h in hs]
        tinv = [tinv[h] - _dg(t16[h], mt[h]) for h in hs]
        blk *= 2
    t16 = [tinv[h].astype(BF16) for h in hs]
    akv = [_dg(a_k[h], vv[h]).astype(BF16) for h in hs]
    x = [_dg(t16[h], jnp.concatenate([kt[h], akv[h]], axis=1)).astype(BF16) for h in hs]
    qw = [_dg(q_b[h], x[h]) for h in hs]
    r2 = [(rt_all[:, sls[h]] - qw[h][:, :N]).astype(BF16) for h in hs]
    y0 = [_dg(q_k[h], vv[h]) - qw[h][:, N:] for h in hs]
    bw = [_dg(bb[h], x[h], TN) for h in hs]
    gmat = [(jnp.where(eye_n, jnp.broadcast_to(p_last[:, sls[h]], (N, N)), 0.0) - bw[h][:, :N]).astype(BF16) for h in hs]
    h_add = [_dg(kb[h], vv[h], TN) - bw[h][:, N:] for h in hs]
    h0 = [h_scr[h].astype(BF16) for h in hs]
    y = [_dg(r2[h], h0[h]) + y0[h] for h in hs]
    for h in hs:
        h_scr[h] = _dg(gmat[h], h0[h]) + h_add[h]
    for h in hs:
        sl = sls[h]
        mean = jnp.mean(y[h], axis=-1, keepdims=True)
        yc = y[h] - mean
        var = jnp.mean(yc * yc, axis=-1, keepdims=True)
        yn = yc * lax.rsqrt(var + GN_EPS)
        yn = yn * lnw_ref[:, sl] + lnb_ref[:, sl]
        bonus = jnp.sum(rkr[:, sl], axis=-1, keepdims=True) * v_all[:, sl]
        o_ref[:, sl] = ((yn + bonus) * g_ref[:, sl]).astype(o_ref.dtype)

    @pl.when(c_idx == nchunks - 1)
    def _():
        hT_ref[0] = h_scr[...]


def _wkv(r, lw, k, v, kn, b, g, r_k, lnx_w, lnx_b, h0, *, nseq, T, C):
    M = r.shape[0]
    nchunks = T // C
    HN = MIX_A
    row = pl.BlockSpec((C, HN), lambda s, c: (s * nchunks + c, 0))
    vec = pl.BlockSpec((1, HN), lambda s, c: (0, 0))
    st = pl.BlockSpec((1, H_A, HEAD_A, HEAD_A), lambda s, c: (s, 0, 0, 0))
    return pl.pallas_call(
        functools.partial(_wkv_kernel, C=C, H=H_A, N=HEAD_A, nchunks=nchunks),
        out_shape=(jax.ShapeDtypeStruct((M, HN), BF16),
                   jax.ShapeDtypeStruct((nseq, H_A, HEAD_A, HEAD_A), F32)),
        grid=(nseq, nchunks),
        in_specs=[row] * 7 + [vec] * 3 + [st],
        out_specs=(row, st),
        scratch_shapes=[pltpu.VMEM((H_A, HEAD_A, HEAD_A), F32)],
        compiler_params=_cparams(2),
        name="wkv7",
    )(r, lw, k, v, kn, b, g, r_k.reshape(1, HN), lnx_w.reshape(1, HN), lnx_b.reshape(1, HN), h0)


def _mem_attn_kernel(q_ref, k_ref, v_ref, o_ref):
    q = q_ref[...]
    kk = k_ref[0].astype(BF16)
    vv = v_ref[0].astype(BF16)
    for h in range(MEM_HEADS):
        sl = slice(h * MEM_HEAD_DIM, (h + 1) * MEM_HEAD_DIM)
        s = _dg(q[:, sl], kk[:, sl], NT) * MEM_SCALE
        m = jnp.max(s, axis=-1, keepdims=True)
        p = jnp.exp(s - m)
        p = p / jnp.sum(p, axis=-1, keepdims=True)
        o_ref[:, sl] = _dg(p.astype(BF16), vv[:, sl]).astype(o_ref.dtype)


def _mem_attn(q, mem_k, mem_v, *, nb, T):
    M = q.shape[0]
    tq = _pick_tile(T, (512, 256, 128, 64, 32, 16, 8))
    nq = T // tq
    n_mem = mem_k.shape[1]
    return pl.pallas_call(
        _mem_attn_kernel,
        out_shape=jax.ShapeDtypeStruct((M, MEM_WIDTH), BF16),
        grid=(nb, nq),
        in_specs=[pl.BlockSpec((tq, MEM_WIDTH), lambda b, i: (b * nq + i, 0)),
                  pl.BlockSpec((1, n_mem, MEM_WIDTH), lambda b, i: (b, 0, 0)),
                  pl.BlockSpec((1, n_mem, MEM_WIDTH), lambda b, i: (b, 0, 0))],
        out_specs=pl.BlockSpec((tq, MEM_WIDTH), lambda b, i: (b * nq + i, 0)),
        compiler_params=_cparams(2),
        name="mem_attn",
    )(q, mem_k, mem_v)


def _ffn_kernel(*refs, tm, T, long_mode, col_chunks):
    if long_mode:
        (x_ref, g_ref, wg_ref, wv_ref, cw_ref, cb_ref, wd_ref, cp_ref, o_ref, tail_ref, xn_scr, carry_scr) = refs
    else:
        (x_ref, g_ref, wg_ref, wv_ref, cw_ref, cb_ref, wd_ref, e1_ref, e2_ref, o_ref, gate_ref, xn_scr) = refs
    xn_scr[...] = _rms(x_ref[...], g_ref[...]).astype(BF16)
    xn = xn_scr[...]
    row = lax.broadcasted_iota(jnp.int32, (tm, 1), 0)
    if long_mode:
        nblk = T // tm

        @pl.when(pl.program_id(0) % nblk == 0)
        def _():
            carry_scr[...] = cp_ref[0]
    else:
        t_in_seq = row % T
    acc = x_ref[...]
    for (lo, w) in col_chunks:
        cs = slice(lo, lo + w)
        gate = _dg(xn, wg_ref[:, cs])
        val = _dg(xn, wv_ref[:, cs])
        r1 = pltpu.roll(gate, 1, 0)
        r2 = pltpu.roll(gate, 2, 0)
        if long_mode:
            c0 = carry_scr[6:7, cs]
            c1 = carry_scr[7:8, cs]
            p1 = jnp.where(row == 0, c1, r1)
            p2 = jnp.where(row == 0, c0, jnp.where(row == 1, c1, r2))
            tail = gate[tm - 8:tm, :]
            carry_scr[:, cs] = tail
            tail_ref[0, :, cs] = tail
        else:
            p1 = jnp.where(t_in_seq >= 1, r1, 0.0) + e1_ref[:, cs]
            p2 = jnp.where(t_in_seq >= 2, r2, 0.0) + e2_ref[:, cs]
            gate_ref[:, cs] = gate
        conv = cb_ref[:, cs] + p2 * cw_ref[0:1, cs]
        conv = conv + p1 * cw_ref[1:2, cs]
        conv = conv + gate * cw_ref[2:3, cs]
        hmid = (conv * jax.nn.sigmoid(conv) * val).astype(BF16)
        acc = acc + _dg(hmid, wd_ref[cs, :])
    o_ref[...] = acc


def _ffn(x, g, w_gate, w_val, conv_w, conv_b, w_down, conv_prev, *, nseq, T):
    M = x.shape[0]
    long_mode = T >= 512
    tm = 512 if long_mode else 256
    assert (T % tm == 0) if long_mode else (tm % T == 0 and M % tm == 0)
    col_chunks = tuple((lo, min(256, D_FF - lo)) for lo in range(0, D_FF, 256))
    const = lambda a: pl.BlockSpec(a.shape, lambda i: (0,) * a.ndim, pipeline_mode=pl.Buffered(1))
    small = lambda a: pl.BlockSpec(a.shape, lambda i: (0,) * a.ndim)
    g2 = g.reshape(1, D_MODEL).astype(F32)
    cb2 = conv_b.reshape(1, D_FF).astype(F32)
    common_specs = [pl.BlockSpec((tm, D_MODEL), lambda i: (i, 0)), small(g2), const(w_gate), const(w_val),
                    small(conv_w), small(cb2), const(w_down)]
    common_args = [x, g2, w_gate, w_val, conv_w, cb2, w_down]
    kern = functools.partial(_ffn_kernel, tm=tm, T=T, long_mode=long_mode, col_chunks=col_chunks)
    if long_mode:
        nblk = T // tm
        cp = jnp.concatenate([jnp.zeros((nseq, 6, D_FF), F32), conv_prev.astype(F32)], axis=1)
        out, tails = pl.pallas_call(
            kern,
            out_shape=(jax.ShapeDtypeStruct((M, D_MODEL), F32), jax.ShapeDtypeStruct((M // tm, 8, D_FF), F32)),
            grid=(M // tm,),
            in_specs=common_specs + [pl.BlockSpec((1, 8, D_FF), lambda i: (i // nblk, 0, 0))],
            out_specs=(pl.BlockSpec((tm, D_MODEL), lambda i: (i, 0)),
                       pl.BlockSpec((1, 8, D_FF), lambda i: (i, 0, 0))),
            scratch_shapes=[pltpu.VMEM((tm, D_MODEL), BF16), pltpu.VMEM((8, D_FF), F32)],
            compiler_params=_cparams(1),
            name="convglu_long",
        )(*common_args, cp)
        new_conv = tails.reshape(nseq, nblk, 8, D_FF)[:, -1, 6:8, :]
        return out, new_conv
    cpf = conv_prev.astype(F32)
    zeros = jnp.zeros((nseq, T, D_FF), F32)
    e1 = zeros.at[:, 0].set(cpf[:, 1]).reshape(M, D_FF)
    e2 = zeros.at[:, 0].set(cpf[:, 0]).at[:, 1].set(cpf[:, 1]).reshape(M, D_FF)
    out, gate = pl.pallas_call(
        kern,
        out_shape=(jax.ShapeDtypeStruct((M, D_MODEL), F32), jax.ShapeDtypeStruct((M, D_FF), F32)),
        grid=(M // tm,),
        in_specs=common_specs + [pl.BlockSpec((tm, D_FF), lambda i: (i, 0))] * 2,
        out_specs=(pl.BlockSpec((tm, D_MODEL), lambda i: (i, 0)), pl.BlockSpec((tm, D_FF), lambda i: (i, 0))),
        scratch_shapes=[pltpu.VMEM((tm, D_MODEL), BF16)],
        compiler_params=_cparams(1),
        name="convglu_short",
    )(*common_args, e1, e2)
    new_conv = gate.reshape(nseq, T, D_FF)[:, T - 2:, :]
    return out, new_conv


def _rope128(x128, cs):
    a = x128 * cs
    s = a + pltpu.roll(a, 64, 1)
    lane = lax.broadcasted_iota(jnp.int32, a.shape, 1)
    return jnp.where(lane < 64, s, 0.0)


def _latent_kernel(h_ref, g_ref, cs_ref, ckv_ref, kpe_ref, ckpe_ref):
    ckv = _rms(h_ref[:, :KV_LORA], g_ref[...])
    rp = _rope128(h_ref[:, KV_LORA:KV_LORA + 128], cs_ref[...])
    ckv_ref[...] = ckv
    kpe_ref[...] = rp[:, :ROPE_DIM]
    ckpe_ref[:, :KV_LORA] = ckv.astype(BF16)
    ckpe_ref[:, KV_LORA:] = rp.astype(BF16)


def _latent_post(h, norm_ckv, cs, *, T):
    M = h.shape[0]
    tm = _pos_tile(M, T)
    ncs = cs.shape[0] // tm
    return pl.pallas_call(
        _latent_kernel,
        out_shape=(jax.ShapeDtypeStruct((M, KV_LORA), F32), jax.ShapeDtypeStruct((M, ROPE_DIM), F32),
                   jax.ShapeDtypeStruct((M, KV_LORA + 128), BF16)),
        grid=(M // tm,),
        in_specs=[pl.BlockSpec((tm, KV_LORA + 128), lambda i: (i, 0)),
                  pl.BlockSpec((1, KV_LORA), lambda i: (0, 0)),
                  pl.BlockSpec((tm, 128), lambda i: (i % ncs, 0))],
        out_specs=(pl.BlockSpec((tm, KV_LORA), lambda i: (i, 0)), pl.BlockSpec((tm, ROPE_DIM), lambda i: (i, 0)),
                   pl.BlockSpec((tm, KV_LORA + 128), lambda i: (i, 0))),
        compiler_params=_cparams(1),
        name="latent_post",
    )(h, norm_ckv.reshape(1, KV_LORA).astype(F32), cs)


def _qpost_kernel(q_ref, cs_ref, o_ref):
    cs = cs_ref[...]
    for h in range(H_B):
        base = h * 256
        o_ref[:, base:base + 128] = q_ref[:, base:base + 128].astype(o_ref.dtype)
        o_ref[:, base + 128:base + 256] = _rope128(q_ref[:, base + 128:base + 256], cs).astype(o_ref.dtype)


def _q_post(q, cs, *, T, out_dtype):
    M = q.shape[0]
    tm = _pos_tile(M, T)
    ncs = cs.shape[0] // tm
    return pl.pallas_call(
        _qpost_kernel,
        out_shape=jax.ShapeDtypeStruct((M, H_B * 256), out_dtype),
        grid=(M // tm,),
        in_specs=[pl.BlockSpec((tm, H_B * 256), lambda i: (i, 0)),
                  pl.BlockSpec((tm, 128), lambda i: (i % ncs, 0))],
        out_specs=pl.BlockSpec((tm, H_B * 256), lambda i: (i, 0)),
        compiler_params=_cparams(1),
        name="q_post",
    )(q, cs)


def _flash_kernel(q_ref, k_ref, v_ref, o_ref, m_scr, l_scr, acc_scr, *, tq, tk):
    i = pl.program_id(1)
    j = pl.program_id(2)

    @pl.when(j == 0)
    def _():
        m_scr[...] = jnp.full(m_scr.shape, -jnp.inf, F32)
        l_scr[...] = jnp.zeros(l_scr.shape, F32)
        acc_scr[...] = jnp.zeros(acc_scr.shape, F32)

    hs = range(H_B)

    def block(masked):
        c = ATTN_SCALE * math.log2(math.e)
        ss = [_dg(q_ref[:, h * 256:(h + 1) * 256], k_ref[:, h * 256:(h + 1) * 256], NT) for h in hs]
        if masked:
            keep = lax.broadcasted_iota(jnp.int32, (tq, tk), 1) <= lax.broadcasted_iota(jnp.int32, (tq, tk), 0)
            ss = [jnp.where(keep, s, -jnp.inf) for s in ss]
        m_prev = [m_scr[h] for h in hs]
        m_new = [jnp.maximum(m_prev[h], jnp.max(ss[h], axis=-1, keepdims=True)) for h in hs]
        alpha = [jnp.exp2((m_prev[h] - m_new[h]) * c) for h in hs]
        ps = [jnp.exp2((ss[h] - m_new[h]) * c) for h in hs]
        pv = [_dg(ps[h].astype(BF16), v_ref[:, h * V_DIM:(h + 1) * V_DIM]) for h in hs]
        for h in hs:
            l_scr[h] = alpha[h] * l_scr[h] + jnp.sum(ps[h], axis=-1, keepdims=True)
            acc_scr[:, h * V_DIM:(h + 1) * V_DIM] = alpha[h] * acc_scr[:, h * V_DIM:(h + 1) * V_DIM] + pv[h]
            m_scr[h] = m_new[h]

    @pl.when(j < i)
    def _():
        block(False)

    @pl.when(j == i)
    def _():
        block(True)
        for h in hs:
            sl = slice(h * V_DIM, (h + 1) * V_DIM)
            o_ref[:, sl] = (acc_scr[:, sl] / l_scr[h]).astype(o_ref.dtype)


def _flash_causal(q, kv, *, nb, T):
    M = q.shape[0]
    tq = tk = 512
    nq = T // tq
    kw = H_B * 256
    vw = H_B * V_DIM
    return pl.pallas_call(
        functools.partial(_flash_kernel, tq=tq, tk=tk),
        out_shape=jax.ShapeDtypeStruct((M, vw), BF16),
        grid=(nb, nq, nq),
        in_specs=[pl.BlockSpec((tq, kw), lambda b, i, j: (b * nq + i, 0)),
                  pl.BlockSpec((tk, kw), lambda b, i, j: (b * nq + jnp.minimum(i, j), 0)),
                  pl.BlockSpec((tk, vw), lambda b, i, j: (b * nq + jnp.minimum(i, j), kw // vw))],
        out_specs=pl.BlockSpec((tq, vw), lambda b, i, j: (b * nq + i, 0)),
        scratch_shapes=[pltpu.VMEM((H_B, tq, 1), F32), pltpu.VMEM((H_B, tq, 1), F32), pltpu.VMEM((tq, vw), F32)],
        compiler_params=_cparams(3),
        name="mla_causal",
    )(q, kv, kv)


def _decode_kernel(pt_ref, q_ref, *refs, PP, n_steps, TS):
    ck_refs = refs[:PP]
    kp_refs = refs[PP:2 * PP]
    ckn_ref, kpn_ref, wuv_ref, o_ref, m_scr, l_scr, acc_scr = refs[2 * PP:]
    j = pl.program_id(1)
    R = q_ref.shape[1]

    @pl.when(j == 0)
    def _():
        m_scr[...] = jnp.full(m_scr.shape, -jnp.inf, F32)
        l_scr[...] = jnp.zeros(l_scr.shape, F32)
        acc_scr[...] = jnp.zeros(acc_scr.shape, F32)

    q_lat = q_ref[0, :, :KV_LORA]
    q_pe = q_ref[0, :, KV_LORA:]
    c = ATTN_SCALE * math.log2(math.e)

    cks = [ck_refs[p][0].astype(BF16) for p in range(PP)]
    kps = [kp_refs[p][0].astype(BF16) for p in range(PP)]
    ss = [_dg(q_lat, cks[p], NT) + _dg(q_pe, kps[p], NT) for p in range(PP)]
    pad = PAGE_SIZE - TS
    ckn = jnp.concatenate([ckn_ref[0], jnp.zeros((pad, KV_LORA), F32)], axis=0).astype(BF16)
    kpn = jnp.concatenate([kpn_ref[0], jnp.zeros((pad, ROPE_DIM), F32)], axis=0).astype(BF16)
    t_q = lax.broadcasted_iota(jnp.int32, (R, PAGE_SIZE), 0) % TS
    col = lax.broadcasted_iota(jnp.int32, (R, PAGE_SIZE), 1)
    last_shift = jnp.where(j == n_steps - 1, 0, -PAGE_SIZE)
    s_new = jnp.where(col <= t_q + last_shift, _dg(q_lat, ckn, NT) + _dg(q_pe, kpn, NT), -jnp.inf)
    ss.append(s_new)
    cks.append(ckn)

    tile_max = ss[0]
    for s in ss[1:]:
        tile_max = jnp.maximum(tile_max, s)
    m_prev = m_scr[...]
    m_new = jnp.maximum(m_prev, jnp.max(tile_max, axis=-1, keepdims=True))
    alpha = jnp.exp2((m_prev - m_new) * c)
    ps = [jnp.exp2((s - m_new) * c) for s in ss]
    p_sum = ps[0]
    for p in ps[1:]:
        p_sum = p_sum + p
    acc = alpha * acc_scr[...]
    for p, vv in zip(ps, cks):
        acc = acc + _dg(p.astype(BF16), vv)
    m_scr[...] = m_new
    l_scr[...] = alpha * l_scr[...] + jnp.sum(p_sum, axis=-1, keepdims=True)
    acc_scr[...] = acc

    @pl.when(j == n_steps - 1)
    def _():
        o_lat = acc_scr[...] / l_scr[...]
        for h in range(H_B):
            o_h = o_lat[h * TS:(h + 1) * TS, :].astype(BF16)
            o_ref[0, :, h * V_DIM:(h + 1) * V_DIM] = _dg(o_h, wuv_ref[h]).astype(o_ref.dtype)


def _decode_attn(page_table, qd, cache_ckv, cache_kpe, ckv_new, kpe_new, w_uv, *, PP=32):
    nb, n_pages = page_table.shape
    TS = ckv_new.shape[1]
    R = qd.shape[1]
    n_steps = n_pages // PP
    pt = page_table.reshape(-1).astype(jnp.int32)

    def page_spec(width, p):
        return pl.BlockSpec((1, PAGE_SIZE, width), lambda b, j, pt_ref: (pt_ref[b * n_pages + j * PP + p], 0, 0))

    in_specs = ([pl.BlockSpec((1, R, KV_LORA + ROPE_DIM), lambda b, j, pt_ref: (b, 0, 0))]
                + [page_spec(KV_LORA, p) for p in range(PP)]
                + [page_spec(ROPE_DIM, p) for p in range(PP)]
                + [pl.BlockSpec((1, TS, KV_LORA), lambda b, j, pt_ref: (b, 0, 0)),
                   pl.BlockSpec((1, TS, ROPE_DIM), lambda b, j, pt_ref: (b, 0, 0)),
                   pl.BlockSpec((H_B, KV_LORA, V_DIM), lambda b, j, pt_ref: (0, 0, 0))])
    grid_spec = pltpu.PrefetchScalarGridSpec(
        num_scalar_prefetch=1,
        grid=(nb, n_steps),
        in_specs=in_specs,
        out_specs=pl.BlockSpec((1, TS, H_B * V_DIM), lambda b, j, pt_ref: (b, 0, 0)),
        scratch_shapes=[pltpu.VMEM((R, 1), F32), pltpu.VMEM((R, 1), F32), pltpu.VMEM((R, KV_LORA), F32)],
    )
    return pl.pallas_call(
        functools.partial(_decode_kernel, PP=PP, n_steps=n_steps, TS=TS),
        out_shape=jax.ShapeDtypeStruct((nb, TS, H_B * V_DIM), BF16),
        grid_spec=grid_spec,
        compiler_params=_cparams(2),
        name="mla_decode",
    )(pt, qd, *([cache_ckv] * PP), *([cache_kpe] * PP), ckv_new, kpe_new, w_uv)


def _rope_table(pos):
    half = ROPE_DIM // 2
    inv = ROPE_BASE ** (-jnp.arange(half, dtype=F32) / half)
    ang = pos.astype(F32)[:, None] * inv[None, :]
    cos, sin = jnp.cos(ang), jnp.sin(ang)
    return jnp.concatenate([cos, cos, -sin, sin], axis=-1)


def _swap_halves(w):
    half = ROPE_DIM // 2
    return jnp.concatenate([w[..., half:], w[..., :half]], axis=-1)


def _prep_rwkv_weights(l, P):
    w_in = P["w_in_a"][l]
    mu = P["mu_a"][l]
    cuts = [0, MIX_A, MIX_A + DECAY_LORA, 2 * MIX_A + DECAY_LORA, 3 * MIX_A + DECAY_LORA,
            3 * MIX_A + DECAY_LORA + A_LORA, C_RWKV, C_A]
    seg = lambda a, i: a[..., cuts[i]:cuts[i + 1]]
    r_w, wl_w, k_w, v_w, al_w, gl_w, qm_w = [seg(w_in, i) for i in range(7)]
    r_m, wl_m, k_m, v_m, al_m, gl_m = [seg(mu, i) for i in range(6)]
    zc = lambda n: jnp.zeros((D_MODEL, n), F32)
    zm = lambda n: jnp.zeros((n,), F32)
    if l > 0:
        vl_w, vl_m = P["w_vres_in"][l - 1], P["mu_vres"][l - 1]
        wv_up, v0 = P["w_vres_up"][l - 1], P["v0"][l - 1]
    else:
        vl_w, vl_m = zc(VRES_LORA), zm(VRES_LORA)
        wv_up, v0 = jnp.zeros((VRES_LORA, MIX_A), F32), zm(MIX_A)
    w_full = jnp.concatenate([r_w, k_w, v_w, wl_w, al_w, gl_w, vl_w, zc(64), qm_w, zc(128)], axis=1)
    mu_full = jnp.concatenate([r_m, k_m, v_m, wl_m, al_m, gl_m, vl_m, zm(64), zm(MEM_WIDTH), zm(128)])
    zr = lambda n: jnp.zeros((n, MIX_A), F32)
    return dict(
        w_in=w_full.astype(BF16),
        mu=mu_full.reshape(1, PA_COLS),
        wd=jnp.concatenate([P["w_decay_up"][l], zr(A_LORA)], axis=0).astype(BF16),
        wa=jnp.concatenate([zr(DECAY_LORA), P["w_a_up"][l]], axis=0).astype(BF16),
        wg=jnp.concatenate([P["w_g_up"][l], zr(256 - GATE_LORA)], axis=0).astype(BF16),
        wv=jnp.concatenate([zr(GATE_LORA), wv_up, zr(256 - GATE_LORA - VRES_LORA)], axis=0).astype(BF16),
        w0=P["w0"][l].reshape(1, MIX_A), a0=P["a0"][l].reshape(1, MIX_A), v0=v0.reshape(1, MIX_A),
        k_k=P["k_k"][l].reshape(1, MIX_A), k_a=P["k_a"][l].reshape(1, MIX_A),
    )


def _prep_mla_weights(P):
    w_kv_a = P["w_kv_a"]
    w_kva = jnp.concatenate([w_kv_a, _swap_halves(w_kv_a[:, KV_LORA:])], axis=1).astype(BF16)
    w_kv_b = P["w_kv_b"]
    w_uk, w_uv = w_kv_b[..., :NOPE_DIM], w_kv_b[..., NOPE_DIM:]
    k_part = jnp.zeros((KV_LORA + 128, H_B, 256), F32)
    k_part = k_part.at[:KV_LORA, :, :NOPE_DIM].set(w_uk)
    eye = jnp.eye(ROPE_DIM, dtype=F32)
    k_part = k_part.at[KV_LORA:KV_LORA + ROPE_DIM, :, NOPE_DIM:NOPE_DIM + ROPE_DIM].set(
        jnp.broadcast_to(eye[:, None, :], (ROPE_DIM, H_B, ROPE_DIM)))
    v_part = jnp.zeros((KV_LORA + 128, H_B, V_DIM), F32).at[:KV_LORA].set(w_uv)
    w_kvx = jnp.concatenate([k_part.reshape(KV_LORA + 128, H_B * 256), v_part.reshape(KV_LORA + 128, H_B * V_DIM)],
                            axis=1).astype(BF16)
    w_q = []
    for j in range(N_B):
        wq = P["w_q_b"][j].reshape(Q_LORA, H_B, NOPE_DIM + ROPE_DIM)
        rope_w = wq[..., NOPE_DIM:]
        w_q.append(jnp.concatenate([wq[..., :NOPE_DIM], rope_w, _swap_halves(rope_w)], axis=-1)
                   .reshape(Q_LORA, H_B * 256).astype(BF16))
    return dict(
        w_kva=w_kva, w_kvx=w_kvx, w_q=w_q,
        w_ukT=jnp.transpose(w_uk, (1, 2, 0)).astype(BF16),
        w_uv=jnp.transpose(w_uv, (1, 0, 2)).astype(BF16),
    )


def _trunk(x, pos, mem_k, mem_v, wkv0, shift0, conv0, past, P, W, *, nseq, T):
    M = nseq * T
    x = x.reshape(M, D_MODEL)
    cs = _rope_table(pos)
    if T < ROW_TILES[0]:
        cs = jnp.tile(cs, (nseq, 1))
    chunk = 64 if T >= 64 else T
    new_wkv, new_shift, new_conv = [], [], []
    v_first = None
    ckv = kpe = ckpe = kvx = None
    for l in range(DEPTH):
        g_mix = P["norm_mix"][l]
        if l < N_A:
            pw = W["rwkv"][l]
            proj = _mm([x], [pw["w_in"]], g=g_mix, name="rwkv_in")
            x_last = x.reshape(nseq, T, D_MODEL)[:, -1]
            new_shift.append(_rmsnorm(x_last, g_mix, name="shift_norm"))
            sp = _mm([shift0[l].astype(F32)], [pw["w_in"]], name="rwkv_in_shift")
            prev = jnp.concatenate([sp[:, None, :], proj.reshape(nseq, T, PA_COLS)[:, :-1]], axis=1).reshape(M, PA_COLS)
            r, lw, k, v, kn, b, g, q_mem = _rwkv_prep(proj, prev, pw, v_first if l > 0 else None)
            if l == 0:
                v_first = v
            h0 = jnp.swapaxes(wkv0[l].astype(F32), -1, -2)
            o_tok, h_new = _wkv(r, lw, k, v, kn, b, g, P["r_k"][l], P["lnx_w"][l], P["lnx_b"][l], h0,
                                nseq=nseq, T=T, C=chunk)
            new_wkv.append(jnp.swapaxes(h_new, -1, -2))
        else:
            j = l - N_A
            if l == N_A:
                hkv = _mm([x], [W["mla"]["w_kva"]], g=P["norm_kv"], name="kv_a")
                ckv, kpe, ckpe = _latent_post(hkv, P["norm_ckv"], cs, T=T)
                if past is None:
                    kvx = _mm([ckpe], [W["mla"]["w_kvx"]], out_dtype=BF16, name="kv_expand")
            proj = _mm([x], [W["w_in_b"][j]], g=g_mix, name="mla_in")
            qf = _mm([proj], [W["mla"]["w_q"][j]], g=P["norm_q"][j], name="q_b")
            q_mem = proj[:, Q_LORA:].astype(BF16)
            if past is None:
                q = _q_post(qf, cs, T=T, out_dtype=BF16)
                o_tok = _flash_causal(q, kvx, nb=nseq, T=T)
            else:
                q = _q_post(qf, cs, T=T, out_dtype=BF16).reshape(M, H_B, 256)
                q_lat = jnp.stack([_mm([q[:, h, :NOPE_DIM]], [W["mla"]["w_ukT"][h]], out_dtype=BF16, name="q_absorb")
                                   for h in range(H_B)], axis=1)
                qd = jnp.concatenate([q_lat, q[:, :, NOPE_DIM:NOPE_DIM + ROPE_DIM]], axis=-1)
                qd = qd.reshape(nseq, T, H_B, KV_LORA + ROPE_DIM).transpose(0, 2, 1, 3).reshape(nseq, H_B * T, -1)
                o_tok = _decode_attn(past[2], qd, past[0], past[1], ckv.reshape(nseq, T, KV_LORA),
                                     kpe.reshape(nseq, T, ROPE_DIM), W["mla"]["w_uv"]).reshape(M, H_B * V_DIM)
        o_mem = _mem_attn(q_mem, mem_k[l], mem_v[l], nb=nseq, T=T)
        x = _mm([o_tok, o_mem], [W["w_o_tok"][l], W["w_o_mem"][l]], res=x, name="w_o")
        x, c_new = _ffn(x, P["norm_ffn"][l], W["w_gate"][l], W["w_val"][l], P["conv_w"][l], P["conv_b"][l],
                        W["w_down"][l], conv0[l], nseq=nseq, T=T)
        new_conv.append(c_new)
    y = _rmsnorm(x, P["final_norm"], name="final_norm")
    return (y.reshape(nseq, T, D_MODEL), ckv.reshape(nseq, T, KV_LORA), kpe.reshape(nseq, T, ROPE_DIM),
            jnp.stack(new_wkv), jnp.stack(new_shift), jnp.stack(new_conv))


def kernel(x_prompt, x_sample, cache_ckv, cache_kpe, cache_mem_k, cache_mem_v, state_wkv, state_shift, state_conv, page_table, mem_prompt, norm_mix, norm_ffn, norm_mem, w_mem_kv, w_o, w_in_a, mu_a, w_vres_in, mu_vres, w_decay_up, w0, w_a_up, a0, w_g_up, w_vres_up, v0, k_k, k_a, r_k, lnx_w, lnx_b, norm_kv, w_kv_a, norm_ckv, w_kv_b, w_in_b, norm_q, w_q_b, w_ffn_up, conv_w, conv_b, w_ffn_down, final_norm):
    P = dict(norm_mix=norm_mix, norm_ffn=norm_ffn, w_o=w_o, w_in_a=w_in_a, mu_a=mu_a, w_vres_in=w_vres_in,
             mu_vres=mu_vres, w_decay_up=w_decay_up, w0=w0, w_a_up=w_a_up, a0=a0, w_g_up=w_g_up,
             w_vres_up=w_vres_up, v0=v0, k_k=k_k, k_a=k_a, r_k=r_k, lnx_w=lnx_w, lnx_b=lnx_b,
             norm_kv=norm_kv, w_kv_a=w_kv_a, norm_ckv=norm_ckv, w_kv_b=w_kv_b, w_in_b=w_in_b, norm_q=norm_q,
             w_q_b=w_q_b, conv_w=conv_w, conv_b=conv_b, final_norm=final_norm)
    W = dict(
        rwkv=[_prep_rwkv_weights(l, P) for l in range(N_A)],
        mla=_prep_mla_weights(P),
        w_in_b=[w_in_b[j].astype(BF16) for j in range(N_B)],
        w_o_tok=[w_o[l, :MIX_A].astype(BF16) for l in range(DEPTH)],
        w_o_mem=[w_o[l, MIX_A:].astype(BF16) for l in range(DEPTH)],
        w_gate=[w_ffn_up[l, :, :D_FF].astype(BF16) for l in range(DEPTH)],
        w_val=[w_ffn_up[l, :, D_FF:].astype(BF16) for l in range(DEPTH)],
        w_down=[w_ffn_down[l].astype(BF16) for l in range(DEPTH)],
    )
    dt = x_prompt.dtype
    B, T = x_prompt.shape[:2]
    n_mem = mem_prompt.shape[1]
    mem2d = mem_prompt.reshape(B * n_mem, D_MODEL)
    mem_k_p, mem_v_p = [], []
    for l in range(DEPTH):
        kv = _mm([mem2d], [w_mem_kv[l].astype(BF16)], g=norm_mem[l], name="mem_kv")
        mem_k_p.append(kv[:, :MEM_WIDTH].reshape(B, n_mem, MEM_WIDTH))
        mem_v_p.append(kv[:, MEM_WIDTH:].reshape(B, n_mem, MEM_WIDTH))
    y_p, ckv_p, kpe_p, wkv_p, shift_p, conv_p = _trunk(
        x_prompt, jnp.arange(T), mem_k_p, mem_v_p,
        jnp.zeros((N_A, B, H_A, HEAD_A, HEAD_A), dt), jnp.zeros((N_A, B, D_MODEL), dt),
        jnp.zeros((DEPTH, B, CONV_W - 1, D_FF), dt), None, P, W, nseq=B, T=T)
    DB, TS = x_sample.shape[:2]
    past_len = page_table.shape[1] * PAGE_SIZE
    mk_s = [cache_mem_k[l].reshape(DB, n_mem, MEM_WIDTH) for l in range(DEPTH)]
    mv_s = [cache_mem_v[l].reshape(DB, n_mem, MEM_WIDTH) for l in range(DEPTH)]
    y_s, ckv_s, kpe_s, wkv_s, shift_s, conv_s = _trunk(
        x_sample, past_len + jnp.arange(TS), mk_s, mv_s, state_wkv, state_shift, state_conv,
        (cache_ckv, cache_kpe, page_table), P, W, nseq=DB, T=TS)
    mem_k_out = jnp.stack(mem_k_p).reshape(DEPTH, B, n_mem, MEM_HEADS, MEM_HEAD_DIM)
    mem_v_out = jnp.stack(mem_v_p).reshape(DEPTH, B, n_mem, MEM_HEADS, MEM_HEAD_DIM)
    return (y_p, y_s, ckv_p, kpe_p, mem_k_out, mem_v_out, wkv_p, shift_p, conv_p,
            ckv_s, kpe_s, wkv_s, shift_s, conv_s)
```

```python
import functools
import math

import jax
import jax.numpy as jnp
from jax import lax
from jax.experimental import pallas as pl
from jax.experimental.pallas import tpu as pltpu

F32 = jnp.float32
BF16 = jnp.bfloat16

D_MODEL = 1024
DEPTH = 4
N_A = 2
N_B = 2
MEM_WIDTH = 256
MIX_A = 768
HEAD_A = 64
H_A = 12
DECAY_LORA = 64
A_LORA = 64
VRES_LORA = 32
GATE_LORA = 160
GN_EPS = 64e-5
C_RWKV = 3 * MIX_A + DECAY_LORA + A_LORA + GATE_LORA
C_A = C_RWKV + MEM_WIDTH
MEM_HEADS = 4
MEM_HEAD_DIM = 64
MEM_SCALE = MEM_HEAD_DIM ** -0.5
NOPE_DIM = 128
ROPE_DIM = 64
V_DIM = 128
H_B = 6
Q_LORA = 256
KV_LORA = 256
ROPE_BASE = 10000.0
ATTN_SCALE = (NOPE_DIM + ROPE_DIM) ** -0.5
D_FF = 2816
CONV_W = 3
RMS_EPS = 1e-6
PAGE_SIZE = 128

PA_R, PA_K, PA_V = 0, 768, 1536
PA_LORA = 2304
PA_GATE = 2432
PA_QMEM = 2688
PA_COLS = 3072

VMEM_LIMIT_BYTES = 52 * 1024 * 1024

NN = ((1,), (0,))
NT = ((1,), (1,))
TN = ((0,), (0,))


def _dg(a, b, dims=NN):
    return lax.dot_general(a, b, (dims, ((), ())), preferred_element_type=F32)


def _split(x):
    hi = x.astype(BF16)
    lo = (x - hi.astype(F32)).astype(BF16)
    return hi, lo


def _dot3(a, b, dims=NN):
    ah, al = _split(a)
    bh, bl = _split(b)
    return _dg(ah, bh, dims) + (_dg(ah, bl, dims) + _dg(al, bh, dims))


def _transpose_exact(x, eye_bf16):
    hi = x.astype(BF16)
    r1 = x - hi.astype(F32)
    mid = r1.astype(BF16)
    lo = (r1 - mid.astype(F32)).astype(BF16)
    return (_dg(eye_bf16, hi, NT) + _dg(eye_bf16, mid, NT)) + _dg(eye_bf16, lo, NT)


def _cparams(n_axes):
    return pltpu.CompilerParams(dimension_semantics=("arbitrary",) * n_axes,
                                vmem_limit_bytes=VMEM_LIMIT_BYTES)


def _rms(x, g, eps=RMS_EPS):
    return x * lax.rsqrt(jnp.mean(x * x, axis=-1, keepdims=True) + eps) * g


def _pick_tile(n, candidates):
    for c in candidates:
        if n % c == 0:
            return c
    return n


ROW_TILES = (512, 256, 128, 64, 32, 16, 8)


def _pos_tile(M, T):
    return _pick_tile(T, ROW_TILES) if T >= ROW_TILES[0] else _pick_tile(M, ROW_TILES)


def _mm_kernel(*refs, n_in, has_norm, has_res, f32_in):
    xs = refs[:n_in]
    ws = refs[n_in:2 * n_in]
    k = 2 * n_in
    g_ref = refs[k] if has_norm else None
    k += int(has_norm)
    r_ref = refs[k] if has_res else None
    k += int(has_res)
    o_ref = refs[k]
    scr = list(refs[k + 1:])

    lhs = []
    si = 0
    for idx in range(n_in):
        if f32_in[idx]:
            s_ref = scr[si]
            si += 1

            @pl.when(pl.program_id(1) == 0)
            def _(x_ref=xs[idx], s_ref=s_ref, idx=idx):
                x = x_ref[...]
                if has_norm and idx == 0:
                    x = _rms(x, g_ref[...])
                s_ref[...] = x.astype(BF16)

            lhs.append(s_ref)
        else:
            lhs.append(xs[idx])
    acc = _dg(lhs[0][...], ws[0][...])
    for idx in range(1, n_in):
        acc = acc + _dg(lhs[idx][...], ws[idx][...])
    if has_res:
        acc = acc + r_ref[...]
    o_ref[...] = acc.astype(o_ref.dtype)


def _mm(xs, ws, *, g=None, res=None, out_dtype=F32, x_col_blocks=None, ks=None, name="mm"):
    n_in = len(xs)
    M = xs[0].shape[0]
    N = ws[0].shape[1]
    ks = [w.shape[0] for w in ws]
    x_col_blocks = x_col_blocks or [0] * n_in
    tm = _pick_tile(M, (512, 256, 128, 64, 32, 16, 8))
    tn = _pick_tile(N, (1024, 768, 512, 384, 256, 128))
    f32_in = tuple(x.dtype == F32 for x in xs)
    in_specs = []
    for x, kk, cb in zip(xs, ks, x_col_blocks):
        in_specs.append(pl.BlockSpec((tm, kk), lambda i, j, cb=cb: (i, cb)))
    for w, kk in zip(ws, ks):
        in_specs.append(pl.BlockSpec((kk, tn), lambda i, j: (0, j)))
    args = list(xs) + list(ws)
    if g is not None:
        in_specs.append(pl.BlockSpec((1, ks[0]), lambda i, j: (0, 0)))
        args.append(g.reshape(1, ks[0]).astype(F32))
    if res is not None:
        in_specs.append(pl.BlockSpec((tm, tn), lambda i, j: (i, j)))
        args.append(res)
    scratch = [pltpu.VMEM((tm, kk), BF16) for kk, f in zip(ks, f32_in) if f]
    return pl.pallas_call(
        functools.partial(_mm_kernel, n_in=n_in, has_norm=g is not None, has_res=res is not None,
                          f32_in=f32_in),
        out_shape=jax.ShapeDtypeStruct((M, N), out_dtype),
        grid=(M // tm, N // tn),
        in_specs=in_specs,
        out_specs=pl.BlockSpec((tm, tn), lambda i, j: (i, j)),
        scratch_shapes=scratch,
        compiler_params=_cparams(2),
        name=name,
    )(*args)


def _rmsnorm_kernel(x_ref, g_ref, o_ref):
    o_ref[...] = _rms(x_ref[...], g_ref[...])


def _rmsnorm(x, g, name="rmsnorm"):
    M, D = x.shape
    tm = _pick_tile(M, (512, 256, 128, 64, 32, 16, 8))
    return pl.pallas_call(
        _rmsnorm_kernel,
        out_shape=jax.ShapeDtypeStruct((M, D), F32),
        grid=(M // tm,),
        in_specs=[pl.BlockSpec((tm, D), lambda i: (i, 0)), pl.BlockSpec((1, D), lambda i: (0, 0))],
        out_specs=pl.BlockSpec((tm, D), lambda i: (i, 0)),
        compiler_params=_cparams(1),
        name=name,
    )(x, g.reshape(1, D).astype(F32))


def _mem_kv_kernel(x_ref, g_ref, w_ref, k_ref, v_ref):
    kv = _dg(_rms(x_ref[...], g_ref[0]).astype(BF16), w_ref[0])
    k_ref[0] = kv[:, :MEM_WIDTH]
    v_ref[0] = kv[:, MEM_WIDTH:]


def _mem_kv(mem2d, norm_mem, w_mem_kv):
    M = mem2d.shape[0]
    tm = _pick_tile(M, ROW_TILES)
    out = jax.ShapeDtypeStruct((DEPTH, M, MEM_WIDTH), F32)
    return pl.pallas_call(
        _mem_kv_kernel,
        out_shape=(out, out),
        grid=(DEPTH, M // tm),
        in_specs=[pl.BlockSpec((tm, D_MODEL), lambda l, i: (i, 0)),
                  pl.BlockSpec((1, 1, D_MODEL), lambda l, i: (l, 0, 0)),
                  pl.BlockSpec((1, D_MODEL, 2 * MEM_WIDTH), lambda l, i: (l, 0, 0))],
        out_specs=(pl.BlockSpec((1, tm, MEM_WIDTH), lambda l, i: (l, i, 0)),
                   pl.BlockSpec((1, tm, MEM_WIDTH), lambda l, i: (l, i, 0))),
        compiler_params=_cparams(2),
        name="mem_kv",
    )(mem2d, norm_mem.reshape(DEPTH, 1, D_MODEL).astype(F32), w_mem_kv.astype(BF16))


def _segsum64(x):
    r = lax.broadcasted_iota(jnp.int32, (128, 128), 0) >> 6
    c = lax.broadcasted_iota(jnp.int32, (128, 128), 1) >> 6
    ones = jnp.where(r == c, 1.0, 0.0).astype(BF16)
    outs = []
    for j in range(x.shape[1] // 128):
        hi, lo = _split(x[:, j * 128:(j + 1) * 128])
        outs.append(_dg(hi, ones) + _dg(lo, ones))
    return jnp.concatenate(outs, axis=1)


def _softplus(x):
    return jnp.maximum(x, 0.0) + jnp.log1p(jnp.exp(-jnp.abs(x)))


def _prep_kernel(*refs, has_vres, blocks_per_seq):
    (cur_ref, prev_ref, mu_ref, wd_ref, wa_ref, wg_ref, wv_ref, w0_ref, a0_ref, v0_ref, kk_ref, ka_ref) = refs[:12]
    k = 12
    vf_ref = refs[k] if has_vres else None
    k += int(has_vres)
    r_o, lw_o, k_o, v_o, kn_o, b_o, g_o, qm_o = refs[k:k + 8]
    tm = cur_ref.shape[0]

    if blocks_per_seq:
        carry_scr = refs[k + 8]

        @pl.when(pl.program_id(0) % blocks_per_seq == 0)
        def _():
            carry_scr[...] = prev_ref[0]

        first_row = lax.broadcasted_iota(jnp.int32, (tm, 1), 0) == 0

    def mix(lo, hi):
        c = cur_ref[:, lo:hi]
        if blocks_per_seq:
            p = jnp.where(first_row, carry_scr[:, lo:hi], pltpu.roll(c, 1, 0))
        else:
            p = prev_ref[:, lo:hi]
        return c + (p - c) * mu_ref[:, lo:hi]

    lora = mix(PA_LORA, PA_LORA + 128)
    zw = _dg(jnp.tanh(lora).astype(BF16), wd_ref[...]) + w0_ref[...]
    w_log = -_softplus(-zw) - 0.5
    lw_o[...] = -jnp.exp(w_log)
    a = jax.nn.sigmoid(_dg(lora.astype(BF16), wa_ref[...]) + a0_ref[...])
    gin = mix(PA_GATE, PA_GATE + 256)
    g_o[...] = _dg(jax.nn.sigmoid(gin).astype(BF16), wg_ref[...])
    r_o[...] = mix(PA_R, PA_R + MIX_A)
    kx = mix(PA_K, PA_K + MIX_A)
    v = mix(PA_V, PA_V + MIX_A)
    if has_vres:
        sv = jax.nn.sigmoid(_dg(gin.astype(BF16), wv_ref[...]) + v0_ref[...])
        v = v + (vf_ref[...] - v) * sv
    v_o[...] = v
    kk = kx * kk_ref[...]
    kk = kk * lax.rsqrt(jnp.maximum(_segsum64(kk * kk), 1e-24))
    kn_o[...] = kk
    b_o[...] = kk * a
    k_o[...] = kx * (1.0 + (a - 1.0) * ka_ref[...])
    qm_o[...] = cur_ref[:, PA_QMEM:PA_QMEM + MEM_WIDTH].astype(BF16)
    if blocks_per_seq:
        carry_scr[...] = cur_ref[tm - 1:tm, :]


def _rwkv_prep(cur, sp, pw, v_first, *, nseq, T):
    M = cur.shape[0]
    has_vres = v_first is not None
    if T >= 256:
        tm = 256
        blocks_per_seq = T // tm
        prev = sp.reshape(nseq, 1, PA_COLS)
        prev_spec = pl.BlockSpec((1, 1, PA_COLS), lambda i: (i // blocks_per_seq, 0, 0))
        scratch = [pltpu.VMEM((1, PA_COLS), F32)]
    else:
        tm = _pick_tile(M, (256, 128, 64, 32, 16, 8))
        blocks_per_seq = 0
        prev = jnp.concatenate([sp[:, None, :], cur.reshape(nseq, T, PA_COLS)[:, :-1]], axis=1).reshape(M, PA_COLS)
        prev_spec = pl.BlockSpec((tm, PA_COLS), lambda i: (i, 0))
        scratch = []
    row = lambda w: pl.BlockSpec((tm, w), lambda i: (i, 0))
    full = lambda a: pl.BlockSpec(a.shape, lambda i: (0,) * a.ndim)
    consts = [pw["mu"], pw["wd"], pw["wa"], pw["wg"], pw["wv"], pw["w0"], pw["a0"], pw["v0"], pw["k_k"], pw["k_a"]]
    args = [cur, prev] + consts
    in_specs = [row(PA_COLS), prev_spec] + [full(a) for a in consts]
    if has_vres:
        args.append(v_first)
        in_specs.append(row(MIX_A))
    outs = [jax.ShapeDtypeStruct((M, MIX_A), F32)] * 7 + [jax.ShapeDtypeStruct((M, MEM_WIDTH), BF16)]
    out_specs = [row(MIX_A)] * 7 + [row(MEM_WIDTH)]
    return pl.pallas_call(
        functools.partial(_prep_kernel, has_vres=has_vres, blocks_per_seq=blocks_per_seq),
        out_shape=outs,
        grid=(M // tm,),
        in_specs=in_specs,
        out_specs=out_specs,
        scratch_shapes=scratch,
        compiler_params=_cparams(1),
        name="rwkv_prep",
    )(*args)


def _wkv_kernel(r_ref, lw_ref, k_ref, v_ref, kn_ref, b_ref, g_ref, rk_ref, lnw_ref, lnb_ref, h0_ref,
                o_ref, hT_ref, h_scr, *, C, H, N, nchunks):
    c_idx = pl.program_id(1)

    eye_bf = jnp.where(lax.broadcasted_iota(jnp.int32, (N, N), 0) == lax.broadcasted_iota(jnp.int32, (N, N), 1),
                       1.0, 0.0).astype(BF16)

    @pl.when(c_idx == 0)
    def _():
        for h in range(H):
            h_scr[h] = _transpose_exact(h0_ref[0, h], eye_bf)

    lw = lw_ref[...]
    ti = lax.broadcasted_iota(jnp.int32, (C, C), 0)
    tj = lax.broadcasted_iota(jnp.int32, (C, C), 1)
    ltri = jnp.where(tj <= ti, 1.0, 0.0).astype(BF16)
    lw_hi, lw_lo = _split(lw)
    cum = _dg(ltri, lw_hi) + _dg(ltri, lw_lo)
    cum_last = jnp.sum(lw, axis=0, keepdims=True)
    e_neg = jnp.exp(-cum)
    r_all = r_ref[...]
    k_all = k_ref[...]
    v_all = v_ref[...]
    kt_all = kn_ref[...] * jnp.exp(cum - lw)
    rt_all = r_all * jnp.exp(cum)
    kh_all = k_all * e_neg
    bh_all = b_ref[...] * e_neg
    e_rem = jnp.exp(cum_last - cum)
    kb_all = k_all * e_rem
    bb_all = b_ref[...] * e_rem
    p_last = jnp.exp(cum_last)
    rkr = r_all * k_all * rk_ref[...]

    strict = tj < ti
    incl = tj <= ti
    ni = lax.broadcasted_iota(jnp.int32, (N, N), 0)
    nj = lax.broadcasted_iota(jnp.int32, (N, N), 1)
    eye_n = ni == nj
    eye_c = jnp.where(ti == tj, 1.0, 0.0)

    hs = range(H)
    sls = [slice(h * N, (h + 1) * N) for h in hs]
    cut = lambda a: [a[:, sl].astype(BF16) for sl in sls]
    kt, rt, kh, bh, kb, bb, vv = (cut(a) for a in (kt_all, rt_all, kh_all, bh_all, kb_all, bb_all, v_all))
    a_k = [jnp.where(strict, _dg(kt[h], kh[h], NT), 0.0).astype(BF16) for h in hs]
    a_b = [jnp.where(strict, _dg(kt[h], bh[h], NT), 0.0) for h in hs]
    q_k = [jnp.where(incl, _dg(rt[h], kh[h], NT), 0.0).astype(BF16) for h in hs]
    q_b = [jnp.where(incl, _dg(rt[h], bh[h], NT), 0.0).astype(BF16) for h in hs]
    tinv = [eye_c - jnp.where(((ti >> 1) == (tj >> 1)), a_b[h], 0.0) for h in hs]
    blk = 2
    while blk < C:
        sh = blk.bit_length() - 1
        lvl = ((ti >> (sh + 1)) == (tj >> (sh + 1))) & (((ti >> sh) & 1) == 1) & (((tj >> sh) & 1) == 0)
        t16 = [tinv[h].astype(BF16) for h in hs]
        mt = [_dg(jnp.where(lvl, a_b[h], 0.0).astype(BF16), t16[h]).astype(BF16) for h in hs]
        tinv = [tinv[h] - _dg(t16[h], mt[h]) for h in hs]
        blk *= 2
    t16 = [tinv[h].astype(BF16) for h in hs]
    akv = [_dg(a_k[h], vv[h]).astype(BF16) for h in hs]
    x = [_dg(t16[h], jnp.concatenate([kt[h], akv[h]], axis=1)).astype(BF16) for h in hs]
    qw = [_dg(q_b[h], x[h]) for h in hs]
    r2 = [(rt_all[:, sls[h]] - qw[h][:, :N]).astype(BF16) for h in hs]
    y0 = [_dg(q_k[h], vv[h]) - qw[h][:, N:] for h in hs]
    bw = [_dg(bb[h], x[h], TN) for h in hs]
    gmat = [(jnp.where(eye_n, jnp.broadcast_to(p_last[:, sls[h]], (N, N)), 0.0) - bw[h][:, :N]).astype(BF16) for h in hs]
    h_add = [_dg(kb[h], vv[h], TN) - bw[h][:, N:] for h in hs]
    h0 = [h_scr[h].astype(BF16) for h in hs]
    y = [_dg(r2[h], h0[h]) + y0[h] for h in hs]
    for h in hs:
        h_scr[h] = _dg(gmat[h], h0[h]) + h_add[h]
    for h in hs:
        sl = sls[h]
        mean = jnp.mean(y[h], axis=-1, keepdims=True)
        yc = y[h] - mean
        var = jnp.mean(yc * yc, axis=-1, keepdims=True)
        yn = yc * lax.rsqrt(var + GN_EPS)
        yn = yn * lnw_ref[:, sl] + lnb_ref[:, sl]
        bonus = jnp.sum(rkr[:, sl], axis=-1, keepdims=True) * v_all[:, sl]
        o_ref[:, sl] = ((yn + bonus) * g_ref[:, sl]).astype(o_ref.dtype)

    @pl.when(c_idx == nchunks - 1)
    def _():
        for h in range(H):
            hT_ref[0, h] = _transpose_exact(h_scr[h], eye_bf)


def _wkv(r, lw, k, v, kn, b, g, r_k, lnx_w, lnx_b, h0, *, nseq, T, C):
    M = r.shape[0]
    nchunks = T // C
    HN = MIX_A
    row = pl.BlockSpec((C, HN), lambda s, c: (s * nchunks + c, 0))
    vec = pl.BlockSpec((1, HN), lambda s, c: (0, 0))
    st = pl.BlockSpec((1, H_A, HEAD_A, HEAD_A), lambda s, c: (s, 0, 0, 0))
    return pl.pallas_call(
        functools.partial(_wkv_kernel, C=C, H=H_A, N=HEAD_A, nchunks=nchunks),
        out_shape=(jax.ShapeDtypeStruct((M, HN), BF16),
                   jax.ShapeDtypeStruct((nseq, H_A, HEAD_A, HEAD_A), F32)),
        grid=(nseq, nchunks),
        in_specs=[row] * 7 + [vec] * 3 + [st],
        out_specs=(row, st),
        scratch_shapes=[pltpu.VMEM((H_A, HEAD_A, HEAD_A), F32)],
        compiler_params=_cparams(2),
        name="wkv7",
    )(r, lw, k, v, kn, b, g, r_k.reshape(1, HN), lnx_w.reshape(1, HN), lnx_b.reshape(1, HN), h0)


def _mem_attn_kernel(q_ref, k_ref, v_ref, o_ref, *, nbb, tq):
    c = MEM_SCALE * math.log2(math.e)
    lane_head = lax.broadcasted_iota(jnp.int32, (1, MEM_WIDTH), 1) >> 6
    items = [(bb, h) for bb in range(nbb) for h in range(MEM_HEADS)]
    q_all = q_ref[...].astype(F32)
    qs = [q_all[bb * tq:(bb + 1) * tq, :] for bb in range(nbb)]
    ks = [k_ref[bb].astype(BF16) for bb in range(nbb)]
    vs = [v_ref[bb].astype(BF16) for bb in range(nbb)]
    ss = [_dg(jnp.where(lane_head == h, qs[bb], 0.0).astype(BF16), ks[bb], NT) for bb, h in items]
    ms = [jnp.max(s, axis=-1, keepdims=True) for s in ss]
    ps = [jnp.exp2((s - m) * c) for s, m in zip(ss, ms)]
    inv_l = [1.0 / jnp.sum(p, axis=-1, keepdims=True) for p in ps]
    pv = [_dg(p.astype(BF16), vs[bb]) for p, (bb, h) in zip(ps, items)]
    outs = []
    for bb in range(nbb):
        o = jnp.zeros((tq, MEM_WIDTH), F32)
        for h in range(MEM_HEADS):
            idx = bb * MEM_HEADS + h
            o = jnp.where(lane_head == h, pv[idx] * inv_l[idx], o)
        outs.append(o)
    o_ref[...] = (outs[0] if nbb == 1 else jnp.concatenate(outs, axis=0)).astype(o_ref.dtype)


def _mem_attn(q, mem_k, mem_v, *, nb, T):
    M = q.shape[0]
    if T >= ROW_TILES[0]:
        tq, nbb = ROW_TILES[0], 1
    else:
        tq, nbb = T, _pick_tile(nb, (16, 8, 4, 2, 1))
    nq = T // tq
    n_mem = mem_k.shape[1]
    return pl.pallas_call(
        functools.partial(_mem_attn_kernel, nbb=nbb, tq=tq),
        out_shape=jax.ShapeDtypeStruct((M, MEM_WIDTH), BF16),
        grid=(nb // nbb, nq),
        in_specs=[pl.BlockSpec((nbb * tq, MEM_WIDTH), lambda b, i: (b * nq + i, 0)),
                  pl.BlockSpec((nbb, n_mem, MEM_WIDTH), lambda b, i: (b, 0, 0)),
                  pl.BlockSpec((nbb, n_mem, MEM_WIDTH), lambda b, i: (b, 0, 0))],
        out_specs=pl.BlockSpec((nbb * tq, MEM_WIDTH), lambda b, i: (b * nq + i, 0)),
        compiler_params=_cparams(2),
        name="mem_attn",
    )(q, mem_k, mem_v)


def _ffn_kernel(*refs, tm, T, long_mode, col_chunks):
    if long_mode:
        (x_ref, g_ref, wg_ref, wv_ref, cw_ref, cb_ref, wd_ref, cp_ref, o_ref, tail_ref, xn_scr, carry_scr) = refs
    else:
        (x_ref, g_ref, wg_ref, wv_ref, cw_ref, cb_ref, wd_ref, e1_ref, e2_ref, o_ref, gate_ref, xn_scr) = refs
    xn_scr[...] = _rms(x_ref[...], g_ref[...]).astype(BF16)
    xn = xn_scr[...]
    row = lax.broadcasted_iota(jnp.int32, (tm, 1), 0)
    if long_mode:
        nblk = T // tm

        @pl.when(pl.program_id(0) % nblk == 0)
        def _():
            carry_scr[...] = cp_ref[0]
    else:
        t_in_seq = row % T
    acc = x_ref[...]
    for (lo, w) in col_chunks:
        cs = slice(lo, lo + w)
        gate = _dg(xn, wg_ref[:, cs])
        val = _dg(xn, wv_ref[:, cs])
        r1 = pltpu.roll(gate, 1, 0)
        r2 = pltpu.roll(gate, 2, 0)
        if long_mode:
            c0 = carry_scr[6:7, cs]
            c1 = carry_scr[7:8, cs]
            p1 = jnp.where(row == 0, c1, r1)
            p2 = jnp.where(row == 0, c0, jnp.where(row == 1, c1, r2))
            tail = gate[tm - 8:tm, :]
            carry_scr[:, cs] = tail
            tail_ref[0, :, cs] = tail
        else:
            p1 = jnp.where(t_in_seq >= 1, r1, 0.0) + e1_ref[:, cs]
            p2 = jnp.where(t_in_seq >= 2, r2, 0.0) + e2_ref[:, cs]
            gate_ref[:, cs] = gate
        conv = cb_ref[:, cs] + p2 * cw_ref[0:1, cs]
        conv = conv + p1 * cw_ref[1:2, cs]
        conv = conv + gate * cw_ref[2:3, cs]
        hmid = (conv * jax.nn.sigmoid(conv) * val).astype(BF16)
        acc = acc + _dg(hmid, wd_ref[cs, :])
    o_ref[...] = acc


def _ffn(x, g, w_gate, w_val, conv_w, conv_b, w_down, conv_prev, *, nseq, T):
    M = x.shape[0]
    long_mode = T >= 512
    tm = 512 if long_mode else 256
    assert (T % tm == 0) if long_mode else (tm % T == 0 and M % tm == 0)
    col_chunks = tuple((lo, min(256, D_FF - lo)) for lo in range(0, D_FF, 256))
    const = lambda a: pl.BlockSpec(a.shape, lambda i: (0,) * a.ndim, pipeline_mode=pl.Buffered(1))
    small = lambda a: pl.BlockSpec(a.shape, lambda i: (0,) * a.ndim)
    g2 = g.reshape(1, D_MODEL).astype(F32)
    cb2 = conv_b.reshape(1, D_FF).astype(F32)
    common_specs = [pl.BlockSpec((tm, D_MODEL), lambda i: (i, 0)), small(g2), const(w_gate), const(w_val),
                    small(conv_w), small(cb2), const(w_down)]
    common_args = [x, g2, w_gate, w_val, conv_w, cb2, w_down]
    kern = functools.partial(_ffn_kernel, tm=tm, T=T, long_mode=long_mode, col_chunks=col_chunks)
    if long_mode:
        nblk = T // tm
        cp = jnp.concatenate([jnp.zeros((nseq, 6, D_FF), F32), conv_prev.astype(F32)], axis=1)
        out, tails = pl.pallas_call(
            kern,
            out_shape=(jax.ShapeDtypeStruct((M, D_MODEL), F32), jax.ShapeDtypeStruct((M // tm, 8, D_FF), F32)),
            grid=(M // tm,),
            in_specs=common_specs + [pl.BlockSpec((1, 8, D_FF), lambda i: (i // nblk, 0, 0))],
            out_specs=(pl.BlockSpec((tm, D_MODEL), lambda i: (i, 0)),
                       pl.BlockSpec((1, 8, D_FF), lambda i: (i, 0, 0))),
            scratch_shapes=[pltpu.VMEM((tm, D_MODEL), BF16), pltpu.VMEM((8, D_FF), F32)],
            compiler_params=_cparams(1),
            name="convglu_long",
        )(*common_args, cp)
        new_conv = tails.reshape(nseq, nblk, 8, D_FF)[:, -1, 6:8, :]
        return out, new_conv
    cpf = conv_prev.astype(F32)
    zeros = jnp.zeros((nseq, T, D_FF), F32)
    e1 = zeros.at[:, 0].set(cpf[:, 1]).reshape(M, D_FF)
    e2 = zeros.at[:, 0].set(cpf[:, 0]).at[:, 1].set(cpf[:, 1]).reshape(M, D_FF)
    out, gate = pl.pallas_call(
        kern,
        out_shape=(jax.ShapeDtypeStruct((M, D_MODEL), F32), jax.ShapeDtypeStruct((M, D_FF), F32)),
        grid=(M // tm,),
        in_specs=common_specs + [pl.BlockSpec((tm, D_FF), lambda i: (i, 0))] * 2,
        out_specs=(pl.BlockSpec((tm, D_MODEL), lambda i: (i, 0)), pl.BlockSpec((tm, D_FF), lambda i: (i, 0))),
        scratch_shapes=[pltpu.VMEM((tm, D_MODEL), BF16)],
        compiler_params=_cparams(1),
        name="convglu_short",
    )(*common_args, e1, e2)
    new_conv = gate.reshape(nseq, T, D_FF)[:, T - 2:, :]
    return out, new_conv


def _rope128(x128, cs):
    a = x128 * cs
    s = a + pltpu.roll(a, 64, 1)
    lane = lax.broadcasted_iota(jnp.int32, a.shape, 1)
    return jnp.where(lane < 64, s, 0.0)


def _latent_kernel(h_ref, g_ref, cs_ref, ckv_ref, kpe_ref, ckpe_ref):
    ckv = _rms(h_ref[:, :KV_LORA], g_ref[...])
    rp = _rope128(h_ref[:, KV_LORA:KV_LORA + 128], cs_ref[...])
    ckv_ref[...] = ckv
    kpe_ref[...] = rp[:, :ROPE_DIM]
    ckpe_ref[:, :KV_LORA] = ckv.astype(BF16)
    ckpe_ref[:, KV_LORA:] = rp.astype(BF16)


def _latent_post(h, norm_ckv, cs, *, T):
    M = h.shape[0]
    tm = _pos_tile(M, T)
    ncs = cs.shape[0] // tm
    return pl.pallas_call(
        _latent_kernel,
        out_shape=(jax.ShapeDtypeStruct((M, KV_LORA), F32), jax.ShapeDtypeStruct((M, ROPE_DIM), F32),
                   jax.ShapeDtypeStruct((M, KV_LORA + 128), BF16)),
        grid=(M // tm,),
        in_specs=[pl.BlockSpec((tm, KV_LORA + 128), lambda i: (i, 0)),
                  pl.BlockSpec((1, KV_LORA), lambda i: (0, 0)),
                  pl.BlockSpec((tm, 128), lambda i: (i % ncs, 0))],
        out_specs=(pl.BlockSpec((tm, KV_LORA), lambda i: (i, 0)), pl.BlockSpec((tm, ROPE_DIM), lambda i: (i, 0)),
                   pl.BlockSpec((tm, KV_LORA + 128), lambda i: (i, 0))),
        compiler_params=_cparams(1),
        name="latent_post",
    )(h, norm_ckv.reshape(1, KV_LORA).astype(F32), cs)


def _qpost_kernel(q_ref, cs_ref, o_ref):
    cs = cs_ref[...]
    for h in range(H_B):
        base = h * 256
        o_ref[:, base:base + 128] = q_ref[:, base:base + 128].astype(o_ref.dtype)
        o_ref[:, base + 128:base + 256] = _rope128(q_ref[:, base + 128:base + 256], cs).astype(o_ref.dtype)


def _q_post(q, cs, *, T, out_dtype):
    M = q.shape[0]
    tm = _pos_tile(M, T)
    ncs = cs.shape[0] // tm
    return pl.pallas_call(
        _qpost_kernel,
        out_shape=jax.ShapeDtypeStruct((M, H_B * 256), out_dtype),
        grid=(M // tm,),
        in_specs=[pl.BlockSpec((tm, H_B * 256), lambda i: (i, 0)),
                  pl.BlockSpec((tm, 128), lambda i: (i % ncs, 0))],
        out_specs=pl.BlockSpec((tm, H_B * 256), lambda i: (i, 0)),
        compiler_params=_cparams(1),
        name="q_post",
    )(q, cs)


def _flash_kernel(q_ref, k_ref, v_ref, o_ref, m_scr, l_scr, acc_scr, *, tq, tk):
    i = pl.program_id(1)
    j = pl.program_id(2)

    @pl.when(j == 0)
    def _():
        m_scr[...] = jnp.full(m_scr.shape, -jnp.inf, F32)
        l_scr[...] = jnp.zeros(l_scr.shape, F32)
        acc_scr[...] = jnp.zeros(acc_scr.shape, F32)

    hs = range(H_B)

    def block(masked):
        c = ATTN_SCALE * math.log2(math.e)
        ss = [_dg(q_ref[:, h * 256:(h + 1) * 256], k_ref[:, h * 256:(h + 1) * 256], NT) for h in hs]
        if masked:
            keep = lax.broadcasted_iota(jnp.int32, (tq, tk), 1) <= lax.broadcasted_iota(jnp.int32, (tq, tk), 0)
            ss = [jnp.where(keep, s, -jnp.inf) for s in ss]
        m_prev = [m_scr[h] for h in hs]
        m_new = [jnp.maximum(m_prev[h], jnp.max(ss[h], axis=-1, keepdims=True)) for h in hs]
        alpha = [jnp.exp2((m_prev[h] - m_new[h]) * c) for h in hs]
        ps = [jnp.exp2((ss[h] - m_new[h]) * c) for h in hs]
        pv = [_dg(ps[h].astype(BF16), v_ref[:, h * V_DIM:(h + 1) * V_DIM]) for h in hs]
        for h in hs:
            l_scr[h] = alpha[h] * l_scr[h] + jnp.sum(ps[h], axis=-1, keepdims=True)
            acc_scr[:, h * V_DIM:(h + 1) * V_DIM] = alpha[h] * acc_scr[:, h * V_DIM:(h + 1) * V_DIM] + pv[h]
            m_scr[h] = m_new[h]

    @pl.when(j < i)
    def _():
        block(False)

    @pl.when(j == i)
    def _():
        block(True)
        for h in hs:
            sl = slice(h * V_DIM, (h + 1) * V_DIM)
            o_ref[:, sl] = (acc_scr[:, sl] / l_scr[h]).astype(o_ref.dtype)


def _flash_causal(q, kv, *, nb, T):
    M = q.shape[0]
    tq = tk = 512
    nq = T // tq
    kw = H_B * 256
    vw = H_B * V_DIM
    return pl.pallas_call(
        functools.partial(_flash_kernel, tq=tq, tk=tk),
        out_shape=jax.ShapeDtypeStruct((M, vw), BF16),
        grid=(nb, nq, nq),
        in_specs=[pl.BlockSpec((tq, kw), lambda b, i, j: (b * nq + i, 0)),
                  pl.BlockSpec((tk, kw), lambda b, i, j: (b * nq + jnp.minimum(i, j), 0)),
                  pl.BlockSpec((tk, vw), lambda b, i, j: (b * nq + jnp.minimum(i, j), kw // vw))],
        out_specs=pl.BlockSpec((tq, vw), lambda b, i, j: (b * nq + i, 0)),
        scratch_shapes=[pltpu.VMEM((H_B, tq, 1), F32), pltpu.VMEM((H_B, tq, 1), F32), pltpu.VMEM((tq, vw), F32)],
        compiler_params=_cparams(3),
        name="mla_causal",
    )(q, kv, kv)


def _decode_kernel(pt_ref, q_ref, *refs, PP, n_steps, TS):
    ck_refs = refs[:PP]
    kp_refs = refs[PP:2 * PP]
    ckn_ref, kpn_ref, wuv_ref, o_ref, m_scr, l_scr, acc_scr = refs[2 * PP:]
    j = pl.program_id(1)
    R = q_ref.shape[1]

    @pl.when(j == 0)
    def _():
        m_scr[...] = jnp.full(m_scr.shape, -jnp.inf, F32)
        l_scr[...] = jnp.zeros(l_scr.shape, F32)
        acc_scr[...] = jnp.zeros(acc_scr.shape, F32)

    q_lat = q_ref[0, :, :KV_LORA]
    q_pe = q_ref[0, :, KV_LORA:]
    c = ATTN_SCALE * math.log2(math.e)

    cks = [ck_refs[p][0].astype(BF16) for p in range(PP)]
    kps = [kp_refs[p][0].astype(BF16) for p in range(PP)]
    ss = [_dg(q_lat, cks[p], NT) + _dg(q_pe, kps[p], NT) for p in range(PP)]
    pad = PAGE_SIZE - TS
    ckn = jnp.concatenate([ckn_ref[0], jnp.zeros((pad, KV_LORA), F32)], axis=0).astype(BF16)
    kpn = jnp.concatenate([kpn_ref[0], jnp.zeros((pad, ROPE_DIM), F32)], axis=0).astype(BF16)
    t_q = lax.broadcasted_iota(jnp.int32, (R, PAGE_SIZE), 0) % TS
    col = lax.broadcasted_iota(jnp.int32, (R, PAGE_SIZE), 1)
    last_shift = jnp.where(j == n_steps - 1, 0, -PAGE_SIZE)
    s_new = jnp.where(col <= t_q + last_shift, _dg(q_lat, ckn, NT) + _dg(q_pe, kpn, NT), -jnp.inf)
    ss.append(s_new)
    cks.append(ckn)

    tile_max = ss[0]
    for s in ss[1:]:
        tile_max = jnp.maximum(tile_max, s)
    m_prev = m_scr[...]
    m_new = jnp.maximum(m_prev, jnp.max(tile_max, axis=-1, keepdims=True))
    alpha = jnp.exp2((m_prev - m_new) * c)
    ps = [jnp.exp2((s - m_new) * c) for s in ss]
    p_sum = ps[0]
    for p in ps[1:]:
        p_sum = p_sum + p
    acc = alpha * acc_scr[...]
    for p, vv in zip(ps, cks):
        acc = acc + _dg(p.astype(BF16), vv)
    m_scr[...] = m_new
    l_scr[...] = alpha * l_scr[...] + jnp.sum(p_sum, axis=-1, keepdims=True)
    acc_scr[...] = acc

    @pl.when(j == n_steps - 1)
    def _():
        o_lat = acc_scr[...] / l_scr[...]
        for h in range(H_B):
            o_h = o_lat[h * TS:(h + 1) * TS, :].astype(BF16)
            o_ref[0, :, h * V_DIM:(h + 1) * V_DIM] = _dg(o_h, wuv_ref[h]).astype(o_ref.dtype)


def _decode_attn(page_table, qd, cache_ckv, cache_kpe, ckv_new, kpe_new, w_uv, *, PP=32):
    nb, n_pages = page_table.shape
    TS = ckv_new.shape[1]
    R = qd.shape[1]
    n_steps = n_pages // PP
    pt = page_table.reshape(-1).astype(jnp.int32)

    def page_spec(width, p):
        return pl.BlockSpec((1, PAGE_SIZE, width), lambda b, j, pt_ref: (pt_ref[b * n_pages + j * PP + p], 0, 0))

    in_specs = ([pl.BlockSpec((1, R, KV_LORA + ROPE_DIM), lambda b, j, pt_ref: (b, 0, 0))]
                + [page_spec(KV_LORA, p) for p in range(PP)]
                + [page_spec(ROPE_DIM, p) for p in range(PP)]
                + [pl.BlockSpec((1, TS, KV_LORA), lambda b, j, pt_ref: (b, 0, 0)),
                   pl.BlockSpec((1, TS, ROPE_DIM), lambda b, j, pt_ref: (b, 0, 0)),
                   pl.BlockSpec((H_B, KV_LORA, V_DIM), lambda b, j, pt_ref: (0, 0, 0))])
    grid_spec = pltpu.PrefetchScalarGridSpec(
        num_scalar_prefetch=1,
        grid=(nb, n_steps),
        in_specs=in_specs,
        out_specs=pl.BlockSpec((1, TS, H_B * V_DIM), lambda b, j, pt_ref: (b, 0, 0)),
        scratch_shapes=[pltpu.VMEM((R, 1), F32), pltpu.VMEM((R, 1), F32), pltpu.VMEM((R, KV_LORA), F32)],
    )
    return pl.pallas_call(
        functools.partial(_decode_kernel, PP=PP, n_steps=n_steps, TS=TS),
        out_shape=jax.ShapeDtypeStruct((nb, TS, H_B * V_DIM), BF16),
        grid_spec=grid_spec,
        compiler_params=_cparams(2),
        name="mla_decode",
    )(pt, qd, *([cache_ckv] * PP), *([cache_kpe] * PP), ckv_new, kpe_new, w_uv)


def _rope_table(pos):
    half = ROPE_DIM // 2
    inv = ROPE_BASE ** (-jnp.arange(half, dtype=F32) / half)
    ang = pos.astype(F32)[:, None] * inv[None, :]
    cos, sin = jnp.cos(ang), jnp.sin(ang)
    return jnp.concatenate([cos, cos, -sin, sin], axis=-1)


def _swap_halves(w):
    half = ROPE_DIM // 2
    return jnp.concatenate([w[..., half:], w[..., :half]], axis=-1)


def _prep_rwkv_weights(l, P):
    w_in = P["w_in_a"][l]
    mu = P["mu_a"][l]
    cuts = [0, MIX_A, MIX_A + DECAY_LORA, 2 * MIX_A + DECAY_LORA, 3 * MIX_A + DECAY_LORA,
            3 * MIX_A + DECAY_LORA + A_LORA, C_RWKV, C_A]
    seg = lambda a, i: a[..., cuts[i]:cuts[i + 1]]
    r_w, wl_w, k_w, v_w, al_w, gl_w, qm_w = [seg(w_in, i) for i in range(7)]
    r_m, wl_m, k_m, v_m, al_m, gl_m = [seg(mu, i) for i in range(6)]
    zc = lambda n: jnp.zeros((D_MODEL, n), F32)
    zm = lambda n: jnp.zeros((n,), F32)
    if l > 0:
        vl_w, vl_m = P["w_vres_in"][l - 1], P["mu_vres"][l - 1]
        wv_up, v0 = P["w_vres_up"][l - 1], P["v0"][l - 1]
    else:
        vl_w, vl_m = zc(VRES_LORA), zm(VRES_LORA)
        wv_up, v0 = jnp.zeros((VRES_LORA, MIX_A), F32), zm(MIX_A)
    w_full = jnp.concatenate([r_w, k_w, v_w, wl_w, al_w, gl_w, vl_w, zc(64), qm_w, zc(128)], axis=1)
    mu_full = jnp.concatenate([r_m, k_m, v_m, wl_m, al_m, gl_m, vl_m, zm(64), zm(MEM_WIDTH), zm(128)])
    zr = lambda n: jnp.zeros((n, MIX_A), F32)
    return dict(
        w_in=w_full.astype(BF16),
        mu=mu_full.reshape(1, PA_COLS),
        wd=jnp.concatenate([P["w_decay_up"][l], zr(A_LORA)], axis=0).astype(BF16),
        wa=jnp.concatenate([zr(DECAY_LORA), P["w_a_up"][l]], axis=0).astype(BF16),
        wg=jnp.concatenate([P["w_g_up"][l], zr(256 - GATE_LORA)], axis=0).astype(BF16),
        wv=jnp.concatenate([zr(GATE_LORA), wv_up, zr(256 - GATE_LORA - VRES_LORA)], axis=0).astype(BF16),
        w0=P["w0"][l].reshape(1, MIX_A), a0=P["a0"][l].reshape(1, MIX_A), v0=v0.reshape(1, MIX_A),
        k_k=P["k_k"][l].reshape(1, MIX_A), k_a=P["k_a"][l].reshape(1, MIX_A),
    )


def _prep_mla_weights(P):
    w_kv_a = P["w_kv_a"]
    w_kva = jnp.concatenate([w_kv_a, _swap_halves(w_kv_a[:, KV_LORA:])], axis=1).astype(BF16)
    w_kv_b = P["w_kv_b"]
    w_uk, w_uv = w_kv_b[..., :NOPE_DIM], w_kv_b[..., NOPE_DIM:]
    k_part = jnp.zeros((KV_LORA + 128, H_B, 256), F32)
    k_part = k_part.at[:KV_LORA, :, :NOPE_DIM].set(w_uk)
    eye = jnp.eye(ROPE_DIM, dtype=F32)
    k_part = k_part.at[KV_LORA:KV_LORA + ROPE_DIM, :, NOPE_DIM:NOPE_DIM + ROPE_DIM].set(
        jnp.broadcast_to(eye[:, None, :], (ROPE_DIM, H_B, ROPE_DIM)))
    v_part = jnp.zeros((KV_LORA + 128, H_B, V_DIM), F32).at[:KV_LORA].set(w_uv)
    w_kvx = jnp.concatenate([k_part.reshape(KV_LORA + 128, H_B * 256), v_part.reshape(KV_LORA + 128, H_B * V_DIM)],
                            axis=1).astype(BF16)
    w_q = []
    for j in range(N_B):
        wq = P["w_q_b"][j].reshape(Q_LORA, H_B, NOPE_DIM + ROPE_DIM)
        rope_w = wq[..., NOPE_DIM:]
        w_q.append(jnp.concatenate([wq[..., :NOPE_DIM], rope_w, _swap_halves(rope_w)], axis=-1)
                   .reshape(Q_LORA, H_B * 256).astype(BF16))
    return dict(
        w_kva=w_kva, w_kvx=w_kvx, w_q=w_q,
        w_ukT=jnp.transpose(w_uk, (1, 2, 0)).astype(BF16),
        w_uv=jnp.transpose(w_uv, (1, 0, 2)).astype(BF16),
    )


def _trunk(x, pos, mem_k, mem_v, wkv0, shift0, conv0, past, P, W, *, nseq, T):
    M = nseq * T
    x = x.reshape(M, D_MODEL)
    cs = _rope_table(pos)
    if T < ROW_TILES[0]:
        cs = jnp.tile(cs, (nseq, 1))
    chunk = 64 if T >= 64 else T
    new_wkv, new_shift, new_conv = [], [], []
    v_first = None
    ckv = kpe = ckpe = kvx = None
    for l in range(DEPTH):
        g_mix = P["norm_mix"][l]
        if l < N_A:
            pw = W["rwkv"][l]
            proj = _mm([x], [pw["w_in"]], g=g_mix, name="rwkv_in")
            x_last = x.reshape(nseq, T, D_MODEL)[:, -1]
            new_shift.append(_rmsnorm(x_last, g_mix, name="shift_norm"))
            sp = _mm([shift0[l].astype(F32)], [pw["w_in"]], name="rwkv_in_shift")
            r, lw, k, v, kn, b, g, q_mem = _rwkv_prep(proj, sp, pw, v_first if l > 0 else None, nseq=nseq, T=T)
            if l == 0:
                v_first = v
            h0 = wkv0[l].astype(F32)
            o_tok, h_new = _wkv(r, lw, k, v, kn, b, g, P["r_k"][l], P["lnx_w"][l], P["lnx_b"][l], h0,
                                nseq=nseq, T=T, C=chunk)
            new_wkv.append(h_new)
        else:
            j = l - N_A
            if l == N_A:
                hkv = _mm([x], [W["mla"]["w_kva"]], g=P["norm_kv"], name="kv_a")
                ckv, kpe, ckpe = _latent_post(hkv, P["norm_ckv"], cs, T=T)
                if past is None:
                    kvx = _mm([ckpe], [W["mla"]["w_kvx"]], out_dtype=BF16, name="kv_expand")
            proj = _mm([x], [W["w_in_b"][j]], g=g_mix, name="mla_in")
            qf = _mm([proj], [W["mla"]["w_q"][j]], g=P["norm_q"][j], name="q_b")
            q_mem = proj[:, Q_LORA:].astype(BF16)
            if past is None:
                q = _q_post(qf, cs, T=T, out_dtype=BF16)
                o_tok = _flash_causal(q, kvx, nb=nseq, T=T)
            else:
                q = _q_post(qf, cs, T=T, out_dtype=BF16).reshape(M, H_B, 256)
                q_lat = jnp.stack([_mm([q[:, h, :NOPE_DIM]], [W["mla"]["w_ukT"][h]], out_dtype=BF16, name="q_absorb")
                                   for h in range(H_B)], axis=1)
                qd = jnp.concatenate([q_lat, q[:, :, NOPE_DIM:NOPE_DIM + ROPE_DIM]], axis=-1)
                qd = qd.reshape(nseq, T, H_B, KV_LORA + ROPE_DIM).transpose(0, 2, 1, 3).reshape(nseq, H_B * T, -1)
                o_tok = _decode_attn(past[2], qd, past[0], past[1], ckv.reshape(nseq, T, KV_LORA),
                                     kpe.reshape(nseq, T, ROPE_DIM), W["mla"]["w_uv"]).reshape(M, H_B * V_DIM)
        o_mem = _mem_attn(q_mem, mem_k[l], mem_v[l], nb=nseq, T=T)
        x = _mm([o_tok, o_mem], [W["w_o_tok"][l], W["w_o_mem"][l]], res=x, name="w_o")
        x, c_new = _ffn(x, P["norm_ffn"][l], W["w_gate"][l], W["w_val"][l], P["conv_w"][l], P["conv_b"][l],
                        W["w_down"][l], conv0[l], nseq=nseq, T=T)
        new_conv.append(c_new)
    y = _rmsnorm(x, P["final_norm"], name="final_norm")
    return (y.reshape(nseq, T, D_MODEL), ckv.reshape(nseq, T, KV_LORA), kpe.reshape(nseq, T, ROPE_DIM),
            jnp.stack(new_wkv), jnp.stack(new_shift), jnp.stack(new_conv))


def kernel(x_prompt, x_sample, cache_ckv, cache_kpe, cache_mem_k, cache_mem_v, state_wkv, state_shift, state_conv, page_table, mem_prompt, norm_mix, norm_ffn, norm_mem, w_mem_kv, w_o, w_in_a, mu_a, w_vres_in, mu_vres, w_decay_up, w0, w_a_up, a0, w_g_up, w_vres_up, v0, k_k, k_a, r_k, lnx_w, lnx_b, norm_kv, w_kv_a, norm_ckv, w_kv_b, w_in_b, norm_q, w_q_b, w_ffn_up, conv_w, conv_b, w_ffn_down, final_norm):
    P = dict(norm_mix=norm_mix, norm_ffn=norm_ffn, w_o=w_o, w_in_a=w_in_a, mu_a=mu_a, w_vres_in=w_vres_in,
             mu_vres=mu_vres, w_decay_up=w_decay_up, w0=w0, w_a_up=w_a_up, a0=a0, w_g_up=w_g_up,
             w_vres_up=w_vres_up, v0=v0, k_k=k_k, k_a=k_a, r_k=r_k, lnx_w=lnx_w, lnx_b=lnx_b,
             norm_kv=norm_kv, w_kv_a=w_kv_a, norm_ckv=norm_ckv, w_kv_b=w_kv_b, w_in_b=w_in_b, norm_q=norm_q,
             w_q_b=w_q_b, conv_w=conv_w, conv_b=conv_b, final_norm=final_norm)
    W = dict(
        rwkv=[_prep_rwkv_weights(l, P) for l in range(N_A)],
        mla=_prep_mla_weights(P),
        w_in_b=[w_in_b[j].astype(BF16) for j in range(N_B)],
        w_o_tok=[w_o[l, :MIX_A].astype(BF16) for l in range(DEPTH)],
        w_o_mem=[w_o[l, MIX_A:].astype(BF16) for l in range(DEPTH)],
        w_gate=[w_ffn_up[l, :, :D_FF].astype(BF16) for l in range(DEPTH)],
        w_val=[w_ffn_up[l, :, D_FF:].astype(BF16) for l in range(DEPTH)],
        w_down=[w_ffn_down[l].astype(BF16) for l in range(DEPTH)],
    )
    dt = x_prompt.dtype
    B, T = x_prompt.shape[:2]
    n_mem = mem_prompt.shape[1]
    mem2d = mem_prompt.reshape(B * n_mem, D_MODEL)
    mem_k_all, mem_v_all = _mem_kv(mem2d, norm_mem, w_mem_kv)
    mem_k_p = [mem_k_all[l].reshape(B, n_mem, MEM_WIDTH) for l in range(DEPTH)]
    mem_v_p = [mem_v_all[l].reshape(B, n_mem, MEM_WIDTH) for l in range(DEPTH)]
    y_p, ckv_p, kpe_p, wkv_p, shift_p, conv_p = _trunk(
        x_prompt, jnp.arange(T), mem_k_p, mem_v_p,
        jnp.zeros((N_A, B, H_A, HEAD_A, HEAD_A), dt), jnp.zeros((N_A, B, D_MODEL), dt),
        jnp.zeros((DEPTH, B, CONV_W - 1, D_FF), dt), None, P, W, nseq=B, T=T)
    DB, TS = x_sample.shape[:2]
    past_len = page_table.shape[1] * PAGE_SIZE
    mk_s = [cache_mem_k[l].reshape(DB, n_mem, MEM_WIDTH) for l in range(DEPTH)]
    mv_s = [cache_mem_v[l].reshape(DB, n_mem, MEM_WIDTH) for l in range(DEPTH)]
    y_s, ckv_s, kpe_s, wkv_s, shift_s, conv_s = _trunk(
        x_sample, past_len + jnp.arange(TS), mk_s, mv_s, state_wkv, state_shift, state_conv,
        (cache_ckv, cache_kpe, page_table), P, W, nseq=DB, T=TS)
    mem_k_out = mem_k_all.reshape(DEPTH, B, n_mem, MEM_HEADS, MEM_HEAD_DIM)
    mem_v_out = mem_v_all.reshape(DEPTH, B, n_mem, MEM_HEADS, MEM_HEAD_DIM)
    return (y_p, y_s, ckv_p, kpe_p, mem_k_out, mem_v_out, wkv_p, shift_p, conv_p,
            ckv_s, kpe_s, wkv_s, shift_s, conv_s)
```

```python
import functools
import math

import jax
import jax.numpy as jnp
from jax import lax
from jax.experimental import pallas as pl
from jax.experimental.pallas import tpu as pltpu

F32 = jnp.float32
BF16 = jnp.bfloat16

D_MODEL = 1024
DEPTH = 4
N_A = 2
N_B = 2
MEM_WIDTH = 256
MIX_A = 768
HEAD_A = 64
H_A = 12
DECAY_LORA = 64
A_LORA = 64
VRES_LORA = 32
GATE_LORA = 160
GN_EPS = 64e-5
C_RWKV = 3 * MIX_A + DECAY_LORA + A_LORA + GATE_LORA
C_A = C_RWKV + MEM_WIDTH
MEM_HEADS = 4
MEM_HEAD_DIM = 64
MEM_SCALE = MEM_HEAD_DIM ** -0.5
NOPE_DIM = 128
ROPE_DIM = 64
V_DIM = 128
H_B = 6
Q_LORA = 256
KV_LORA = 256
ROPE_BASE = 10000.0
ATTN_SCALE = (NOPE_DIM + ROPE_DIM) ** -0.5
D_FF = 2816
CONV_W = 3
RMS_EPS = 1e-6
PAGE_SIZE = 128

PA_R, PA_K, PA_V = 0, 768, 1536
PA_LORA = 2304
PA_GATE = 2432
PA_QMEM = 2688
PA_COLS = 3072

VMEM_LIMIT_BYTES = 52 * 1024 * 1024

NN = ((1,), (0,))
NT = ((1,), (1,))
TN = ((0,), (0,))


def _dg(a, b, dims=NN):
    return lax.dot_general(a, b, (dims, ((), ())), preferred_element_type=F32)


def _split(x):
    hi = x.astype(BF16)
    lo = (x - hi.astype(F32)).astype(BF16)
    return hi, lo


def _dot3(a, b, dims=NN):
    ah, al = _split(a)
    bh, bl = _split(b)
    return _dg(ah, bh, dims) + (_dg(ah, bl, dims) + _dg(al, bh, dims))


def _transpose_exact(x, eye_bf16):
    hi = x.astype(BF16)
    r1 = x - hi.astype(F32)
    mid = r1.astype(BF16)
    lo = (r1 - mid.astype(F32)).astype(BF16)
    return (_dg(eye_bf16, hi, NT) + _dg(eye_bf16, mid, NT)) + _dg(eye_bf16, lo, NT)


def _cparams(n_axes):
    return pltpu.CompilerParams(dimension_semantics=("arbitrary",) * n_axes,
                                vmem_limit_bytes=VMEM_LIMIT_BYTES)


def _rms(x, g, eps=RMS_EPS):
    return x * lax.rsqrt(jnp.mean(x * x, axis=-1, keepdims=True) + eps) * g


def _pick_tile(n, candidates):
    for c in candidates:
        if n % c == 0:
            return c
    return n


ROW_TILES = (512, 256, 128, 64, 32, 16, 8)


def _pos_tile(M, T):
    return _pick_tile(T, ROW_TILES) if T >= ROW_TILES[0] else _pick_tile(M, ROW_TILES)


def _mm_kernel(*refs, n_in, has_norm, has_res, f32_in):
    xs = refs[:n_in]
    ws = refs[n_in:2 * n_in]
    k = 2 * n_in
    g_ref = refs[k] if has_norm else None
    k += int(has_norm)
    r_ref = refs[k] if has_res else None
    k += int(has_res)
    o_ref = refs[k]
    scr = list(refs[k + 1:])

    lhs = []
    si = 0
    for idx in range(n_in):
        if f32_in[idx]:
            s_ref = scr[si]
            si += 1

            @pl.when(pl.program_id(1) == 0)
            def _(x_ref=xs[idx], s_ref=s_ref, idx=idx):
                x = x_ref[...]
                if has_norm and idx == 0:
                    x = _rms(x, g_ref[...])
                s_ref[...] = x.astype(BF16)

            lhs.append(s_ref)
        else:
            lhs.append(xs[idx])
    acc = _dg(lhs[0][...], ws[0][...])
    for idx in range(1, n_in):
        acc = acc + _dg(lhs[idx][...], ws[idx][...])
    if has_res:
        acc = acc + r_ref[...]
    o_ref[...] = acc.astype(o_ref.dtype)


def _mm(xs, ws, *, g=None, res=None, out_dtype=F32, x_col_blocks=None, ks=None, name="mm"):
    n_in = len(xs)
    M = xs[0].shape[0]
    N = ws[0].shape[1]
    ks = [w.shape[0] for w in ws]
    x_col_blocks = x_col_blocks or [0] * n_in
    tm = _pick_tile(M, (512, 256, 128, 64, 32, 16, 8))
    tn = _pick_tile(N, (1024, 768, 512, 384, 256, 128))
    f32_in = tuple(x.dtype == F32 for x in xs)
    in_specs = []
    for x, kk, cb in zip(xs, ks, x_col_blocks):
        in_specs.append(pl.BlockSpec((tm, kk), lambda i, j, cb=cb: (i, cb)))
    for w, kk in zip(ws, ks):
        in_specs.append(pl.BlockSpec((kk, tn), lambda i, j: (0, j)))
    args = list(xs) + list(ws)
    if g is not None:
        in_specs.append(pl.BlockSpec((1, ks[0]), lambda i, j: (0, 0)))
        args.append(g.reshape(1, ks[0]).astype(F32))
    if res is not None:
        in_specs.append(pl.BlockSpec((tm, tn), lambda i, j: (i, j)))
        args.append(res)
    scratch = [pltpu.VMEM((tm, kk), BF16) for kk, f in zip(ks, f32_in) if f]
    return pl.pallas_call(
        functools.partial(_mm_kernel, n_in=n_in, has_norm=g is not None, has_res=res is not None,
                          f32_in=f32_in),
        out_shape=jax.ShapeDtypeStruct((M, N), out_dtype),
        grid=(M // tm, N // tn),
        in_specs=in_specs,
        out_specs=pl.BlockSpec((tm, tn), lambda i, j: (i, j)),
        scratch_shapes=scratch,
        compiler_params=_cparams(2),
        name=name,
    )(*args)


def _rmsnorm_kernel(x_ref, g_ref, o_ref):
    o_ref[...] = _rms(x_ref[...], g_ref[...])


def _rmsnorm(x, g, name="rmsnorm"):
    M, D = x.shape
    tm = _pick_tile(M, (512, 256, 128, 64, 32, 16, 8))
    return pl.pallas_call(
        _rmsnorm_kernel,
        out_shape=jax.ShapeDtypeStruct((M, D), F32),
        grid=(M // tm,),
        in_specs=[pl.BlockSpec((tm, D), lambda i: (i, 0)), pl.BlockSpec((1, D), lambda i: (0, 0))],
        out_specs=pl.BlockSpec((tm, D), lambda i: (i, 0)),
        compiler_params=_cparams(1),
        name=name,
    )(x, g.reshape(1, D).astype(F32))


def _mem_kv_kernel(x_ref, g_ref, wt_ref, k_ref, v_ref):
    kvt = _dg(wt_ref[0], _rms(x_ref[0], g_ref[0]).astype(BF16), NT)
    k_ref[0, 0] = kvt[:MEM_WIDTH]
    v_ref[0, 0] = kvt[MEM_WIDTH:]


def _mem_kv(mem, norm_mem, w_mem_kv):
    B, n_mem, _ = mem.shape
    out = jax.ShapeDtypeStruct((DEPTH, B, MEM_WIDTH, n_mem), F32)
    wt = jnp.swapaxes(w_mem_kv, 1, 2).astype(BF16)
    return pl.pallas_call(
        _mem_kv_kernel,
        out_shape=(out, out),
        grid=(DEPTH, B),
        in_specs=[pl.BlockSpec((1, n_mem, D_MODEL), lambda l, b: (b, 0, 0)),
                  pl.BlockSpec((1, 1, D_MODEL), lambda l, b: (l, 0, 0)),
                  pl.BlockSpec((1, 2 * MEM_WIDTH, D_MODEL), lambda l, b: (l, 0, 0))],
        out_specs=(pl.BlockSpec((1, 1, MEM_WIDTH, n_mem), lambda l, b: (l, b, 0, 0)),
                   pl.BlockSpec((1, 1, MEM_WIDTH, n_mem), lambda l, b: (l, b, 0, 0))),
        compiler_params=_cparams(2),
        name="mem_kv",
    )(mem, norm_mem.reshape(DEPTH, 1, D_MODEL).astype(F32), wt)


def _segsum64(x):
    r = lax.broadcasted_iota(jnp.int32, (128, 128), 0) >> 6
    c = lax.broadcasted_iota(jnp.int32, (128, 128), 1) >> 6
    ones = jnp.where(r == c, 1.0, 0.0).astype(BF16)
    outs = []
    for j in range(x.shape[1] // 128):
        hi, lo = _split(x[:, j * 128:(j + 1) * 128])
        outs.append(_dg(hi, ones) + _dg(lo, ones))
    return jnp.concatenate(outs, axis=1)


def _softplus(x):
    return jnp.maximum(x, 0.0) + jnp.log1p(jnp.exp(-jnp.abs(x)))


def _prep_kernel(*refs, has_vres, blocks_per_seq):
    (cur_ref, prev_ref, mu_ref, wd_ref, wa_ref, wg_ref, wv_ref, w0_ref, a0_ref, v0_ref, kk_ref, ka_ref) = refs[:12]
    k = 12
    vf_ref = refs[k] if has_vres else None
    k += int(has_vres)
    r_o, lw_o, k_o, v_o, kn_o, b_o, g_o, qm_o = refs[k:k + 8]
    tm = cur_ref.shape[0]

    if blocks_per_seq:
        carry_scr = refs[k + 8]

        @pl.when(pl.program_id(0) % blocks_per_seq == 0)
        def _():
            carry_scr[...] = prev_ref[0]

        first_row = lax.broadcasted_iota(jnp.int32, (tm, 1), 0) == 0

    def mix(lo, hi):
        c = cur_ref[:, lo:hi]
        if blocks_per_seq:
            p = jnp.where(first_row, carry_scr[:, lo:hi], pltpu.roll(c, 1, 0))
        else:
            p = prev_ref[:, lo:hi]
        return c + (p - c) * mu_ref[:, lo:hi]

    lora = mix(PA_LORA, PA_LORA + 128)
    zw = _dg(jnp.tanh(lora).astype(BF16), wd_ref[...]) + w0_ref[...]
    w_log = -_softplus(-zw) - 0.5
    lw_o[...] = -jnp.exp(w_log)
    a = jax.nn.sigmoid(_dg(lora.astype(BF16), wa_ref[...]) + a0_ref[...])
    gin = mix(PA_GATE, PA_GATE + 256)
    g_o[...] = _dg(jax.nn.sigmoid(gin).astype(BF16), wg_ref[...])
    r_o[...] = mix(PA_R, PA_R + MIX_A)
    kx = mix(PA_K, PA_K + MIX_A)
    v = mix(PA_V, PA_V + MIX_A)
    if has_vres:
        sv = jax.nn.sigmoid(_dg(gin.astype(BF16), wv_ref[...]) + v0_ref[...])
        v = v + (vf_ref[...] - v) * sv
    v_o[...] = v
    kk = kx * kk_ref[...]
    kk = kk * lax.rsqrt(jnp.maximum(_segsum64(kk * kk), 1e-24))
    kn_o[...] = kk
    b_o[...] = kk * a
    k_o[...] = kx * (1.0 + (a - 1.0) * ka_ref[...])
    qm_o[...] = cur_ref[:, PA_QMEM:PA_QMEM + MEM_WIDTH].astype(BF16)
    if blocks_per_seq:
        carry_scr[...] = cur_ref[tm - 1:tm, :]


def _rwkv_prep(cur, sp, pw, v_first, *, nseq, T):
    M = cur.shape[0]
    has_vres = v_first is not None
    if T >= 256:
        tm = 256
        blocks_per_seq = T // tm
        prev = sp.reshape(nseq, 1, PA_COLS)
        prev_spec = pl.BlockSpec((1, 1, PA_COLS), lambda i: (i // blocks_per_seq, 0, 0))
        scratch = [pltpu.VMEM((1, PA_COLS), F32)]
    else:
        tm = _pick_tile(M, (256, 128, 64, 32, 16, 8))
        blocks_per_seq = 0
        prev = jnp.concatenate([sp[:, None, :], cur.reshape(nseq, T, PA_COLS)[:, :-1]], axis=1).reshape(M, PA_COLS)
        prev_spec = pl.BlockSpec((tm, PA_COLS), lambda i: (i, 0))
        scratch = []
    row = lambda w: pl.BlockSpec((tm, w), lambda i: (i, 0))
    full = lambda a: pl.BlockSpec(a.shape, lambda i: (0,) * a.ndim)
    consts = [pw["mu"], pw["wd"], pw["wa"], pw["wg"], pw["wv"], pw["w0"], pw["a0"], pw["v0"], pw["k_k"], pw["k_a"]]
    args = [cur, prev] + consts
    in_specs = [row(PA_COLS), prev_spec] + [full(a) for a in consts]
    if has_vres:
        args.append(v_first)
        in_specs.append(row(MIX_A))
    outs = [jax.ShapeDtypeStruct((M, MIX_A), F32)] * 7 + [jax.ShapeDtypeStruct((M, MEM_WIDTH), BF16)]
    out_specs = [row(MIX_A)] * 7 + [row(MEM_WIDTH)]
    return pl.pallas_call(
        functools.partial(_prep_kernel, has_vres=has_vres, blocks_per_seq=blocks_per_seq),
        out_shape=outs,
        grid=(M // tm,),
        in_specs=in_specs,
        out_specs=out_specs,
        scratch_shapes=scratch,
        compiler_params=_cparams(1),
        name="rwkv_prep",
    )(*args)


def _wkv_kernel(r_ref, lw_ref, k_ref, v_ref, kn_ref, b_ref, g_ref, rk_ref, lnw_ref, lnb_ref, h0_ref,
                o_ref, hT_ref, h_scr, *, C, H, N, nchunks, xpose):
    c_idx = pl.program_id(1)

    eye_bf = jnp.where(lax.broadcasted_iota(jnp.int32, (N, N), 0) == lax.broadcasted_iota(jnp.int32, (N, N), 1),
                       1.0, 0.0).astype(BF16)

    @pl.when(c_idx == 0)
    def _():
        if xpose:
            for h in range(H):
                h_scr[h] = _transpose_exact(h0_ref[0, h], eye_bf)
        else:
            h_scr[...] = h0_ref[0]

    lw = lw_ref[...]
    ti = lax.broadcasted_iota(jnp.int32, (C, C), 0)
    tj = lax.broadcasted_iota(jnp.int32, (C, C), 1)
    ltri = jnp.where(tj <= ti, 1.0, 0.0).astype(BF16)
    lw_hi, lw_lo = _split(lw)
    cum = _dg(ltri, lw_hi) + _dg(ltri, lw_lo)
    cum_last = jnp.sum(lw, axis=0, keepdims=True)
    e_neg = jnp.exp(-cum)
    r_all = r_ref[...]
    k_all = k_ref[...]
    v_all = v_ref[...]
    kt_all = kn_ref[...] * jnp.exp(cum - lw)
    rt_all = r_all * jnp.exp(cum)
    kh_all = k_all * e_neg
    bh_all = b_ref[...] * e_neg
    e_rem = jnp.exp(cum_last - cum)
    kb_all = k_all * e_rem
    bb_all = b_ref[...] * e_rem
    p_last = jnp.exp(cum_last)
    rkr = r_all * k_all * rk_ref[...]

    strict = tj < ti
    incl = tj <= ti
    ni = lax.broadcasted_iota(jnp.int32, (N, N), 0)
    nj = lax.broadcasted_iota(jnp.int32, (N, N), 1)
    eye_n = ni == nj
    eye_c = jnp.where(ti == tj, 1.0, 0.0)

    hs = range(H)
    sls = [slice(h * N, (h + 1) * N) for h in hs]
    cut = lambda a: [a[:, sl].astype(BF16) for sl in sls]
    kt, rt, kh, bh, kb, bb, vv = (cut(a) for a in (kt_all, rt_all, kh_all, bh_all, kb_all, bb_all, v_all))
    a_k = [jnp.where(strict, _dg(kt[h], kh[h], NT), 0.0).astype(BF16) for h in hs]
    a_b = [jnp.where(strict, _dg(kt[h], bh[h], NT), 0.0) for h in hs]
    q_k = [jnp.where(incl, _dg(rt[h], kh[h], NT), 0.0).astype(BF16) for h in hs]
    q_b = [jnp.where(incl, _dg(rt[h], bh[h], NT), 0.0).astype(BF16) for h in hs]
    tinv = [eye_c - jnp.where(((ti >> 1) == (tj >> 1)), a_b[h], 0.0) for h in hs]
    blk = 2
    while blk < C:
        sh = blk.bit_length() - 1
        lvl = ((ti >> (sh + 1)) == (tj >> (sh + 1))) & (((ti >> sh) & 1) == 1) & (((tj >> sh) & 1) == 0)
        t16 = [tinv[h].astype(BF16) for h in hs]
        mt = [_dg(jnp.where(lvl, a_b[h], 0.0).astype(BF16), t16[h]).astype(BF16) for h in hs]
        tinv = [tinv[h] - _dg(t16[h], mt[h]) for h in hs]
        blk *= 2
    t16 = [tinv[h].astype(BF16) for h in hs]
    akv = [_dg(a_k[h], vv[h]).astype(BF16) for h in hs]
    x = [_dg(t16[h], jnp.concatenate([kt[h], akv[h]], axis=1)).astype(BF16) for h in hs]
    qw = [_dg(q_b[h], x[h]) for h in hs]
    r2 = [(rt_all[:, sls[h]] - qw[h][:, :N]).astype(BF16) for h in hs]
    y0 = [_dg(q_k[h], vv[h]) - qw[h][:, N:] for h in hs]
    bw = [_dg(bb[h], x[h], TN) for h in hs]
    gmat = [(jnp.where(eye_n, jnp.broadcast_to(p_last[:, sls[h]], (N, N)), 0.0) - bw[h][:, :N]).astype(BF16) for h in hs]
    h_add = [_dg(kb[h], vv[h], TN) - bw[h][:, N:] for h in hs]
    h0 = [h_scr[h].astype(BF16) for h in hs]
    y = [_dg(r2[h], h0[h]) + y0[h] for h in hs]
    for h in hs:
        h_scr[h] = _dg(gmat[h], h0[h]) + h_add[h]
    for h in hs:
        sl = sls[h]
        mean = jnp.mean(y[h], axis=-1, keepdims=True)
        yc = y[h] - mean
        var = jnp.mean(yc * yc, axis=-1, keepdims=True)
        yn = yc * lax.rsqrt(var + GN_EPS)
        yn = yn * lnw_ref[:, sl] + lnb_ref[:, sl]
        bonus = jnp.sum(rkr[:, sl], axis=-1, keepdims=True) * v_all[:, sl]
        o_ref[:, sl] = ((yn + bonus) * g_ref[:, sl]).astype(o_ref.dtype)

    @pl.when(c_idx == nchunks - 1)
    def _():
        if xpose:
            for h in range(H):
                hT_ref[0, h] = _transpose_exact(h_scr[h], eye_bf)
        else:
            hT_ref[0] = h_scr[...]


def _wkv(r, lw, k, v, kn, b, g, r_k, lnx_w, lnx_b, h0, *, nseq, T, C):
    M = r.shape[0]
    nchunks = T // C
    HN = MIX_A
    xpose = nchunks > 1
    h_in = h0 if xpose else jnp.swapaxes(h0, -1, -2)
    row = pl.BlockSpec((C, HN), lambda s, c: (s * nchunks + c, 0))
    vec = pl.BlockSpec((1, HN), lambda s, c: (0, 0))
    st = pl.BlockSpec((1, H_A, HEAD_A, HEAD_A), lambda s, c: (s, 0, 0, 0))
    o, h_new = pl.pallas_call(
        functools.partial(_wkv_kernel, C=C, H=H_A, N=HEAD_A, nchunks=nchunks, xpose=xpose),
        out_shape=(jax.ShapeDtypeStruct((M, HN), BF16),
                   jax.ShapeDtypeStruct((nseq, H_A, HEAD_A, HEAD_A), F32)),
        grid=(nseq, nchunks),
        in_specs=[row] * 7 + [vec] * 3 + [st],
        out_specs=(row, st),
        scratch_shapes=[pltpu.VMEM((H_A, HEAD_A, HEAD_A), F32)],
        compiler_params=_cparams(2),
        name="wkv7",
    )(r, lw, k, v, kn, b, g, r_k.reshape(1, HN), lnx_w.reshape(1, HN), lnx_b.reshape(1, HN), h_in)
    return o, (h_new if xpose else jnp.swapaxes(h_new, -1, -2))


def _mem_attn_kernel(q_ref, k_ref, v_ref, o_ref, *, nbb, tq):
    c = MEM_SCALE * math.log2(math.e)
    lane_head = lax.broadcasted_iota(jnp.int32, (1, MEM_WIDTH), 1) >> 6
    items = [(bb, h) for bb in range(nbb) for h in range(MEM_HEADS)]
    q_all = q_ref[...].astype(F32)
    qs = [q_all[bb * tq:(bb + 1) * tq, :] for bb in range(nbb)]
    kts = [k_ref[0, bb].astype(BF16) for bb in range(nbb)]
    vts = [v_ref[0, bb].astype(BF16) for bb in range(nbb)]
    ss = [_dg(jnp.where(lane_head == h, qs[bb], 0.0).astype(BF16), kts[bb]) for bb, h in items]
    ms = [jnp.max(s, axis=-1, keepdims=True) for s in ss]
    ps = [jnp.exp2((s - m) * c) for s, m in zip(ss, ms)]
    inv_l = [1.0 / jnp.sum(p, axis=-1, keepdims=True) for p in ps]
    pv = [_dg(p.astype(BF16), vts[bb], NT) for p, (bb, h) in zip(ps, items)]
    outs = []
    for bb in range(nbb):
        o = jnp.zeros((tq, MEM_WIDTH), F32)
        for h in range(MEM_HEADS):
            idx = bb * MEM_HEADS + h
            o = jnp.where(lane_head == h, pv[idx] * inv_l[idx], o)
        outs.append(o)
    o_ref[...] = (outs[0] if nbb == 1 else jnp.concatenate(outs, axis=0)).astype(o_ref.dtype)


def _mem_attn(q, mem_k, mem_v, layer, *, nb, T):
    M = q.shape[0]
    if T >= ROW_TILES[0]:
        tq, nbb = ROW_TILES[0], 1
    else:
        tq, nbb = T, _pick_tile(nb, (16, 8, 4, 2, 1))
    nq = T // tq
    n_mem = mem_k.shape[3]
    return pl.pallas_call(
        functools.partial(_mem_attn_kernel, nbb=nbb, tq=tq),
        out_shape=jax.ShapeDtypeStruct((M, MEM_WIDTH), BF16),
        grid=(nb // nbb, nq),
        in_specs=[pl.BlockSpec((nbb * tq, MEM_WIDTH), lambda b, i: (b * nq + i, 0)),
                  pl.BlockSpec((1, nbb, MEM_WIDTH, n_mem), lambda b, i: (layer, b, 0, 0)),
                  pl.BlockSpec((1, nbb, MEM_WIDTH, n_mem), lambda b, i: (layer, b, 0, 0))],
        out_specs=pl.BlockSpec((nbb * tq, MEM_WIDTH), lambda b, i: (b * nq + i, 0)),
        compiler_params=_cparams(2),
        name="mem_attn",
    )(q, mem_k, mem_v)


def _ffn_kernel(*refs, tm, T, long_mode, col_chunks):
    if long_mode:
        (x_ref, g_ref, wg_ref, wv_ref, cw_ref, cb_ref, wd_ref, cp_ref, o_ref, tail_ref, xn_scr, carry_scr) = refs
    else:
        (x_ref, g_ref, wg_ref, wv_ref, cw_ref, cb_ref, wd_ref, cp_ref, o_ref, gate_ref, xn_scr) = refs
    xn_scr[...] = _rms(x_ref[...], g_ref[...]).astype(BF16)
    xn = xn_scr[...]
    row = lax.broadcasted_iota(jnp.int32, (tm, 1), 0)
    if long_mode:
        nblk = T // tm

        @pl.when(pl.program_id(0) % nblk == 0)
        def _():
            carry_scr[...] = cp_ref[0]
    else:
        t_in_seq = row % T
        ncp = cp_ref.shape[0]
        rr = lax.broadcasted_iota(jnp.int32, (tm, ncp), 0)
        cc = lax.broadcasted_iota(jnp.int32, (tm, ncp), 1)
        same_seq = (rr // T) == (cc >> 1)
        sel1 = jnp.where(same_seq & ((rr % T) + 1 == (cc & 1)), 1.0, 0.0).astype(BF16)
        sel2 = jnp.where(same_seq & ((rr % T) == (cc & 1)), 1.0, 0.0).astype(BF16)
        sel = jnp.concatenate([sel1, sel2], axis=0)
    acc = x_ref[...]
    for (lo, w) in col_chunks:
        cs = slice(lo, lo + w)
        gate = _dg(xn, wg_ref[:, cs])
        val = _dg(xn, wv_ref[:, cs])
        r1 = pltpu.roll(gate, 1, 0)
        r2 = pltpu.roll(gate, 2, 0)
        if long_mode:
            c0 = carry_scr[6:7, cs]
            c1 = carry_scr[7:8, cs]
            p1 = jnp.where(row == 0, c1, r1)
            p2 = jnp.where(row == 0, c0, jnp.where(row == 1, c1, r2))
            tail = gate[tm - 8:tm, :]
            carry_scr[:, cs] = tail
            tail_ref[0, :, cs] = tail
        else:
            cpc = cp_ref[:, cs]
            hi = cpc.astype(BF16)
            r_hi = cpc - hi.astype(F32)
            mid = r_hi.astype(BF16)
            lo = (r_hi - mid.astype(F32)).astype(BF16)
            edge = (_dg(sel, hi) + _dg(sel, mid)) + _dg(sel, lo)
            p1 = jnp.where(t_in_seq >= 1, r1, 0.0) + edge[:tm]
            p2 = jnp.where(t_in_seq >= 2, r2, 0.0) + edge[tm:]
            gate_ref[:, cs] = gate
        conv = cb_ref[:, cs] + p2 * cw_ref[0:1, cs]
        conv = conv + p1 * cw_ref[1:2, cs]
        conv = conv + gate * cw_ref[2:3, cs]
        hmid = (conv * jax.nn.sigmoid(conv) * val).astype(BF16)
        acc = acc + _dg(hmid, wd_ref[cs, :])
    o_ref[...] = acc


def _ffn(x, g, w_gate, w_val, conv_w, conv_b, w_down, conv_prev, *, nseq, T):
    M = x.shape[0]
    long_mode = T >= 512
    tm = 512 if long_mode else 256
    assert (T % tm == 0) if long_mode else (tm % T == 0 and M % tm == 0)
    col_chunks = tuple((lo, min(256, D_FF - lo)) for lo in range(0, D_FF, 256))
    const = lambda a: pl.BlockSpec(a.shape, lambda i: (0,) * a.ndim, pipeline_mode=pl.Buffered(1))
    small = lambda a: pl.BlockSpec(a.shape, lambda i: (0,) * a.ndim)
    g2 = g.reshape(1, D_MODEL).astype(F32)
    cb2 = conv_b.reshape(1, D_FF).astype(F32)
    common_specs = [pl.BlockSpec((tm, D_MODEL), lambda i: (i, 0)), small(g2), const(w_gate), const(w_val),
                    small(conv_w), small(cb2), const(w_down)]
    common_args = [x, g2, w_gate, w_val, conv_w, cb2, w_down]
    kern = functools.partial(_ffn_kernel, tm=tm, T=T, long_mode=long_mode, col_chunks=col_chunks)
    if long_mode:
        nblk = T // tm
        cp = jnp.concatenate([jnp.zeros((nseq, 6, D_FF), F32), conv_prev.astype(F32)], axis=1)
        out, tails = pl.pallas_call(
            kern,
            out_shape=(jax.ShapeDtypeStruct((M, D_MODEL), F32), jax.ShapeDtypeStruct((M // tm, 8, D_FF), F32)),
            grid=(M // tm,),
            in_specs=common_specs + [pl.BlockSpec((1, 8, D_FF), lambda i: (i // nblk, 0, 0))],
            out_specs=(pl.BlockSpec((tm, D_MODEL), lambda i: (i, 0)),
                       pl.BlockSpec((1, 8, D_FF), lambda i: (i, 0, 0))),
            scratch_shapes=[pltpu.VMEM((tm, D_MODEL), BF16), pltpu.VMEM((8, D_FF), F32)],
            compiler_params=_cparams(1),
            name="convglu_long",
        )(*common_args, cp)
        new_conv = tails.reshape(nseq, nblk, 8, D_FF)[:, -1, 6:8, :]
        return out, new_conv
    ncp = (CONV_W - 1) * (tm // T)
    out, gate = pl.pallas_call(
        kern,
        out_shape=(jax.ShapeDtypeStruct((M, D_MODEL), F32), jax.ShapeDtypeStruct((M, D_FF), F32)),
        grid=(M // tm,),
        in_specs=common_specs + [pl.BlockSpec((ncp, D_FF), lambda i: (i, 0))],
        out_specs=(pl.BlockSpec((tm, D_MODEL), lambda i: (i, 0)), pl.BlockSpec((tm, D_FF), lambda i: (i, 0))),
        scratch_shapes=[pltpu.VMEM((tm, D_MODEL), BF16)],
        compiler_params=_cparams(1),
        name="convglu_short",
    )(*common_args, conv_prev.astype(F32).reshape(nseq * (CONV_W - 1), D_FF))
    new_conv = gate.reshape(nseq, T, D_FF)[:, T - 2:, :]
    return out, new_conv


def _rope128(x128, cs):
    a = x128 * cs
    s = a + pltpu.roll(a, 64, 1)
    lane = lax.broadcasted_iota(jnp.int32, a.shape, 1)
    return jnp.where(lane < 64, s, 0.0)


def _latent_kernel(h_ref, g_ref, cs_ref, ckv_ref, kpe_ref, ckpe_ref):
    ckv = _rms(h_ref[:, :KV_LORA], g_ref[...])
    rp = _rope128(h_ref[:, KV_LORA:KV_LORA + 128], cs_ref[...])
    ckv_ref[...] = ckv
    kpe_ref[...] = rp[:, :ROPE_DIM]
    ckpe_ref[:, :KV_LORA] = ckv.astype(BF16)
    ckpe_ref[:, KV_LORA:] = rp.astype(BF16)


def _latent_post(h, norm_ckv, cs, *, T):
    M = h.shape[0]
    tm = _pos_tile(M, T)
    ncs = cs.shape[0] // tm
    return pl.pallas_call(
        _latent_kernel,
        out_shape=(jax.ShapeDtypeStruct((M, KV_LORA), F32), jax.ShapeDtypeStruct((M, ROPE_DIM), F32),
                   jax.ShapeDtypeStruct((M, KV_LORA + 128), BF16)),
        grid=(M // tm,),
        in_specs=[pl.BlockSpec((tm, KV_LORA + 128), lambda i: (i, 0)),
                  pl.BlockSpec((1, KV_LORA), lambda i: (0, 0)),
                  pl.BlockSpec((tm, 128), lambda i: (i % ncs, 0))],
        out_specs=(pl.BlockSpec((tm, KV_LORA), lambda i: (i, 0)), pl.BlockSpec((tm, ROPE_DIM), lambda i: (i, 0)),
                   pl.BlockSpec((tm, KV_LORA + 128), lambda i: (i, 0))),
        compiler_params=_cparams(1),
        name="latent_post",
    )(h, norm_ckv.reshape(1, KV_LORA).astype(F32), cs)


def _qpost_kernel(q_ref, cs_ref, o_ref):
    cs = cs_ref[...]
    for h in range(H_B):
        base = h * 256
        o_ref[:, base:base + 128] = q_ref[:, base:base + 128].astype(o_ref.dtype)
        o_ref[:, base + 128:base + 256] = _rope128(q_ref[:, base + 128:base + 256], cs).astype(o_ref.dtype)


def _q_post(q, cs, *, T, out_dtype):
    M = q.shape[0]
    tm = _pos_tile(M, T)
    ncs = cs.shape[0] // tm
    return pl.pallas_call(
        _qpost_kernel,
        out_shape=jax.ShapeDtypeStruct((M, H_B * 256), out_dtype),
        grid=(M // tm,),
        in_specs=[pl.BlockSpec((tm, H_B * 256), lambda i: (i, 0)),
                  pl.BlockSpec((tm, 128), lambda i: (i % ncs, 0))],
        out_specs=pl.BlockSpec((tm, H_B * 256), lambda i: (i, 0)),
        compiler_params=_cparams(1),
        name="q_post",
    )(q, cs)


def _flash_kernel(q_ref, k_ref, v_ref, o_ref, m_scr, l_scr, acc_scr, *, tq, tk):
    i = pl.program_id(1)
    j = pl.program_id(2)

    @pl.when(j == 0)
    def _():
        m_scr[...] = jnp.full(m_scr.shape, -jnp.inf, F32)
        l_scr[...] = jnp.zeros(l_scr.shape, F32)
        acc_scr[...] = jnp.zeros(acc_scr.shape, F32)

    hs = range(H_B)

    def block(masked):
        c = ATTN_SCALE * math.log2(math.e)
        ss = [_dg(q_ref[:, h * 256:(h + 1) * 256], k_ref[:, h * 256:(h + 1) * 256], NT) for h in hs]
        if masked:
            keep = lax.broadcasted_iota(jnp.int32, (tq, tk), 1) <= lax.broadcasted_iota(jnp.int32, (tq, tk), 0)
            ss = [jnp.where(keep, s, -jnp.inf) for s in ss]
        m_prev = [m_scr[h] for h in hs]
        m_new = [jnp.maximum(m_prev[h], jnp.max(ss[h], axis=-1, keepdims=True)) for h in hs]
        alpha = [jnp.exp2((m_prev[h] - m_new[h]) * c) for h in hs]
        ps = [jnp.exp2((ss[h] - m_new[h]) * c) for h in hs]
        pv = [_dg(ps[h].astype(BF16), v_ref[:, h * V_DIM:(h + 1) * V_DIM]) for h in hs]
        for h in hs:
            l_scr[h] = alpha[h] * l_scr[h] + jnp.sum(ps[h], axis=-1, keepdims=True)
            acc_scr[:, h * V_DIM:(h + 1) * V_DIM] = alpha[h] * acc_scr[:, h * V_DIM:(h + 1) * V_DIM] + pv[h]
            m_scr[h] = m_new[h]

    @pl.when(j < i)
    def _():
        block(False)

    @pl.when(j == i)
    def _():
        block(True)
        for h in hs:
            sl = slice(h * V_DIM, (h + 1) * V_DIM)
            o_ref[:, sl] = (acc_scr[:, sl] / l_scr[h]).astype(o_ref.dtype)


def _flash_causal(q, kv, *, nb, T):
    M = q.shape[0]
    tq = tk = 512
    nq = T // tq
    kw = H_B * 256
    vw = H_B * V_DIM
    return pl.pallas_call(
        functools.partial(_flash_kernel, tq=tq, tk=tk),
        out_shape=jax.ShapeDtypeStruct((M, vw), BF16),
        grid=(nb, nq, nq),
        in_specs=[pl.BlockSpec((tq, kw), lambda b, i, j: (b * nq + i, 0)),
                  pl.BlockSpec((tk, kw), lambda b, i, j: (b * nq + jnp.minimum(i, j), 0)),
                  pl.BlockSpec((tk, vw), lambda b, i, j: (b * nq + jnp.minimum(i, j), kw // vw))],
        out_specs=pl.BlockSpec((tq, vw), lambda b, i, j: (b * nq + i, 0)),
        scratch_shapes=[pltpu.VMEM((H_B, tq, 1), F32), pltpu.VMEM((H_B, tq, 1), F32), pltpu.VMEM((tq, vw), F32)],
        compiler_params=_cparams(3),
        name="mla_causal",
    )(q, kv, kv)


def _decode_kernel(pt_ref, q_ref, *refs, PP, n_steps, TS):
    ck_refs = refs[:PP]
    kp_refs = refs[PP:2 * PP]
    ckn_ref, kpn_ref, wuv_ref, o_ref, m_scr, l_scr, acc_scr = refs[2 * PP:]
    j = pl.program_id(1)
    R = q_ref.shape[1]

    @pl.when(j == 0)
    def _():
        m_scr[...] = jnp.full(m_scr.shape, -jnp.inf, F32)
        l_scr[...] = jnp.zeros(l_scr.shape, F32)
        acc_scr[...] = jnp.zeros(acc_scr.shape, F32)

    q_lat = q_ref[0, :, :KV_LORA]
    q_pe = q_ref[0, :, KV_LORA:]
    c = ATTN_SCALE * math.log2(math.e)

    cks = [ck_refs[p][0].astype(BF16) for p in range(PP)]
    kps = [kp_refs[p][0].astype(BF16) for p in range(PP)]
    ss = [_dg(q_lat, cks[p], NT) + _dg(q_pe, kps[p]) for p in range(PP)]
    pad = PAGE_SIZE - TS
    ckn = jnp.concatenate([ckn_ref[0], jnp.zeros((pad, KV_LORA), F32)], axis=0).astype(BF16)
    kpn = jnp.concatenate([kpn_ref[0], jnp.zeros((pad, ROPE_DIM), F32)], axis=0).astype(BF16)
    t_q = lax.broadcasted_iota(jnp.int32, (R, PAGE_SIZE), 0) % TS
    col = lax.broadcasted_iota(jnp.int32, (R, PAGE_SIZE), 1)
    last_shift = jnp.where(j == n_steps - 1, 0, -PAGE_SIZE)
    s_new = jnp.where(col <= t_q + last_shift, _dg(q_lat, ckn, NT) + _dg(q_pe, kpn, NT), -jnp.inf)
    ss.append(s_new)
    cks.append(ckn)

    tile_max = ss[0]
    for s in ss[1:]:
        tile_max = jnp.maximum(tile_max, s)
    m_prev = m_scr[...]
    m_new = jnp.maximum(m_prev, jnp.max(tile_max, axis=-1, keepdims=True))
    alpha = jnp.exp2((m_prev - m_new) * c)
    ps = [jnp.exp2((s - m_new) * c) for s in ss]
    p_sum = ps[0]
    for p in ps[1:]:
        p_sum = p_sum + p
    acc = alpha * acc_scr[...]
    for p, vv in zip(ps, cks):
        acc = acc + _dg(p.astype(BF16), vv)
    m_scr[...] = m_new
    l_scr[...] = alpha * l_scr[...] + jnp.sum(p_sum, axis=-1, keepdims=True)
    acc_scr[...] = acc

    @pl.when(j == n_steps - 1)
    def _():
        o_lat = acc_scr[...] / l_scr[...]
        for h in range(H_B):
            o_h = o_lat[h * TS:(h + 1) * TS, :].astype(BF16)
            o_ref[0, :, h * V_DIM:(h + 1) * V_DIM] = _dg(o_h, wuv_ref[h]).astype(o_ref.dtype)


def _decode_attn(page_table, qd, cache_ckv, cache_kpe, ckv_new, kpe_new, w_uv, *, PP=32):
    nb, n_pages = page_table.shape
    TS = ckv_new.shape[1]
    R = qd.shape[1]
    n_steps = n_pages // PP
    pt = page_table.reshape(-1).astype(jnp.int32)
    kpe_t = jnp.swapaxes(cache_kpe, 1, 2)

    def page_spec(rows, cols, p):
        return pl.BlockSpec((1, rows, cols), lambda b, j, pt_ref: (pt_ref[b * n_pages + j * PP + p], 0, 0))

    in_specs = ([pl.BlockSpec((1, R, KV_LORA + ROPE_DIM), lambda b, j, pt_ref: (b, 0, 0))]
                + [page_spec(PAGE_SIZE, KV_LORA, p) for p in range(PP)]
                + [page_spec(ROPE_DIM, PAGE_SIZE, p) for p in range(PP)]
                + [pl.BlockSpec((1, TS, KV_LORA), lambda b, j, pt_ref: (b, 0, 0)),
                   pl.BlockSpec((1, TS, ROPE_DIM), lambda b, j, pt_ref: (b, 0, 0)),
                   pl.BlockSpec((H_B, KV_LORA, V_DIM), lambda b, j, pt_ref: (0, 0, 0))])
    grid_spec = pltpu.PrefetchScalarGridSpec(
        num_scalar_prefetch=1,
        grid=(nb, n_steps),
        in_specs=in_specs,
        out_specs=pl.BlockSpec((1, TS, H_B * V_DIM), lambda b, j, pt_ref: (b, 0, 0)),
        scratch_shapes=[pltpu.VMEM((R, 1), F32), pltpu.VMEM((R, 1), F32), pltpu.VMEM((R, KV_LORA), F32)],
    )
    return pl.pallas_call(
        functools.partial(_decode_kernel, PP=PP, n_steps=n_steps, TS=TS),
        out_shape=jax.ShapeDtypeStruct((nb, TS, H_B * V_DIM), BF16),
        grid_spec=grid_spec,
        compiler_params=_cparams(2),
        name="mla_decode",
    )(pt, qd, *([cache_ckv] * PP), *([kpe_t] * PP), ckv_new, kpe_new, w_uv)


def _rope_table(pos):
    half = ROPE_DIM // 2
    inv = ROPE_BASE ** (-jnp.arange(half, dtype=F32) / half)
    ang = pos.astype(F32)[:, None] * inv[None, :]
    cos, sin = jnp.cos(ang), jnp.sin(ang)
    return jnp.concatenate([cos, cos, -sin, sin], axis=-1)


def _swap_halves(w):
    half = ROPE_DIM // 2
    return jnp.concatenate([w[..., half:], w[..., :half]], axis=-1)


def _prep_rwkv_weights(l, P):
    w_in = P["w_in_a"][l]
    mu = P["mu_a"][l]
    cuts = [0, MIX_A, MIX_A + DECAY_LORA, 2 * MIX_A + DECAY_LORA, 3 * MIX_A + DECAY_LORA,
            3 * MIX_A + DECAY_LORA + A_LORA, C_RWKV, C_A]
    seg = lambda a, i: a[..., cuts[i]:cuts[i + 1]]
    r_w, wl_w, k_w, v_w, al_w, gl_w, qm_w = [seg(w_in, i) for i in range(7)]
    r_m, wl_m, k_m, v_m, al_m, gl_m = [seg(mu, i) for i in range(6)]
    zc = lambda n: jnp.zeros((D_MODEL, n), F32)
    zm = lambda n: jnp.zeros((n,), F32)
    if l > 0:
        vl_w, vl_m = P["w_vres_in"][l - 1], P["mu_vres"][l - 1]
        wv_up, v0 = P["w_vres_up"][l - 1], P["v0"][l - 1]
    else:
        vl_w, vl_m = zc(VRES_LORA), zm(VRES_LORA)
        wv_up, v0 = jnp.zeros((VRES_LORA, MIX_A), F32), zm(MIX_A)
    w_full = jnp.concatenate([r_w, k_w, v_w, wl_w, al_w, gl_w, vl_w, zc(64), qm_w, zc(128)], axis=1)
    mu_full = jnp.concatenate([r_m, k_m, v_m, wl_m, al_m, gl_m, vl_m, zm(64), zm(MEM_WIDTH), zm(128)])
    zr = lambda n: jnp.zeros((n, MIX_A), F32)
    return dict(
        w_in=w_full.astype(BF16),
        mu=mu_full.reshape(1, PA_COLS),
        wd=jnp.concatenate([P["w_decay_up"][l], zr(A_LORA)], axis=0).astype(BF16),
        wa=jnp.concatenate([zr(DECAY_LORA), P["w_a_up"][l]], axis=0).astype(BF16),
        wg=jnp.concatenate([P["w_g_up"][l], zr(256 - GATE_LORA)], axis=0).astype(BF16),
        wv=jnp.concatenate([zr(GATE_LORA), wv_up, zr(256 - GATE_LORA - VRES_LORA)], axis=0).astype(BF16),
        w0=P["w0"][l].reshape(1, MIX_A), a0=P["a0"][l].reshape(1, MIX_A), v0=v0.reshape(1, MIX_A),
        k_k=P["k_k"][l].reshape(1, MIX_A), k_a=P["k_a"][l].reshape(1, MIX_A),
    )


def _prep_mla_weights(P):
    w_kv_a = P["w_kv_a"]
    w_kva = jnp.concatenate([w_kv_a, _swap_halves(w_kv_a[:, KV_LORA:])], axis=1).astype(BF16)
    w_kv_b = P["w_kv_b"]
    w_uk, w_uv = w_kv_b[..., :NOPE_DIM], w_kv_b[..., NOPE_DIM:]
    k_part = jnp.zeros((KV_LORA + 128, H_B, 256), F32)
    k_part = k_part.at[:KV_LORA, :, :NOPE_DIM].set(w_uk)
    eye = jnp.eye(ROPE_DIM, dtype=F32)
    k_part = k_part.at[KV_LORA:KV_LORA + ROPE_DIM, :, NOPE_DIM:NOPE_DIM + ROPE_DIM].set(
        jnp.broadcast_to(eye[:, None, :], (ROPE_DIM, H_B, ROPE_DIM)))
    v_part = jnp.zeros((KV_LORA + 128, H_B, V_DIM), F32).at[:KV_LORA].set(w_uv)
    w_kvx = jnp.concatenate([k_part.reshape(KV_LORA + 128, H_B * 256), v_part.reshape(KV_LORA + 128, H_B * V_DIM)],
                            axis=1).astype(BF16)
    w_q = []
    for j in range(N_B):
        wq = P["w_q_b"][j].reshape(Q_LORA, H_B, NOPE_DIM + ROPE_DIM)
        rope_w = wq[..., NOPE_DIM:]
        w_q.append(jnp.concatenate([wq[..., :NOPE_DIM], rope_w, _swap_halves(rope_w)], axis=-1)
                   .reshape(Q_LORA, H_B * 256).astype(BF16))
    return dict(
        w_kva=w_kva, w_kvx=w_kvx, w_q=w_q,
        w_ukT=jnp.transpose(w_uk, (1, 2, 0)).astype(BF16),
        w_uv=jnp.transpose(w_uv, (1, 0, 2)).astype(BF16),
    )


def _trunk(x, pos, mem_k, mem_v, wkv0, shift0, conv0, past, P, W, *, nseq, T):
    M = nseq * T
    x = x.reshape(M, D_MODEL)
    cs = _rope_table(pos)
    if T < ROW_TILES[0]:
        cs = jnp.tile(cs, (nseq, 1))
    chunk = 64 if T >= 64 else T
    new_wkv, new_shift, new_conv = [], [], []
    v_first = None
    ckv = kpe = ckpe = kvx = None
    for l in range(DEPTH):
        g_mix = P["norm_mix"][l]
        if l < N_A:
            pw = W["rwkv"][l]
            proj = _mm([x], [pw["w_in"]], g=g_mix, name="rwkv_in")
            x_last = x.reshape(nseq, T, D_MODEL)[:, -1]
            new_shift.append(_rmsnorm(x_last, g_mix, name="shift_norm"))
            sp = _mm([shift0[l].astype(F32)], [pw["w_in"]], name="rwkv_in_shift")
            r, lw, k, v, kn, b, g, q_mem = _rwkv_prep(proj, sp, pw, v_first if l > 0 else None, nseq=nseq, T=T)
            if l == 0:
                v_first = v
            h0 = wkv0[l].astype(F32)
            o_tok, h_new = _wkv(r, lw, k, v, kn, b, g, P["r_k"][l], P["lnx_w"][l], P["lnx_b"][l], h0,
                                nseq=nseq, T=T, C=chunk)
            new_wkv.append(h_new)
        else:
            j = l - N_A
            if l == N_A:
                hkv = _mm([x], [W["mla"]["w_kva"]], g=P["norm_kv"], name="kv_a")
                ckv, kpe, ckpe = _latent_post(hkv, P["norm_ckv"], cs, T=T)
                if past is None:
                    kvx = _mm([ckpe], [W["mla"]["w_kvx"]], out_dtype=BF16, name="kv_expand")
            proj = _mm([x], [W["w_in_b"][j]], g=g_mix, name="mla_in")
            qf = _mm([proj], [W["mla"]["w_q"][j]], g=P["norm_q"][j], name="q_b")
            q_mem = proj[:, Q_LORA:].astype(BF16)
            if past is None:
                q = _q_post(qf, cs, T=T, out_dtype=BF16)
                o_tok = _flash_causal(q, kvx, nb=nseq, T=T)
            else:
                q = _q_post(qf, cs, T=T, out_dtype=BF16).reshape(M, H_B, 256)
                q_lat = jnp.stack([_mm([q[:, h, :NOPE_DIM]], [W["mla"]["w_ukT"][h]], out_dtype=BF16, name="q_absorb")
                                   for h in range(H_B)], axis=1)
                qd = jnp.concatenate([q_lat, q[:, :, NOPE_DIM:NOPE_DIM + ROPE_DIM]], axis=-1)
                qd = qd.reshape(nseq, T, H_B, KV_LORA + ROPE_DIM).transpose(0, 2, 1, 3).reshape(nseq, H_B * T, -1)
                o_tok = _decode_attn(past[2], qd, past[0], past[1], ckv.reshape(nseq, T, KV_LORA),
                                     kpe.reshape(nseq, T, ROPE_DIM), W["mla"]["w_uv"]).reshape(M, H_B * V_DIM)
        o_mem = _mem_attn(q_mem, mem_k, mem_v, l, nb=nseq, T=T)
        x = _mm([o_tok, o_mem], [W["w_o_tok"][l], W["w_o_mem"][l]], res=x, name="w_o")
        x, c_new = _ffn(x, P["norm_ffn"][l], W["w_gate"][l], W["w_val"][l], P["conv_w"][l], P["conv_b"][l],
                        W["w_down"][l], conv0[l], nseq=nseq, T=T)
        new_conv.append(c_new)
    y = _rmsnorm(x, P["final_norm"], name="final_norm")
    return (y.reshape(nseq, T, D_MODEL), ckv.reshape(nseq, T, KV_LORA), kpe.reshape(nseq, T, ROPE_DIM),
            jnp.stack(new_wkv), jnp.stack(new_shift), jnp.stack(new_conv))


def kernel(x_prompt, x_sample, cache_ckv, cache_kpe, cache_mem_k, cache_mem_v, state_wkv, state_shift, state_conv, page_table, mem_prompt, norm_mix, norm_ffn, norm_mem, w_mem_kv, w_o, w_in_a, mu_a, w_vres_in, mu_vres, w_decay_up, w0, w_a_up, a0, w_g_up, w_vres_up, v0, k_k, k_a, r_k, lnx_w, lnx_b, norm_kv, w_kv_a, norm_ckv, w_kv_b, w_in_b, norm_q, w_q_b, w_ffn_up, conv_w, conv_b, w_ffn_down, final_norm):
    P = dict(norm_mix=norm_mix, norm_ffn=norm_ffn, w_o=w_o, w_in_a=w_in_a, mu_a=mu_a, w_vres_in=w_vres_in,
             mu_vres=mu_vres, w_decay_up=w_decay_up, w0=w0, w_a_up=w_a_up, a0=a0, w_g_up=w_g_up,
             w_vres_up=w_vres_up, v0=v0, k_k=k_k, k_a=k_a, r_k=r_k, lnx_w=lnx_w, lnx_b=lnx_b,
             norm_kv=norm_kv, w_kv_a=w_kv_a, norm_ckv=norm_ckv, w_kv_b=w_kv_b, w_in_b=w_in_b, norm_q=norm_q,
             w_q_b=w_q_b, conv_w=conv_w, conv_b=conv_b, final_norm=final_norm)
    W = dict(
        rwkv=[_prep_rwkv_weights(l, P) for l in range(N_A)],
        mla=_prep_mla_weights(P),
        w_in_b=[w_in_b[j].astype(BF16) for j in range(N_B)],
        w_o_tok=[w_o[l, :MIX_A].astype(BF16) for l in range(DEPTH)],
        w_o_mem=[w_o[l, MIX_A:].astype(BF16) for l in range(DEPTH)],
        w_gate=[w_ffn_up[l, :, :D_FF].astype(BF16) for l in range(DEPTH)],
        w_val=[w_ffn_up[l, :, D_FF:].astype(BF16) for l in range(DEPTH)],
        w_down=[w_ffn_down[l].astype(BF16) for l in range(DEPTH)],
    )
    dt = x_prompt.dtype
    B, T = x_prompt.shape[:2]
    n_mem = mem_prompt.shape[1]
    mem_k_all, mem_v_all = _mem_kv(mem_prompt, norm_mem, w_mem_kv)
    y_p, ckv_p, kpe_p, wkv_p, shift_p, conv_p = _trunk(
        x_prompt, jnp.arange(T), mem_k_all, mem_v_all,
        jnp.zeros((N_A, B, H_A, HEAD_A, HEAD_A), dt), jnp.zeros((N_A, B, D_MODEL), dt),
        jnp.zeros((DEPTH, B, CONV_W - 1, D_FF), dt), None, P, W, nseq=B, T=T)
    DB, TS = x_sample.shape[:2]
    past_len = page_table.shape[1] * PAGE_SIZE
    to_t = lambda c: jnp.transpose(c, (0, 1, 3, 4, 2)).reshape(DEPTH, DB, MEM_WIDTH, n_mem)
    y_s, ckv_s, kpe_s, wkv_s, shift_s, conv_s = _trunk(
        x_sample, past_len + jnp.arange(TS), to_t(cache_mem_k), to_t(cache_mem_v), state_wkv, state_shift, state_conv,
        (cache_ckv, cache_kpe, page_table), P, W, nseq=DB, T=TS)
    from_t = lambda a: jnp.transpose(a.reshape(DEPTH, B, MEM_HEADS, MEM_HEAD_DIM, n_mem), (0, 1, 4, 2, 3))
    mem_k_out = from_t(mem_k_all)
    mem_v_out = from_t(mem_v_all)
    return (y_p, y_s, ckv_p, kpe_p, mem_k_out, mem_v_out, wkv_p, shift_p, conv_p,
            ckv_s, kpe_s, wkv_s, shift_s, conv_s)
```

```python
import functools
import math

import jax
import jax.numpy as jnp
from jax import lax
from jax.experimental import pallas as pl
from jax.experimental.pallas import tpu as pltpu

F32 = jnp.float32
BF16 = jnp.bfloat16

D_MODEL = 1024
DEPTH = 4
N_A = 2
N_B = 2
MEM_WIDTH = 256
MIX_A = 768
HEAD_A = 64
H_A = 12
DECAY_LORA = 64
A_LORA = 64
VRES_LORA = 32
GATE_LORA = 160
GN_EPS = 64e-5
C_RWKV = 3 * MIX_A + DECAY_LORA + A_LORA + GATE_LORA
C_A = C_RWKV + MEM_WIDTH
MEM_HEADS = 4
MEM_HEAD_DIM = 64
MEM_SCALE = MEM_HEAD_DIM ** -0.5
NOPE_DIM = 128
ROPE_DIM = 64
V_DIM = 128
H_B = 6
Q_LORA = 256
KV_LORA = 256
ROPE_BASE = 10000.0
ATTN_SCALE = (NOPE_DIM + ROPE_DIM) ** -0.5
D_FF = 2816
CONV_W = 3
RMS_EPS = 1e-6
PAGE_SIZE = 128

PA_R, PA_K, PA_V = 0, 768, 1536
PA_LORA = 2304
PA_GATE = 2432
PA_QMEM = 2688
PA_COLS = 3072

VMEM_LIMIT_BYTES = 52 * 1024 * 1024

NN = ((1,), (0,))
NT = ((1,), (1,))
TN = ((0,), (0,))


def _dg(a, b, dims=NN):
    return lax.dot_general(a, b, (dims, ((), ())), preferred_element_type=F32)


def _split(x):
    hi = x.astype(BF16)
    lo = (x - hi.astype(F32)).astype(BF16)
    return hi, lo


def _dot3(a, b, dims=NN):
    ah, al = _split(a)
    bh, bl = _split(b)
    return _dg(ah, bh, dims) + (_dg(ah, bl, dims) + _dg(al, bh, dims))


def _transpose_exact(x, eye_bf16):
    hi = x.astype(BF16)
    r1 = x - hi.astype(F32)
    mid = r1.astype(BF16)
    lo = (r1 - mid.astype(F32)).astype(BF16)
    return (_dg(eye_bf16, hi, NT) + _dg(eye_bf16, mid, NT)) + _dg(eye_bf16, lo, NT)


def _cparams(n_axes):
    return pltpu.CompilerParams(dimension_semantics=("arbitrary",) * n_axes,
                                vmem_limit_bytes=VMEM_LIMIT_BYTES)


def _rms(x, g, eps=RMS_EPS):
    return x * lax.rsqrt(jnp.mean(x * x, axis=-1, keepdims=True) + eps) * g


def _pick_tile(n, candidates):
    for c in candidates:
        if n % c == 0:
            return c
    return n


ROW_TILES = (512, 256, 128, 64, 32, 16, 8)


def _pos_tile(M, T):
    return _pick_tile(T, ROW_TILES) if T >= ROW_TILES[0] else _pick_tile(M, ROW_TILES)


def _mm_kernel(*refs, n_in, has_norm, has_res, f32_in):
    xs = refs[:n_in]
    ws = refs[n_in:2 * n_in]
    k = 2 * n_in
    g_ref = refs[k] if has_norm else None
    k += int(has_norm)
    r_ref = refs[k] if has_res else None
    k += int(has_res)
    o_ref = refs[k]
    scr = list(refs[k + 1:])

    lhs = []
    si = 0
    for idx in range(n_in):
        if f32_in[idx]:
            s_ref = scr[si]
            si += 1

            @pl.when(pl.program_id(1) == 0)
            def _(x_ref=xs[idx], s_ref=s_ref, idx=idx):
                x = x_ref[...]
                if has_norm and idx == 0:
                    x = _rms(x, g_ref[...])
                s_ref[...] = x.astype(BF16)

            lhs.append(s_ref)
        else:
            lhs.append(xs[idx])
    acc = _dg(lhs[0][...], ws[0][...])
    for idx in range(1, n_in):
        acc = acc + _dg(lhs[idx][...], ws[idx][...])
    if has_res:
        acc = acc + r_ref[...]
    o_ref[...] = acc.astype(o_ref.dtype)


def _mm(xs, ws, *, g=None, res=None, out_dtype=F32, x_col_blocks=None, ks=None, name="mm"):
    n_in = len(xs)
    M = xs[0].shape[0]
    N = ws[0].shape[1]
    ks = [w.shape[0] for w in ws]
    x_col_blocks = x_col_blocks or [0] * n_in
    tm = _pick_tile(M, (512, 256, 128, 64, 32, 16, 8))
    tn = _pick_tile(N, (1024, 768, 512, 384, 256, 128))
    f32_in = tuple(x.dtype == F32 for x in xs)
    in_specs = []
    for x, kk, cb in zip(xs, ks, x_col_blocks):
        in_specs.append(pl.BlockSpec((tm, kk), lambda i, j, cb=cb: (i, cb)))
    for w, kk in zip(ws, ks):
        in_specs.append(pl.BlockSpec((kk, tn), lambda i, j: (0, j)))
    args = list(xs) + list(ws)
    if g is not None:
        in_specs.append(pl.BlockSpec((1, ks[0]), lambda i, j: (0, 0)))
        args.append(g.reshape(1, ks[0]).astype(F32))
    if res is not None:
        in_specs.append(pl.BlockSpec((tm, tn), lambda i, j: (i, j)))
        args.append(res)
    scratch = [pltpu.VMEM((tm, kk), BF16) for kk, f in zip(ks, f32_in) if f]
    return pl.pallas_call(
        functools.partial(_mm_kernel, n_in=n_in, has_norm=g is not None, has_res=res is not None,
                          f32_in=f32_in),
        out_shape=jax.ShapeDtypeStruct((M, N), out_dtype),
        grid=(M // tm, N // tn),
        in_specs=in_specs,
        out_specs=pl.BlockSpec((tm, tn), lambda i, j: (i, j)),
        scratch_shapes=scratch,
        compiler_params=_cparams(2),
        name=name,
    )(*args)


def _rmsnorm_kernel(x_ref, g_ref, o_ref):
    o_ref[...] = _rms(x_ref[...], g_ref[...])


def _rmsnorm(x, g, name="rmsnorm"):
    M, D = x.shape
    tm = _pick_tile(M, (512, 256, 128, 64, 32, 16, 8))
    return pl.pallas_call(
        _rmsnorm_kernel,
        out_shape=jax.ShapeDtypeStruct((M, D), F32),
        grid=(M // tm,),
        in_specs=[pl.BlockSpec((tm, D), lambda i: (i, 0)), pl.BlockSpec((1, D), lambda i: (0, 0))],
        out_specs=pl.BlockSpec((tm, D), lambda i: (i, 0)),
        compiler_params=_cparams(1),
        name=name,
    )(x, g.reshape(1, D).astype(F32))


def _mem_kv_kernel(x_ref, g_ref, wt_ref, k_ref, v_ref):
    kvt = _dg(wt_ref[0], _rms(x_ref[0], g_ref[0]).astype(BF16), NT)
    k_ref[0, 0] = kvt[:MEM_WIDTH]
    v_ref[0, 0] = kvt[MEM_WIDTH:]


def _mem_kv(mem, norm_mem, w_mem_kv):
    B, n_mem, _ = mem.shape
    out = jax.ShapeDtypeStruct((DEPTH, B, MEM_WIDTH, n_mem), F32)
    wt = jnp.swapaxes(w_mem_kv, 1, 2).astype(BF16)
    return pl.pallas_call(
        _mem_kv_kernel,
        out_shape=(out, out),
        grid=(DEPTH, B),
        in_specs=[pl.BlockSpec((1, n_mem, D_MODEL), lambda l, b: (b, 0, 0)),
                  pl.BlockSpec((1, 1, D_MODEL), lambda l, b: (l, 0, 0)),
                  pl.BlockSpec((1, 2 * MEM_WIDTH, D_MODEL), lambda l, b: (l, 0, 0))],
        out_specs=(pl.BlockSpec((1, 1, MEM_WIDTH, n_mem), lambda l, b: (l, b, 0, 0)),
                   pl.BlockSpec((1, 1, MEM_WIDTH, n_mem), lambda l, b: (l, b, 0, 0))),
        compiler_params=_cparams(2),
        name="mem_kv",
    )(mem, norm_mem.reshape(DEPTH, 1, D_MODEL).astype(F32), wt)


def _segsum64(x):
    r = lax.broadcasted_iota(jnp.int32, (128, 128), 0) >> 6
    c = lax.broadcasted_iota(jnp.int32, (128, 128), 1) >> 6
    ones = jnp.where(r == c, 1.0, 0.0).astype(BF16)
    outs = []
    for j in range(x.shape[1] // 128):
        hi, lo = _split(x[:, j * 128:(j + 1) * 128])
        outs.append(_dg(hi, ones) + _dg(lo, ones))
    return jnp.concatenate(outs, axis=1)


def _softplus(x):
    return jnp.maximum(x, 0.0) + jnp.log1p(jnp.exp(-jnp.abs(x)))


def _prep_kernel(*refs, has_vres, blocks_per_seq):
    if blocks_per_seq:
        x_ref, gmix_ref, win_ref = refs[:3]
        refs = refs[2:]
    (cur_ref, prev_ref, mu_ref, wd_ref, wa_ref, wg_ref, wv_ref, w0_ref, a0_ref, v0_ref, kk_ref, ka_ref) = refs[:12]
    k = 12
    vf_ref = refs[k] if has_vres else None
    k += int(has_vres)
    r_o, lw_o, k_o, v_o, kn_o, b_o, g_o, qm_o = refs[k:k + 8]

    if blocks_per_seq:
        cur_ref, carry_scr = refs[k + 8], refs[k + 9]
        xn = _rms(x_ref[...], gmix_ref[...]).astype(BF16)
        for lo in range(0, PA_COLS, 512):
            cur_ref[:, lo:lo + 512] = _dg(xn, win_ref[:, lo:lo + 512])
    tm = cur_ref.shape[0]

    if blocks_per_seq:

        @pl.when(pl.program_id(0) % blocks_per_seq == 0)
        def _():
            carry_scr[...] = prev_ref[0]

        first_row = lax.broadcasted_iota(jnp.int32, (tm, 1), 0) == 0

    def mix(lo, hi):
        c = cur_ref[:, lo:hi]
        if blocks_per_seq:
            p = jnp.where(first_row, carry_scr[:, lo:hi], pltpu.roll(c, 1, 0))
        else:
            p = prev_ref[:, lo:hi]
        return c + (p - c) * mu_ref[:, lo:hi]

    lora = mix(PA_LORA, PA_LORA + 128)
    zw = _dg(jnp.tanh(lora).astype(BF16), wd_ref[...]) + w0_ref[...]
    w_log = -_softplus(-zw) - 0.5
    lw_o[...] = -jnp.exp(w_log)
    a = jax.nn.sigmoid(_dg(lora.astype(BF16), wa_ref[...]) + a0_ref[...])
    gin = mix(PA_GATE, PA_GATE + 256)
    g_o[...] = _dg(jax.nn.sigmoid(gin).astype(BF16), wg_ref[...])
    r_o[...] = mix(PA_R, PA_R + MIX_A)
    kx = mix(PA_K, PA_K + MIX_A)
    v = mix(PA_V, PA_V + MIX_A)
    if has_vres:
        sv = jax.nn.sigmoid(_dg(gin.astype(BF16), wv_ref[...]) + v0_ref[...])
        v = v + (vf_ref[...] - v) * sv
    v_o[...] = v
    kk = kx * kk_ref[...]
    kk = kk * lax.rsqrt(jnp.maximum(_segsum64(kk * kk), 1e-24))
    kn_o[...] = kk
    b_o[...] = kk * a
    k_o[...] = kx * (1.0 + (a - 1.0) * ka_ref[...])
    qm_o[...] = cur_ref[:, PA_QMEM:PA_QMEM + MEM_WIDTH].astype(BF16)
    if blocks_per_seq:
        carry_scr[...] = cur_ref[tm - 1:tm, :]


def _rwkv_prep(x, sp, pw, v_first, g_mix, *, nseq, T):
    M = x.shape[0]
    has_vres = v_first is not None
    row = lambda w: pl.BlockSpec((tm, w), lambda i: (i, 0))
    full = lambda a: pl.BlockSpec(a.shape, lambda i: (0,) * a.ndim)
    if T >= 256:
        tm = 256
        blocks_per_seq = T // tm
        g2 = g_mix.reshape(1, D_MODEL).astype(F32)
        lead_args = [x, g2, pw["w_in"], sp.reshape(nseq, 1, PA_COLS)]
        lead_specs = [row(D_MODEL), full(g2),
                      pl.BlockSpec(pw["w_in"].shape, lambda i: (0, 0), pipeline_mode=pl.Buffered(1)),
                      pl.BlockSpec((1, 1, PA_COLS), lambda i: (i // blocks_per_seq, 0, 0))]
        scratch = [pltpu.VMEM((tm, PA_COLS), F32), pltpu.VMEM((1, PA_COLS), F32)]
    else:
        tm = _pick_tile(M, (256, 128, 64, 32, 16, 8))
        blocks_per_seq = 0
        cur = _mm([x], [pw["w_in"]], g=g_mix, name="rwkv_in")
        prev = jnp.concatenate([sp[:, None, :], cur.reshape(nseq, T, PA_COLS)[:, :-1]], axis=1).reshape(M, PA_COLS)
        lead_args = [cur, prev]
        lead_specs = [row(PA_COLS), row(PA_COLS)]
        scratch = []
    consts = [pw["mu"], pw["wd"], pw["wa"], pw["wg"], pw["wv"], pw["w0"], pw["a0"], pw["v0"], pw["k_k"], pw["k_a"]]
    args = lead_args + consts
    in_specs = lead_specs + [full(a) for a in consts]
    if has_vres:
        args.append(v_first)
        in_specs.append(row(MIX_A))
    outs = [jax.ShapeDtypeStruct((M, MIX_A), F32)] * 7 + [jax.ShapeDtypeStruct((M, MEM_WIDTH), BF16)]
    out_specs = [row(MIX_A)] * 7 + [row(MEM_WIDTH)]
    return pl.pallas_call(
        functools.partial(_prep_kernel, has_vres=has_vres, blocks_per_seq=blocks_per_seq),
        out_shape=outs,
        grid=(M // tm,),
        in_specs=in_specs,
        out_specs=out_specs,
        scratch_shapes=scratch,
        compiler_params=_cparams(1),
        name="rwkv_prep",
    )(*args)


def _wkv_kernel(r_ref, lw_ref, k_ref, v_ref, kn_ref, b_ref, g_ref, rk_ref, lnw_ref, lnb_ref, h0_ref,
                o_ref, hT_ref, h_scr, *, C, H, N, nchunks, xpose):
    c_idx = pl.program_id(1)

    eye_bf = jnp.where(lax.broadcasted_iota(jnp.int32, (N, N), 0) == lax.broadcasted_iota(jnp.int32, (N, N), 1),
                       1.0, 0.0).astype(BF16)

    @pl.when(c_idx == 0)
    def _():
        if xpose:
            for h in range(H):
                h_scr[h] = _transpose_exact(h0_ref[0, h], eye_bf)
        else:
            h_scr[...] = h0_ref[0]

    lw = lw_ref[...]
    ti = lax.broadcasted_iota(jnp.int32, (C, C), 0)
    tj = lax.broadcasted_iota(jnp.int32, (C, C), 1)
    ltri = jnp.where(tj <= ti, 1.0, 0.0).astype(BF16)
    lw_hi, lw_lo = _split(lw)
    cum = _dg(ltri, lw_hi) + _dg(ltri, lw_lo)
    cum_last = jnp.sum(lw, axis=0, keepdims=True)
    e_neg = jnp.exp(-cum)
    r_all = r_ref[...]
    k_all = k_ref[...]
    v_all = v_ref[...]
    kt_all = kn_ref[...] * jnp.exp(cum - lw)
    rt_all = r_all * jnp.exp(cum)
    kh_all = k_all * e_neg
    bh_all = b_ref[...] * e_neg
    e_rem = jnp.exp(cum_last - cum)
    kb_all = k_all * e_rem
    bb_all = b_ref[...] * e_rem
    p_last = jnp.exp(cum_last)
    rkr = r_all * k_all * rk_ref[...]

    strict = tj < ti
    incl = tj <= ti
    ni = lax.broadcasted_iota(jnp.int32, (N, N), 0)
    nj = lax.broadcasted_iota(jnp.int32, (N, N), 1)
    eye_n = ni == nj
    eye_c = jnp.where(ti == tj, 1.0, 0.0)

    hs = range(H)
    sls = [slice(h * N, (h + 1) * N) for h in hs]
    cut = lambda a: [a[:, sl].astype(BF16) for sl in sls]
    kt, rt, kh, bh, kb, bb, vv = (cut(a) for a in (kt_all, rt_all, kh_all, bh_all, kb_all, bb_all, v_all))
    a_k = [jnp.where(strict, _dg(kt[h], kh[h], NT), 0.0).astype(BF16) for h in hs]
    a_b = [jnp.where(strict, _dg(kt[h], bh[h], NT), 0.0) for h in hs]
    q_k = [jnp.where(incl, _dg(rt[h], kh[h], NT), 0.0).astype(BF16) for h in hs]
    q_b = [jnp.where(incl, _dg(rt[h], bh[h], NT), 0.0).astype(BF16) for h in hs]
    tinv = [eye_c - jnp.where(((ti >> 1) == (tj >> 1)), a_b[h], 0.0) for h in hs]
    blk = 2
    while blk < C:
        sh = blk.bit_length() - 1
        lvl = ((ti >> (sh + 1)) == (tj >> (sh + 1))) & (((ti >> sh) & 1) == 1) & (((tj >> sh) & 1) == 0)
        t16 = [tinv[h].astype(BF16) for h in hs]
        mt = [_dg(jnp.where(lvl, a_b[h], 0.0).astype(BF16), t16[h]).astype(BF16) for h in hs]
        tinv = [tinv[h] - _dg(t16[h], mt[h]) for h in hs]
        blk *= 2
    t16 = [tinv[h].astype(BF16) for h in hs]
    akv = [_dg(a_k[h], vv[h]).astype(BF16) for h in hs]
    x = [_dg(t16[h], jnp.concatenate([kt[h], akv[h]], axis=1)).astype(BF16) for h in hs]
    qw = [_dg(q_b[h], x[h]) for h in hs]
    r2 = [(rt_all[:, sls[h]] - qw[h][:, :N]).astype(BF16) for h in hs]
    y0 = [_dg(q_k[h], vv[h]) - qw[h][:, N:] for h in hs]
    bw = [_dg(bb[h], x[h], TN) for h in hs]
    gmat = [(jnp.where(eye_n, jnp.broadcast_to(p_last[:, sls[h]], (N, N)), 0.0) - bw[h][:, :N]).astype(BF16) for h in hs]
    h_add = [_dg(kb[h], vv[h], TN) - bw[h][:, N:] for h in hs]
    h0 = [h_scr[h].astype(BF16) for h in hs]
    y = [_dg(r2[h], h0[h]) + y0[h] for h in hs]
    for h in hs:
        h_scr[h] = _dg(gmat[h], h0[h]) + h_add[h]
    for h in hs:
        sl = sls[h]
        mean = jnp.mean(y[h], axis=-1, keepdims=True)
        yc = y[h] - mean
        var = jnp.mean(yc * yc, axis=-1, keepdims=True)
        yn = yc * lax.rsqrt(var + GN_EPS)
        yn = yn * lnw_ref[:, sl] + lnb_ref[:, sl]
        bonus = jnp.sum(rkr[:, sl], axis=-1, keepdims=True) * v_all[:, sl]
        o_ref[:, sl] = ((yn + bonus) * g_ref[:, sl]).astype(o_ref.dtype)

    @pl.when(c_idx == nchunks - 1)
    def _():
        if xpose:
            for h in range(H):
                hT_ref[0, h] = _transpose_exact(h_scr[h], eye_bf)
        else:
            hT_ref[0] = h_scr[...]


def _wkv(r, lw, k, v, kn, b, g, r_k, lnx_w, lnx_b, h0, *, nseq, T, C):
    M = r.shape[0]
    nchunks = T // C
    HN = MIX_A
    xpose = nchunks > 1
    h_in = h0 if xpose else jnp.swapaxes(h0, -1, -2)
    row = pl.BlockSpec((C, HN), lambda s, c: (s * nchunks + c, 0))
    vec = pl.BlockSpec((1, HN), lambda s, c: (0, 0))
    st = pl.BlockSpec((1, H_A, HEAD_A, HEAD_A), lambda s, c: (s, 0, 0, 0))
    o, h_new = pl.pallas_call(
        functools.partial(_wkv_kernel, C=C, H=H_A, N=HEAD_A, nchunks=nchunks, xpose=xpose),
        out_shape=(jax.ShapeDtypeStruct((M, HN), BF16),
                   jax.ShapeDtypeStruct((nseq, H_A, HEAD_A, HEAD_A), F32)),
        grid=(nseq, nchunks),
        in_specs=[row] * 7 + [vec] * 3 + [st],
        out_specs=(row, st),
        scratch_shapes=[pltpu.VMEM((H_A, HEAD_A, HEAD_A), F32)],
        compiler_params=_cparams(2),
        name="wkv7",
    )(r, lw, k, v, kn, b, g, r_k.reshape(1, HN), lnx_w.reshape(1, HN), lnx_b.reshape(1, HN), h_in)
    return o, (h_new if xpose else jnp.swapaxes(h_new, -1, -2))


def _mem_attn_kernel(q_ref, k_ref, v_ref, o_ref, *, nbb, tq):
    c = MEM_SCALE * math.log2(math.e)
    lane_head = lax.broadcasted_iota(jnp.int32, (1, MEM_WIDTH), 1) >> 6
    items = [(bb, h) for bb in range(nbb) for h in range(MEM_HEADS)]
    q_all = q_ref[...].astype(F32)
    qs = [q_all[bb * tq:(bb + 1) * tq, :] for bb in range(nbb)]
    kts = [k_ref[0, bb].astype(BF16) for bb in range(nbb)]
    vts = [v_ref[0, bb].astype(BF16) for bb in range(nbb)]
    ss = [_dg(jnp.where(lane_head == h, qs[bb], 0.0).astype(BF16), kts[bb]) for bb, h in items]
    ms = [jnp.max(s, axis=-1, keepdims=True) for s in ss]
    ps = [jnp.exp2((s - m) * c) for s, m in zip(ss, ms)]
    inv_l = [1.0 / jnp.sum(p, axis=-1, keepdims=True) for p in ps]
    pv = [_dg(p.astype(BF16), vts[bb], NT) for p, (bb, h) in zip(ps, items)]
    outs = []
    for bb in range(nbb):
        o = jnp.zeros((tq, MEM_WIDTH), F32)
        for h in range(MEM_HEADS):
            idx = bb * MEM_HEADS + h
            o = jnp.where(lane_head == h, pv[idx] * inv_l[idx], o)
        outs.append(o)
    o_ref[...] = (outs[0] if nbb == 1 else jnp.concatenate(outs, axis=0)).astype(o_ref.dtype)


def _mem_attn(q, mem_k, mem_v, layer, *, nb, T):
    M = q.shape[0]
    if T >= ROW_TILES[0]:
        tq, nbb = ROW_TILES[0], 1
    else:
        tq, nbb = T, _pick_tile(nb, (16, 8, 4, 2, 1))
    nq = T // tq
    n_mem = mem_k.shape[3]
    return pl.pallas_call(
        functools.partial(_mem_attn_kernel, nbb=nbb, tq=tq),
        out_shape=jax.ShapeDtypeStruct((M, MEM_WIDTH), BF16),
        grid=(nb // nbb, nq),
        in_specs=[pl.BlockSpec((nbb * tq, MEM_WIDTH), lambda b, i: (b * nq + i, 0)),
                  pl.BlockSpec((1, nbb, MEM_WIDTH, n_mem), lambda b, i: (layer, b, 0, 0)),
                  pl.BlockSpec((1, nbb, MEM_WIDTH, n_mem), lambda b, i: (layer, b, 0, 0))],
        out_specs=pl.BlockSpec((nbb * tq, MEM_WIDTH), lambda b, i: (b * nq + i, 0)),
        compiler_params=_cparams(2),
        name="mem_attn",
    )(q, mem_k, mem_v)


def _ffn_kernel(*refs, tm, T, long_mode, col_chunks):
    if long_mode:
        (x_ref, g_ref, wg_ref, wv_ref, cw_ref, cb_ref, wd_ref, cp_ref, o_ref, tail_ref, xn_scr, h_scr, carry_scr) = refs
    else:
        (x_ref, g_ref, wg_ref, wv_ref, cw_ref, cb_ref, wd_ref, cp_ref, o_ref, gate_ref, xn_scr, h_scr) = refs
    xn_scr[...] = _rms(x_ref[...], g_ref[...]).astype(BF16)
    xn = xn_scr[...]
    row = lax.broadcasted_iota(jnp.int32, (tm, 1), 0)
    if long_mode:
        nblk = T // tm

        @pl.when(pl.program_id(0) % nblk == 0)
        def _():
            carry_scr[...] = cp_ref[0]
    else:
        t_in_seq = row % T
        ncp = cp_ref.shape[0]
        rr = lax.broadcasted_iota(jnp.int32, (tm, ncp), 0)
        cc = lax.broadcasted_iota(jnp.int32, (tm, ncp), 1)
        same_seq = (rr // T) == (cc >> 1)
        sel1 = jnp.where(same_seq & ((rr % T) + 1 == (cc & 1)), 1.0, 0.0).astype(BF16)
        sel2 = jnp.where(same_seq & ((rr % T) == (cc & 1)), 1.0, 0.0).astype(BF16)
        sel = jnp.concatenate([sel1, sel2], axis=0)
    for (lo, w) in col_chunks:
        cs = slice(lo, lo + w)
        gate = _dg(xn, wg_ref[:, cs])
        val = _dg(xn, wv_ref[:, cs])
        r1 = pltpu.roll(gate, 1, 0)
        r2 = pltpu.roll(gate, 2, 0)
        if long_mode:
            c0 = carry_scr[6:7, cs]
            c1 = carry_scr[7:8, cs]
            p1 = jnp.where(row == 0, c1, r1)
            p2 = jnp.where(row == 0, c0, jnp.where(row == 1, c1, r2))
            tail = gate[tm - 8:tm, :]
            carry_scr[:, cs] = tail
            tail_ref[0, :, cs] = tail
        else:
            cpc = cp_ref[:, cs]
            hi = cpc.astype(BF16)
            r_hi = cpc - hi.astype(F32)
            mid = r_hi.astype(BF16)
            lo = (r_hi - mid.astype(F32)).astype(BF16)
            edge = (_dg(sel, hi) + _dg(sel, mid)) + _dg(sel, lo)
            p1 = jnp.where(t_in_seq >= 1, r1, 0.0) + edge[:tm]
            p2 = jnp.where(t_in_seq >= 2, r2, 0.0) + edge[tm:]
            gate_ref[:, cs] = gate
        conv = cb_ref[:, cs] + p2 * cw_ref[0:1, cs]
        conv = conv + p1 * cw_ref[1:2, cs]
        conv = conv + gate * cw_ref[2:3, cs]
        h_scr[:, cs] = (conv * jax.nn.sigmoid(conv) * val).astype(BF16)
    o_ref[...] = x_ref[...] + _dg(h_scr[...], wd_ref[...])


def _ffn(x, g, w_gate, w_val, conv_w, conv_b, w_down, conv_prev, *, nseq, T):
    M = x.shape[0]
    long_mode = T >= 512
    tm = 512 if long_mode else 256
    assert (T % tm == 0) if long_mode else (tm % T == 0 and M % tm == 0)
    col_chunks = tuple((lo, min(256, D_FF - lo)) for lo in range(0, D_FF, 256))
    const = lambda a: pl.BlockSpec(a.shape, lambda i: (0,) * a.ndim, pipeline_mode=pl.Buffered(1))
    small = lambda a: pl.BlockSpec(a.shape, lambda i: (0,) * a.ndim)
    g2 = g.reshape(1, D_MODEL).astype(F32)
    cb2 = conv_b.reshape(1, D_FF).astype(F32)
    common_specs = [pl.BlockSpec((tm, D_MODEL), lambda i: (i, 0)), small(g2), const(w_gate), const(w_val),
                    small(conv_w), small(cb2), const(w_down)]
    common_args = [x, g2, w_gate, w_val, conv_w, cb2, w_down]
    kern = functools.partial(_ffn_kernel, tm=tm, T=T, long_mode=long_mode, col_chunks=col_chunks)
    if long_mode:
        nblk = T // tm
        cp = jnp.concatenate([jnp.zeros((nseq, 6, D_FF), F32), conv_prev.astype(F32)], axis=1)
        out, tails = pl.pallas_call(
            kern,
            out_shape=(jax.ShapeDtypeStruct((M, D_MODEL), F32), jax.ShapeDtypeStruct((M // tm, 8, D_FF), F32)),
            grid=(M // tm,),
            in_specs=common_specs + [pl.BlockSpec((1, 8, D_FF), lambda i: (i // nblk, 0, 0))],
            out_specs=(pl.BlockSpec((tm, D_MODEL), lambda i: (i, 0)),
                       pl.BlockSpec((1, 8, D_FF), lambda i: (i, 0, 0))),
            scratch_shapes=[pltpu.VMEM((tm, D_MODEL), BF16), pltpu.VMEM((tm, D_FF), BF16), pltpu.VMEM((8, D_FF), F32)],
            compiler_params=_cparams(1),
            name="convglu_long",
        )(*common_args, cp)
        new_conv = tails.reshape(nseq, nblk, 8, D_FF)[:, -1, 6:8, :]
        return out, new_conv
    ncp = (CONV_W - 1) * (tm // T)
    out, gate = pl.pallas_call(
        kern,
        out_shape=(jax.ShapeDtypeStruct((M, D_MODEL), F32), jax.ShapeDtypeStruct((M, D_FF), F32)),
        grid=(M // tm,),
        in_specs=common_specs + [pl.BlockSpec((ncp, D_FF), lambda i: (i, 0))],
        out_specs=(pl.BlockSpec((tm, D_MODEL), lambda i: (i, 0)), pl.BlockSpec((tm, D_FF), lambda i: (i, 0))),
        scratch_shapes=[pltpu.VMEM((tm, D_MODEL), BF16), pltpu.VMEM((tm, D_FF), BF16)],
        compiler_params=_cparams(1),
        name="convglu_short",
    )(*common_args, conv_prev.astype(F32).reshape(nseq * (CONV_W - 1), D_FF))
    new_conv = gate.reshape(nseq, T, D_FF)[:, T - 2:, :]
    return out, new_conv


def _rope128(x128, cs):
    a = x128 * cs
    s = a + pltpu.roll(a, 64, 1)
    lane = lax.broadcasted_iota(jnp.int32, a.shape, 1)
    return jnp.where(lane < 64, s, 0.0)


def _latent_kernel(h_ref, g_ref, cs_ref, ckv_ref, kpe_ref, ckpe_ref):
    ckv = _rms(h_ref[:, :KV_LORA], g_ref[...])
    rp = _rope128(h_ref[:, KV_LORA:KV_LORA + 128], cs_ref[...])
    ckv_ref[...] = ckv
    kpe_ref[...] = rp[:, :ROPE_DIM]
    ckpe_ref[:, :KV_LORA] = ckv.astype(BF16)
    ckpe_ref[:, KV_LORA:] = rp.astype(BF16)


def _latent_post(h, norm_ckv, cs, *, T):
    M = h.shape[0]
    tm = _pos_tile(M, T)
    ncs = cs.shape[0] // tm
    return pl.pallas_call(
        _latent_kernel,
        out_shape=(jax.ShapeDtypeStruct((M, KV_LORA), F32), jax.ShapeDtypeStruct((M, ROPE_DIM), F32),
                   jax.ShapeDtypeStruct((M, KV_LORA + 128), BF16)),
        grid=(M // tm,),
        in_specs=[pl.BlockSpec((tm, KV_LORA + 128), lambda i: (i, 0)),
                  pl.BlockSpec((1, KV_LORA), lambda i: (0, 0)),
                  pl.BlockSpec((tm, 128), lambda i: (i % ncs, 0))],
        out_specs=(pl.BlockSpec((tm, KV_LORA), lambda i: (i, 0)), pl.BlockSpec((tm, ROPE_DIM), lambda i: (i, 0)),
                   pl.BlockSpec((tm, KV_LORA + 128), lambda i: (i, 0))),
        compiler_params=_cparams(1),
        name="latent_post",
    )(h, norm_ckv.reshape(1, KV_LORA).astype(F32), cs)


def _qpost_kernel(q_ref, cs_ref, o_ref):
    cs = cs_ref[...]
    for h in range(H_B):
        base = h * 256
        o_ref[:, base:base + 128] = q_ref[:, base:base + 128].astype(o_ref.dtype)
        o_ref[:, base + 128:base + 256] = _rope128(q_ref[:, base + 128:base + 256], cs).astype(o_ref.dtype)


def _q_post(q, cs, *, T, out_dtype):
    M = q.shape[0]
    tm = _pos_tile(M, T)
    ncs = cs.shape[0] // tm
    return pl.pallas_call(
        _qpost_kernel,
        out_shape=jax.ShapeDtypeStruct((M, H_B * 256), out_dtype),
        grid=(M // tm,),
        in_specs=[pl.BlockSpec((tm, H_B * 256), lambda i: (i, 0)),
                  pl.BlockSpec((tm, 128), lambda i: (i % ncs, 0))],
        out_specs=pl.BlockSpec((tm, H_B * 256), lambda i: (i, 0)),
        compiler_params=_cparams(1),
        name="q_post",
    )(q, cs)


def _flash_kernel(q_ref, k_ref, v_ref, o_ref, m_scr, l_scr, acc_scr, *, tq, tk):
    i = pl.program_id(1)
    j = pl.program_id(2)

    @pl.when(j == 0)
    def _():
        m_scr[...] = jnp.full(m_scr.shape, -jnp.inf, F32)
        l_scr[...] = jnp.zeros(l_scr.shape, F32)
        acc_scr[...] = jnp.zeros(acc_scr.shape, F32)

    hs = range(H_B)

    def block(masked):
        c = ATTN_SCALE * math.log2(math.e)
        ss = [_dg(q_ref[:, h * 256:(h + 1) * 256], k_ref[:, h * 256:(h + 1) * 256], NT) for h in hs]
        if masked:
            keep = lax.broadcasted_iota(jnp.int32, (tq, tk), 1) <= lax.broadcasted_iota(jnp.int32, (tq, tk), 0)
            ss = [jnp.where(keep, s, -jnp.inf) for s in ss]
        m_prev = [m_scr[h] for h in hs]
        m_new = [jnp.maximum(m_prev[h], jnp.max(ss[h], axis=-1, keepdims=True)) for h in hs]
        alpha = [jnp.exp2((m_prev[h] - m_new[h]) * c) for h in hs]
        ps = [jnp.exp2((ss[h] - m_new[h]) * c) for h in hs]
        pv = [_dg(ps[h].astype(BF16), v_ref[:, h * V_DIM:(h + 1) * V_DIM]) for h in hs]
        for h in hs:
            l_scr[h] = alpha[h] * l_scr[h] + jnp.sum(ps[h], axis=-1, keepdims=True)
            acc_scr[:, h * V_DIM:(h + 1) * V_DIM] = alpha[h] * acc_scr[:, h * V_DIM:(h + 1) * V_DIM] + pv[h]
            m_scr[h] = m_new[h]

    @pl.when(j < i)
    def _():
        block(False)

    @pl.when(j == i)
    def _():
        block(True)
        for h in hs:
            sl = slice(h * V_DIM, (h + 1) * V_DIM)
            o_ref[:, sl] = (acc_scr[:, sl] / l_scr[h]).astype(o_ref.dtype)


def _flash_causal(q, kv, *, nb, T):
    M = q.shape[0]
    tq = tk = 512
    nq = T // tq
    kw = H_B * 256
    vw = H_B * V_DIM
    return pl.pallas_call(
        functools.partial(_flash_kernel, tq=tq, tk=tk),
        out_shape=jax.ShapeDtypeStruct((M, vw), BF16),
        grid=(nb, nq, nq),
        in_specs=[pl.BlockSpec((tq, kw), lambda b, i, j: (b * nq + i, 0)),
                  pl.BlockSpec((tk, kw), lambda b, i, j: (b * nq + jnp.minimum(i, j), 0)),
                  pl.BlockSpec((tk, vw), lambda b, i, j: (b * nq + jnp.minimum(i, j), kw // vw))],
        out_specs=pl.BlockSpec((tq, vw), lambda b, i, j: (b * nq + i, 0)),
        scratch_shapes=[pltpu.VMEM((H_B, tq, 1), F32), pltpu.VMEM((H_B, tq, 1), F32), pltpu.VMEM((tq, vw), F32)],
        compiler_params=_cparams(3),
        name="mla_causal",
    )(q, kv, kv)


def _decode_kernel(pt_ref, q_ref, ckn_ref, kpn_ref, wuv_ref, ck_hbm, kp_hbm, o_ref,
                   ck_buf, kp_buf, sem, m_scr, l_scr, acc_scr, *, PP, n_steps, n_total, TS):
    j = pl.program_id(1)
    n = pl.program_id(0) * n_steps + j
    slot = lax.rem(n, 2)
    R = q_ref.shape[1]

    def group_copies(step, slot_, for_wait):
        cps = []
        for p in range(PP):
            page = 0 if for_wait else pt_ref[step * PP + p]
            cps.append(pltpu.make_async_copy(ck_hbm.at[page], ck_buf.at[slot_, p], sem.at[slot_, 0]))
            cps.append(pltpu.make_async_copy(kp_hbm.at[page], kp_buf.at[slot_, p], sem.at[slot_, 1]))
        return cps

    @pl.when(n == 0)
    def _():
        for cp in group_copies(0, 0, False):
            cp.start()

    nxt = jnp.minimum(n + 1, n_total - 1)
    for cp in group_copies(nxt, 1 - slot, False):
        cp.start()
    for cp in group_copies(n, slot, True):
        cp.wait()

    @pl.when(j == 0)
    def _():
        m_scr[...] = jnp.full(m_scr.shape, -jnp.inf, F32)
        l_scr[...] = jnp.zeros(l_scr.shape, F32)
        acc_scr[...] = jnp.zeros(acc_scr.shape, F32)

    q_lat = q_ref[0, :, :KV_LORA]
    q_pe = q_ref[0, :, KV_LORA:]
    c = ATTN_SCALE * math.log2(math.e)

    cks = [ck_buf[slot, p].astype(BF16) for p in range(PP)]
    kps = [kp_buf[slot, p].astype(BF16) for p in range(PP)]
    ss = [_dg(q_lat, cks[p], NT) + _dg(q_pe, kps[p]) for p in range(PP)]
    pad = PAGE_SIZE - TS
    ckn = jnp.concatenate([ckn_ref[0], jnp.zeros((pad, KV_LORA), F32)], axis=0).astype(BF16)
    kpn = jnp.concatenate([kpn_ref[0], jnp.zeros((pad, ROPE_DIM), F32)], axis=0).astype(BF16)
    t_q = lax.broadcasted_iota(jnp.int32, (R, PAGE_SIZE), 0) % TS
    col = lax.broadcasted_iota(jnp.int32, (R, PAGE_SIZE), 1)
    last_shift = jnp.where(j == n_steps - 1, 0, -PAGE_SIZE)
    s_new = jnp.where(col <= t_q + last_shift, _dg(q_lat, ckn, NT) + _dg(q_pe, kpn, NT), -jnp.inf)
    ss.append(s_new)
    cks.append(ckn)

    tile_max = ss[0]
    for s in ss[1:]:
        tile_max = jnp.maximum(tile_max, s)
    m_prev = m_scr[...]
    m_new = jnp.maximum(m_prev, jnp.max(tile_max, axis=-1, keepdims=True))
    alpha = jnp.exp2((m_prev - m_new) * c)
    ps = [jnp.exp2((s - m_new) * c) for s in ss]
    p_sum = ps[0]
    for p in ps[1:]:
        p_sum = p_sum + p
    acc = alpha * acc_scr[...]
    for p, vv in zip(ps, cks):
        acc = acc + _dg(p.astype(BF16), vv)
    m_scr[...] = m_new
    l_scr[...] = alpha * l_scr[...] + jnp.sum(p_sum, axis=-1, keepdims=True)
    acc_scr[...] = acc

    @pl.when(j == n_steps - 1)
    def _():
        o_lat = acc_scr[...] / l_scr[...]
        for h in range(H_B):
            o_h = o_lat[h * TS:(h + 1) * TS, :].astype(BF16)
            o_ref[0, :, h * V_DIM:(h + 1) * V_DIM] = _dg(o_h, wuv_ref[h]).astype(o_ref.dtype)

    @pl.when(n == n_total - 1)
    def _():
        for cp in group_copies(n, 1 - slot, True):
            cp.wait()


def _decode_attn(page_table, qd, cache_ckv, cache_kpe, ckv_new, kpe_new, w_uv, *, PP=32):
    nb, n_pages = page_table.shape
    TS = ckv_new.shape[1]
    R = qd.shape[1]
    n_steps = n_pages // PP
    pt = page_table.reshape(-1).astype(jnp.int32)
    kpe_t = jnp.swapaxes(cache_kpe, 1, 2)
    grid_spec = pltpu.PrefetchScalarGridSpec(
        num_scalar_prefetch=1,
        grid=(nb, n_steps),
        in_specs=[pl.BlockSpec((1, R, KV_LORA + ROPE_DIM), lambda b, j, pt_ref: (b, 0, 0)),
                  pl.BlockSpec((1, TS, KV_LORA), lambda b, j, pt_ref: (b, 0, 0)),
                  pl.BlockSpec((1, TS, ROPE_DIM), lambda b, j, pt_ref: (b, 0, 0)),
                  pl.BlockSpec((H_B, KV_LORA, V_DIM), lambda b, j, pt_ref: (0, 0, 0)),
                  pl.BlockSpec(memory_space=pl.ANY),
                  pl.BlockSpec(memory_space=pl.ANY)],
        out_specs=pl.BlockSpec((1, TS, H_B * V_DIM), lambda b, j, pt_ref: (b, 0, 0)),
        scratch_shapes=[pltpu.VMEM((2, PP, PAGE_SIZE, KV_LORA), F32),
                        pltpu.VMEM((2, PP, ROPE_DIM, PAGE_SIZE), F32),
                        pltpu.SemaphoreType.DMA((2, 2)),
                        pltpu.VMEM((R, 1), F32), pltpu.VMEM((R, 1), F32), pltpu.VMEM((R, KV_LORA), F32)],
    )
    return pl.pallas_call(
        functools.partial(_decode_kernel, PP=PP, n_steps=n_steps, n_total=nb * n_steps, TS=TS),
        out_shape=jax.ShapeDtypeStruct((nb, TS, H_B * V_DIM), BF16),
        grid_spec=grid_spec,
        compiler_params=_cparams(2),
        name="mla_decode",
    )(pt, qd, ckv_new, kpe_new, w_uv, cache_ckv, kpe_t)


def _rope_table(pos):
    half = ROPE_DIM // 2
    inv = ROPE_BASE ** (-jnp.arange(half, dtype=F32) / half)
    ang = pos.astype(F32)[:, None] * inv[None, :]
    cos, sin = jnp.cos(ang), jnp.sin(ang)
    return jnp.concatenate([cos, cos, -sin, sin], axis=-1)


def _swap_halves(w):
    half = ROPE_DIM // 2
    return jnp.concatenate([w[..., half:], w[..., :half]], axis=-1)


def _prep_rwkv_weights(l, P):
    w_in = P["w_in_a"][l]
    mu = P["mu_a"][l]
    cuts = [0, MIX_A, MIX_A + DECAY_LORA, 2 * MIX_A + DECAY_LORA, 3 * MIX_A + DECAY_LORA,
            3 * MIX_A + DECAY_LORA + A_LORA, C_RWKV, C_A]
    seg = lambda a, i: a[..., cuts[i]:cuts[i + 1]]
    r_w, wl_w, k_w, v_w, al_w, gl_w, qm_w = [seg(w_in, i) for i in range(7)]
    r_m, wl_m, k_m, v_m, al_m, gl_m = [seg(mu, i) for i in range(6)]
    zc = lambda n: jnp.zeros((D_MODEL, n), F32)
    zm = lambda n: jnp.zeros((n,), F32)
    if l > 0:
        vl_w, vl_m = P["w_vres_in"][l - 1], P["mu_vres"][l - 1]
        wv_up, v0 = P["w_vres_up"][l - 1], P["v0"][l - 1]
    else:
        vl_w, vl_m = zc(VRES_LORA), zm(VRES_LORA)
        wv_up, v0 = jnp.zeros((VRES_LORA, MIX_A), F32), zm(MIX_A)
    w_full = jnp.concatenate([r_w, k_w, v_w, wl_w, al_w, gl_w, vl_w, zc(64), qm_w, zc(128)], axis=1)
    mu_full = jnp.concatenate([r_m, k_m, v_m, wl_m, al_m, gl_m, vl_m, zm(64), zm(MEM_WIDTH), zm(128)])
    zr = lambda n: jnp.zeros((n, MIX_A), F32)
    return dict(
        w_in=w_full.astype(BF16),
        mu=mu_full.reshape(1, PA_COLS),
        wd=jnp.concatenate([P["w_decay_up"][l], zr(A_LORA)], axis=0).astype(BF16),
        wa=jnp.concatenate([zr(DECAY_LORA), P["w_a_up"][l]], axis=0).astype(BF16),
        wg=jnp.concatenate([P["w_g_up"][l], zr(256 - GATE_LORA)], axis=0).astype(BF16),
        wv=jnp.concatenate([zr(GATE_LORA), wv_up, zr(256 - GATE_LORA - VRES_LORA)], axis=0).astype(BF16),
        w0=P["w0"][l].reshape(1, MIX_A), a0=P["a0"][l].reshape(1, MIX_A), v0=v0.reshape(1, MIX_A),
        k_k=P["k_k"][l].reshape(1, MIX_A), k_a=P["k_a"][l].reshape(1, MIX_A),
    )


def _prep_mla_weights(P):
    w_kv_a = P["w_kv_a"]
    w_kva = jnp.concatenate([w_kv_a, _swap_halves(w_kv_a[:, KV_LORA:])], axis=1).astype(BF16)
    w_kv_b = P["w_kv_b"]
    w_uk, w_uv = w_kv_b[..., :NOPE_DIM], w_kv_b[..., NOPE_DIM:]
    k_part = jnp.zeros((KV_LORA + 128, H_B, 256), F32)
    k_part = k_part.at[:KV_LORA, :, :NOPE_DIM].set(w_uk)
    eye = jnp.eye(ROPE_DIM, dtype=F32)
    k_part = k_part.at[KV_LORA:KV_LORA + ROPE_DIM, :, NOPE_DIM:NOPE_DIM + ROPE_DIM].set(
        jnp.broadcast_to(eye[:, None, :], (ROPE_DIM, H_B, ROPE_DIM)))
    v_part = jnp.zeros((KV_LORA + 128, H_B, V_DIM), F32).at[:KV_LORA].set(w_uv)
    w_kvx = jnp.concatenate([k_part.reshape(KV_LORA + 128, H_B * 256), v_part.reshape(KV_LORA + 128, H_B * V_DIM)],
                            axis=1).astype(BF16)
    w_q = []
    for j in range(N_B):
        wq = P["w_q_b"][j].reshape(Q_LORA, H_B, NOPE_DIM + ROPE_DIM)
        rope_w = wq[..., NOPE_DIM:]
        w_q.append(jnp.concatenate([wq[..., :NOPE_DIM], rope_w, _swap_halves(rope_w)], axis=-1)
                   .reshape(Q_LORA, H_B * 256).astype(BF16))
    return dict(
        w_kva=w_kva, w_kvx=w_kvx, w_q=w_q,
        w_ukT=jnp.transpose(w_uk, (1, 2, 0)).astype(BF16),
        w_uv=jnp.transpose(w_uv, (1, 0, 2)).astype(BF16),
    )


def _trunk(x, pos, mem_k, mem_v, wkv0, shift0, conv0, past, P, W, *, nseq, T):
    M = nseq * T
    x = x.reshape(M, D_MODEL)
    cs = _rope_table(pos)
    if T < ROW_TILES[0]:
        cs = jnp.tile(cs, (nseq, 1))
    chunk = 64 if T >= 64 else T
    new_wkv, new_shift, new_conv = [], [], []
    v_first = None
    ckv = kpe = ckpe = kvx = None
    for l in range(DEPTH):
        g_mix = P["norm_mix"][l]
        if l < N_A:
            pw = W["rwkv"][l]
            x_last = x.reshape(nseq, T, D_MODEL)[:, -1]
            new_shift.append(_rmsnorm(x_last, g_mix, name="shift_norm"))
            sp = _mm([shift0[l].astype(F32)], [pw["w_in"]], name="rwkv_in_shift")
            r, lw, k, v, kn, b, g, q_mem = _rwkv_prep(x, sp, pw, v_first if l > 0 else None, g_mix, nseq=nseq, T=T)
            if l == 0:
                v_first = v
            h0 = wkv0[l].astype(F32)
            o_tok, h_new = _wkv(r, lw, k, v, kn, b, g, P["r_k"][l], P["lnx_w"][l], P["lnx_b"][l], h0,
                                nseq=nseq, T=T, C=chunk)
            new_wkv.append(h_new)
        else:
            j = l - N_A
            if l == N_A:
                hkv = _mm([x], [W["mla"]["w_kva"]], g=P["norm_kv"], name="kv_a")
                ckv, kpe, ckpe = _latent_post(hkv, P["norm_ckv"], cs, T=T)
                if past is None:
                    kvx = _mm([ckpe], [W["mla"]["w_kvx"]], out_dtype=BF16, name="kv_expand")
            proj = _mm([x], [W["w_in_b"][j]], g=g_mix, name="mla_in")
            qf = _mm([proj], [W["mla"]["w_q"][j]], g=P["norm_q"][j], name="q_b")
            q_mem = proj[:, Q_LORA:].astype(BF16)
            if past is None:
                q = _q_post(qf, cs, T=T, out_dtype=BF16)
                o_tok = _flash_causal(q, kvx, nb=nseq, T=T)
            else:
                q = _q_post(qf, cs, T=T, out_dtype=BF16).reshape(M, H_B, 256)
                q_lat = jnp.stack([_mm([q[:, h, :NOPE_DIM]], [W["mla"]["w_ukT"][h]], out_dtype=BF16, name="q_absorb")
                                   for h in range(H_B)], axis=1)
                qd = jnp.concatenate([q_lat, q[:, :, NOPE_DIM:NOPE_DIM + ROPE_DIM]], axis=-1)
                qd = qd.reshape(nseq, T, H_B, KV_LORA + ROPE_DIM).transpose(0, 2, 1, 3).reshape(nseq, H_B * T, -1)
                o_tok = _decode_attn(past[2], qd, past[0], past[1], ckv.reshape(nseq, T, KV_LORA),
                                     kpe.reshape(nseq, T, ROPE_DIM), W["mla"]["w_uv"]).reshape(M, H_B * V_DIM)
        o_mem = _mem_attn(q_mem, mem_k, mem_v, l, nb=nseq, T=T)
        x = _mm([o_tok, o_mem], [W["w_o_tok"][l], W["w_o_mem"][l]], res=x, name="w_o")
        x, c_new = _ffn(x, P["norm_ffn"][l], W["w_gate"][l], W["w_val"][l], P["conv_w"][l], P["conv_b"][l],
                        W["w_down"][l], conv0[l], nseq=nseq, T=T)
        new_conv.append(c_new)
    y = _rmsnorm(x, P["final_norm"], name="final_norm")
    return (y.reshape(nseq, T, D_MODEL), ckv.reshape(nseq, T, KV_LORA), kpe.reshape(nseq, T, ROPE_DIM),
            jnp.stack(new_wkv), jnp.stack(new_shift), jnp.stack(new_conv))


def kernel(x_prompt, x_sample, cache_ckv, cache_kpe, cache_mem_k, cache_mem_v, state_wkv, state_shift, state_conv, page_table, mem_prompt, norm_mix, norm_ffn, norm_mem, w_mem_kv, w_o, w_in_a, mu_a, w_vres_in, mu_vres, w_decay_up, w0, w_a_up, a0, w_g_up, w_vres_up, v0, k_k, k_a, r_k, lnx_w, lnx_b, norm_kv, w_kv_a, norm_ckv, w_kv_b, w_in_b, norm_q, w_q_b, w_ffn_up, conv_w, conv_b, w_ffn_down, final_norm):
    P = dict(norm_mix=norm_mix, norm_ffn=norm_ffn, w_o=w_o, w_in_a=w_in_a, mu_a=mu_a, w_vres_in=w_vres_in,
             mu_vres=mu_vres, w_decay_up=w_decay_up, w0=w0, w_a_up=w_a_up, a0=a0, w_g_up=w_g_up,
             w_vres_up=w_vres_up, v0=v0, k_k=k_k, k_a=k_a, r_k=r_k, lnx_w=lnx_w, lnx_b=lnx_b,
             norm_kv=norm_kv, w_kv_a=w_kv_a, norm_ckv=norm_ckv, w_kv_b=w_kv_b, w_in_b=w_in_b, norm_q=norm_q,
             w_q_b=w_q_b, conv_w=conv_w, conv_b=conv_b, final_norm=final_norm)
    W = dict(
        rwkv=[_prep_rwkv_weights(l, P) for l in range(N_A)],
        mla=_prep_mla_weights(P),
        w_in_b=[w_in_b[j].astype(BF16) for j in range(N_B)],
        w_o_tok=[w_o[l, :MIX_A].astype(BF16) for l in range(DEPTH)],
        w_o_mem=[w_o[l, MIX_A:].astype(BF16) for l in range(DEPTH)],
        w_gate=[w_ffn_up[l, :, :D_FF].astype(BF16) for l in range(DEPTH)],
        w_val=[w_ffn_up[l, :, D_FF:].astype(BF16) for l in range(DEPTH)],
        w_down=[w_ffn_down[l].astype(BF16) for l in range(DEPTH)],
    )
    dt = x_prompt.dtype
    B, T = x_prompt.shape[:2]
    n_mem = mem_prompt.shape[1]
    mem_k_all, mem_v_all = _mem_kv(mem_prompt, norm_mem, w_mem_kv)
    y_p, ckv_p, kpe_p, wkv_p, shift_p, conv_p = _trunk(
        x_prompt, jnp.arange(T), mem_k_all, mem_v_all,
        jnp.zeros((N_A, B, H_A, HEAD_A, HEAD_A), dt), jnp.zeros((N_A, B, D_MODEL), dt),
        jnp.zeros((DEPTH, B, CONV_W - 1, D_FF), dt), None, P, W, nseq=B, T=T)
    DB, TS = x_sample.shape[:2]
    past_len = page_table.shape[1] * PAGE_SIZE
    to_t = lambda c: jnp.transpose(c, (0, 1, 3, 4, 2)).reshape(DEPTH, DB, MEM_WIDTH, n_mem)
    y_s, ckv_s, kpe_s, wkv_s, shift_s, conv_s = _trunk(
        x_sample, past_len + jnp.arange(TS), to_t(cache_mem_k), to_t(cache_mem_v), state_wkv, state_shift, state_conv,
        (cache_ckv, cache_kpe, page_table), P, W, nseq=DB, T=TS)
    from_t = lambda a: jnp.transpose(a.reshape(DEPTH, B, MEM_HEADS, MEM_HEAD_DIM, n_mem), (0, 1, 4, 2, 3))
    mem_k_out = from_t(mem_k_all)
    mem_v_out = from_t(mem_v_all)
    return (y_p, y_s, ckv_p, kpe_p, mem_k_out, mem_v_out, wkv_p, shift_p, conv_p,
            ckv_s, kpe_s, wkv_s, shift_s, conv_s)
```

```python
import functools
import math

import jax
import jax.numpy as jnp
from jax import lax
from jax.experimental import pallas as pl
from jax.experimental.pallas import tpu as pltpu

F32 = jnp.float32
BF16 = jnp.bfloat16

D_MODEL = 1024
DEPTH = 4
N_A = 2
N_B = 2
MEM_WIDTH = 256
MIX_A = 768
HEAD_A = 64
H_A = 12
DECAY_LORA = 64
A_LORA = 64
VRES_LORA = 32
GATE_LORA = 160
GN_EPS = 64e-5
C_RWKV = 3 * MIX_A + DECAY_LORA + A_LORA + GATE_LORA
C_A = C_RWKV + MEM_WIDTH
MEM_HEADS = 4
MEM_HEAD_DIM = 64
MEM_SCALE = MEM_HEAD_DIM ** -0.5
NOPE_DIM = 128
ROPE_DIM = 64
V_DIM = 128
H_B = 6
Q_LORA = 256
KV_LORA = 256
ROPE_BASE = 10000.0
ATTN_SCALE = (NOPE_DIM + ROPE_DIM) ** -0.5
D_FF = 2816
CONV_W = 3
RMS_EPS = 1e-6
PAGE_SIZE = 128

PA_R, PA_K, PA_V = 0, 768, 1536
PA_LORA = 2304
PA_GATE = 2432
PA_QMEM = 2688
PA_COLS = 3072

VMEM_LIMIT_BYTES = 52 * 1024 * 1024

NN = ((1,), (0,))
NT = ((1,), (1,))
TN = ((0,), (0,))


def _dg(a, b, dims=NN):
    return lax.dot_general(a, b, (dims, ((), ())), preferred_element_type=F32)


def _split(x):
    hi = x.astype(BF16)
    lo = (x - hi.astype(F32)).astype(BF16)
    return hi, lo


def _dot3(a, b, dims=NN):
    ah, al = _split(a)
    bh, bl = _split(b)
    return _dg(ah, bh, dims) + (_dg(ah, bl, dims) + _dg(al, bh, dims))


def _transpose_exact(x, eye_bf16):
    hi = x.astype(BF16)
    r1 = x - hi.astype(F32)
    mid = r1.astype(BF16)
    lo = (r1 - mid.astype(F32)).astype(BF16)
    return (_dg(eye_bf16, hi, NT) + _dg(eye_bf16, mid, NT)) + _dg(eye_bf16, lo, NT)


def _cparams(n_axes):
    return pltpu.CompilerParams(dimension_semantics=("arbitrary",) * n_axes,
                                vmem_limit_bytes=VMEM_LIMIT_BYTES)


def _rms(x, g, eps=RMS_EPS):
    return x * lax.rsqrt(jnp.mean(x * x, axis=-1, keepdims=True) + eps) * g


def _pick_tile(n, candidates):
    for c in candidates:
        if n % c == 0:
            return c
    return n


ROW_TILES = (512, 256, 128, 64, 32, 16, 8)


def _pos_tile(M, T):
    return _pick_tile(T, ROW_TILES) if T >= ROW_TILES[0] else _pick_tile(M, ROW_TILES)


def _mm_kernel(*refs, n_in, has_norm, has_res, f32_in):
    xs = refs[:n_in]
    ws = refs[n_in:2 * n_in]
    k = 2 * n_in
    g_ref = refs[k] if has_norm else None
    k += int(has_norm)
    r_ref = refs[k] if has_res else None
    k += int(has_res)
    o_ref = refs[k]
    scr = list(refs[k + 1:])

    lhs = []
    si = 0
    for idx in range(n_in):
        if f32_in[idx]:
            s_ref = scr[si]
            si += 1

            @pl.when(pl.program_id(1) == 0)
            def _(x_ref=xs[idx], s_ref=s_ref, idx=idx):
                x = x_ref[...]
                if has_norm and idx == 0:
                    x = _rms(x, g_ref[...])
                s_ref[...] = x.astype(BF16)

            lhs.append(s_ref)
        else:
            lhs.append(xs[idx])
    acc = _dg(lhs[0][...], ws[0][...])
    for idx in range(1, n_in):
        acc = acc + _dg(lhs[idx][...], ws[idx][...])
    if has_res:
        acc = acc + r_ref[...]
    o_ref[...] = acc.astype(o_ref.dtype)


def _mm(xs, ws, *, g=None, res=None, out_dtype=F32, x_col_blocks=None, ks=None, name="mm"):
    n_in = len(xs)
    M = xs[0].shape[0]
    N = ws[0].shape[1]
    ks = [w.shape[0] for w in ws]
    x_col_blocks = x_col_blocks or [0] * n_in
    tm = _pick_tile(M, (512, 256, 128, 64, 32, 16, 8))
    tn = _pick_tile(N, (1024, 768, 512, 384, 256, 128))
    f32_in = tuple(x.dtype == F32 for x in xs)
    in_specs = []
    for x, kk, cb in zip(xs, ks, x_col_blocks):
        in_specs.append(pl.BlockSpec((tm, kk), lambda i, j, cb=cb: (i, cb)))
    for w, kk in zip(ws, ks):
        in_specs.append(pl.BlockSpec((kk, tn), lambda i, j: (0, j)))
    args = list(xs) + list(ws)
    if g is not None:
        in_specs.append(pl.BlockSpec((1, ks[0]), lambda i, j: (0, 0)))
        args.append(g.reshape(1, ks[0]).astype(F32))
    if res is not None:
        in_specs.append(pl.BlockSpec((tm, tn), lambda i, j: (i, j)))
        args.append(res)
    scratch = [pltpu.VMEM((tm, kk), BF16) for kk, f in zip(ks, f32_in) if f]
    return pl.pallas_call(
        functools.partial(_mm_kernel, n_in=n_in, has_norm=g is not None, has_res=res is not None,
                          f32_in=f32_in),
        out_shape=jax.ShapeDtypeStruct((M, N), out_dtype),
        grid=(M // tm, N // tn),
        in_specs=in_specs,
        out_specs=pl.BlockSpec((tm, tn), lambda i, j: (i, j)),
        scratch_shapes=scratch,
        compiler_params=_cparams(2),
        name=name,
    )(*args)


def _rmsnorm_kernel(x_ref, g_ref, o_ref):
    o_ref[...] = _rms(x_ref[...], g_ref[...])


def _rmsnorm(x, g, name="rmsnorm"):
    M, D = x.shape
    tm = _pick_tile(M, (512, 256, 128, 64, 32, 16, 8))
    return pl.pallas_call(
        _rmsnorm_kernel,
        out_shape=jax.ShapeDtypeStruct((M, D), F32),
        grid=(M // tm,),
        in_specs=[pl.BlockSpec((tm, D), lambda i: (i, 0)), pl.BlockSpec((1, D), lambda i: (0, 0))],
        out_specs=pl.BlockSpec((tm, D), lambda i: (i, 0)),
        compiler_params=_cparams(1),
        name=name,
    )(x, g.reshape(1, D).astype(F32))


def _mem_kv_kernel(x_ref, g_ref, wt_ref, k_ref, v_ref):
    kvt = _dg(wt_ref[0], _rms(x_ref[0], g_ref[0]).astype(BF16), NT)
    k_ref[0, 0] = kvt[:MEM_WIDTH]
    v_ref[0, 0] = kvt[MEM_WIDTH:]


def _mem_kv(mem, norm_mem, w_mem_kv):
    B, n_mem, _ = mem.shape
    out = jax.ShapeDtypeStruct((DEPTH, B, MEM_WIDTH, n_mem), F32)
    wt = jnp.swapaxes(w_mem_kv, 1, 2).astype(BF16)
    return pl.pallas_call(
        _mem_kv_kernel,
        out_shape=(out, out),
        grid=(DEPTH, B),
        in_specs=[pl.BlockSpec((1, n_mem, D_MODEL), lambda l, b: (b, 0, 0)),
                  pl.BlockSpec((1, 1, D_MODEL), lambda l, b: (l, 0, 0)),
                  pl.BlockSpec((1, 2 * MEM_WIDTH, D_MODEL), lambda l, b: (l, 0, 0))],
        out_specs=(pl.BlockSpec((1, 1, MEM_WIDTH, n_mem), lambda l, b: (l, b, 0, 0)),
                   pl.BlockSpec((1, 1, MEM_WIDTH, n_mem), lambda l, b: (l, b, 0, 0))),
        compiler_params=_cparams(2),
        name="mem_kv",
    )(mem, norm_mem.reshape(DEPTH, 1, D_MODEL).astype(F32), wt)


def _segsum64(x):
    r = lax.broadcasted_iota(jnp.int32, (128, 128), 0) >> 6
    c = lax.broadcasted_iota(jnp.int32, (128, 128), 1) >> 6
    ones = jnp.where(r == c, 1.0, 0.0).astype(BF16)
    outs = []
    for j in range(x.shape[1] // 128):
        hi, lo = _split(x[:, j * 128:(j + 1) * 128])
        outs.append(_dg(hi, ones) + _dg(lo, ones))
    return jnp.concatenate(outs, axis=1)


def _softplus(x):
    return jnp.maximum(x, 0.0) + jnp.log1p(jnp.exp(-jnp.abs(x)))


def _prep_kernel(*refs, has_vres, blocks_per_seq):
    if blocks_per_seq:
        x_ref, gmix_ref, win_ref = refs[:3]
        refs = refs[2:]
    (cur_ref, prev_ref, mu_ref, wd_ref, wa_ref, wg_ref, wv_ref, w0_ref, a0_ref, v0_ref, kk_ref, ka_ref) = refs[:12]
    k = 12
    vf_ref = refs[k] if has_vres else None
    k += int(has_vres)
    r_o, lw_o, k_o, v_o, kn_o, b_o, g_o, qm_o = refs[k:k + 8]

    if blocks_per_seq:
        cur_ref, carry_scr = refs[k + 8], refs[k + 9]
        xn = _rms(x_ref[...], gmix_ref[...]).astype(BF16)
        for lo in range(0, PA_COLS, 512):
            cur_ref[:, lo:lo + 512] = _dg(xn, win_ref[:, lo:lo + 512])
    tm = cur_ref.shape[0]

    if blocks_per_seq:

        @pl.when(pl.program_id(0) % blocks_per_seq == 0)
        def _():
            carry_scr[...] = prev_ref[0]

        first_row = lax.broadcasted_iota(jnp.int32, (tm, 1), 0) == 0

    def mix(lo, hi):
        c = cur_ref[:, lo:hi]
        if blocks_per_seq:
            p = jnp.where(first_row, carry_scr[:, lo:hi], pltpu.roll(c, 1, 0))
        else:
            p = prev_ref[:, lo:hi]
        return c + (p - c) * mu_ref[:, lo:hi]

    lora = mix(PA_LORA, PA_LORA + 128)
    zw = _dg(jnp.tanh(lora).astype(BF16), wd_ref[...]) + w0_ref[...]
    w_log = -_softplus(-zw) - 0.5
    lw_o[...] = -jnp.exp(w_log)
    a = jax.nn.sigmoid(_dg(lora.astype(BF16), wa_ref[...]) + a0_ref[...])
    gin = mix(PA_GATE, PA_GATE + 256)
    g_o[...] = _dg(jax.nn.sigmoid(gin).astype(BF16), wg_ref[...])
    r_o[...] = mix(PA_R, PA_R + MIX_A)
    kx = mix(PA_K, PA_K + MIX_A)
    v = mix(PA_V, PA_V + MIX_A)
    if has_vres:
        sv = jax.nn.sigmoid(_dg(gin.astype(BF16), wv_ref[...]) + v0_ref[...])
        v = v + (vf_ref[...] - v) * sv
    v_o[...] = v
    kk = kx * kk_ref[...]
    kk = kk * lax.rsqrt(jnp.maximum(_segsum64(kk * kk), 1e-24))
    kn_o[...] = kk
    b_o[...] = kk * a
    k_o[...] = kx * (1.0 + (a - 1.0) * ka_ref[...])
    qm_o[...] = cur_ref[:, PA_QMEM:PA_QMEM + MEM_WIDTH].astype(BF16)
    if blocks_per_seq:
        carry_scr[...] = cur_ref[tm - 1:tm, :]


def _rwkv_prep(x, sp, pw, v_first, g_mix, *, nseq, T):
    M = x.shape[0]
    has_vres = v_first is not None
    row = lambda w: pl.BlockSpec((tm, w), lambda i: (i, 0))
    full = lambda a: pl.BlockSpec(a.shape, lambda i: (0,) * a.ndim)
    if T >= 256:
        tm = 256
        blocks_per_seq = T // tm
        g2 = g_mix.reshape(1, D_MODEL).astype(F32)
        lead_args = [x, g2, pw["w_in"], sp.reshape(nseq, 1, PA_COLS)]
        lead_specs = [row(D_MODEL), full(g2),
                      pl.BlockSpec(pw["w_in"].shape, lambda i: (0, 0), pipeline_mode=pl.Buffered(1)),
                      pl.BlockSpec((1, 1, PA_COLS), lambda i: (i // blocks_per_seq, 0, 0))]
        scratch = [pltpu.VMEM((tm, PA_COLS), F32), pltpu.VMEM((1, PA_COLS), F32)]
    else:
        tm = _pick_tile(M, (256, 128, 64, 32, 16, 8))
        blocks_per_seq = 0
        cur = _mm([x], [pw["w_in"]], g=g_mix, name="rwkv_in")
        prev = jnp.concatenate([sp[:, None, :], cur.reshape(nseq, T, PA_COLS)[:, :-1]], axis=1).reshape(M, PA_COLS)
        lead_args = [cur, prev]
        lead_specs = [row(PA_COLS), row(PA_COLS)]
        scratch = []
    consts = [pw["mu"], pw["wd"], pw["wa"], pw["wg"], pw["wv"], pw["w0"], pw["a0"], pw["v0"], pw["k_k"], pw["k_a"]]
    args = lead_args + consts
    in_specs = lead_specs + [full(a) for a in consts]
    if has_vres:
        args.append(v_first)
        in_specs.append(row(MIX_A))
    outs = [jax.ShapeDtypeStruct((M, MIX_A), F32)] * 7 + [jax.ShapeDtypeStruct((M, MEM_WIDTH), BF16)]
    out_specs = [row(MIX_A)] * 7 + [row(MEM_WIDTH)]
    return pl.pallas_call(
        functools.partial(_prep_kernel, has_vres=has_vres, blocks_per_seq=blocks_per_seq),
        out_shape=outs,
        grid=(M // tm,),
        in_specs=in_specs,
        out_specs=out_specs,
        scratch_shapes=scratch,
        compiler_params=_cparams(1),
        name="rwkv_prep",
    )(*args)


def _wkv_kernel(r_ref, lw_ref, k_ref, v_ref, kn_ref, b_ref, g_ref, rk_ref, lnw_ref, lnb_ref, h0_ref,
                o_ref, hT_ref, h_scr, *, C, H, N, nchunks, xpose):
    c_idx = pl.program_id(1)

    eye_bf = jnp.where(lax.broadcasted_iota(jnp.int32, (N, N), 0) == lax.broadcasted_iota(jnp.int32, (N, N), 1),
                       1.0, 0.0).astype(BF16)

    @pl.when(c_idx == 0)
    def _():
        if xpose:
            for h in range(H):
                h_scr[h] = _transpose_exact(h0_ref[0, h], eye_bf)
        else:
            h_scr[...] = h0_ref[0]

    lw = lw_ref[...]
    ti = lax.broadcasted_iota(jnp.int32, (C, C), 0)
    tj = lax.broadcasted_iota(jnp.int32, (C, C), 1)
    ltri = jnp.where(tj <= ti, 1.0, 0.0).astype(BF16)
    lw_hi, lw_lo = _split(lw)
    cum = _dg(ltri, lw_hi) + _dg(ltri, lw_lo)
    cum_last = jnp.sum(lw, axis=0, keepdims=True)
    e_neg = jnp.exp(-cum)
    r_all = r_ref[...]
    k_all = k_ref[...]
    v_all = v_ref[...]
    kt_all = kn_ref[...] * jnp.exp(cum - lw)
    rt_all = r_all * jnp.exp(cum)
    kh_all = k_all * e_neg
    bh_all = b_ref[...] * e_neg
    e_rem = jnp.exp(cum_last - cum)
    kb_all = k_all * e_rem
    bb_all = b_ref[...] * e_rem
    p_last = jnp.exp(cum_last)
    rkr = r_all * k_all * rk_ref[...]

    strict = tj < ti
    incl = tj <= ti
    ni = lax.broadcasted_iota(jnp.int32, (N, N), 0)
    nj = lax.broadcasted_iota(jnp.int32, (N, N), 1)
    eye_n = ni == nj
    eye_c = jnp.where(ti == tj, 1.0, 0.0)

    hs = range(H)
    sls = [slice(h * N, (h + 1) * N) for h in hs]
    cut = lambda a: [a[:, sl].astype(BF16) for sl in sls]
    kt, rt, kh, bh, kb, bb, vv = (cut(a) for a in (kt_all, rt_all, kh_all, bh_all, kb_all, bb_all, v_all))
    a_k = [jnp.where(strict, _dg(kt[h], kh[h], NT), 0.0).astype(BF16) for h in hs]
    a_b = [jnp.where(strict, _dg(kt[h], bh[h], NT), 0.0) for h in hs]
    q_k = [jnp.where(incl, _dg(rt[h], kh[h], NT), 0.0).astype(BF16) for h in hs]
    q_b = [jnp.where(incl, _dg(rt[h], bh[h], NT), 0.0).astype(BF16) for h in hs]
    tinv = [eye_c - jnp.where(((ti >> 1) == (tj >> 1)), a_b[h], 0.0) for h in hs]
    blk = 2
    while blk < C:
        sh = blk.bit_length() - 1
        lvl = ((ti >> (sh + 1)) == (tj >> (sh + 1))) & (((ti >> sh) & 1) == 1) & (((tj >> sh) & 1) == 0)
        t16 = [tinv[h].astype(BF16) for h in hs]
        mt = [_dg(jnp.where(lvl, a_b[h], 0.0).astype(BF16), t16[h]).astype(BF16) for h in hs]
        tinv = [tinv[h] - _dg(t16[h], mt[h]) for h in hs]
        blk *= 2
    t16 = [tinv[h].astype(BF16) for h in hs]
    akv = [_dg(a_k[h], vv[h]).astype(BF16) for h in hs]
    x = [_dg(t16[h], jnp.concatenate([kt[h], akv[h]], axis=1)).astype(BF16) for h in hs]
    qw = [_dg(q_b[h], x[h]) for h in hs]
    r2 = [(rt_all[:, sls[h]] - qw[h][:, :N]).astype(BF16) for h in hs]
    y0 = [_dg(q_k[h], vv[h]) - qw[h][:, N:] for h in hs]
    bw = [_dg(bb[h], x[h], TN) for h in hs]
    gmat = [(jnp.where(eye_n, jnp.broadcast_to(p_last[:, sls[h]], (N, N)), 0.0) - bw[h][:, :N]).astype(BF16) for h in hs]
    h_add = [_dg(kb[h], vv[h], TN) - bw[h][:, N:] for h in hs]
    h0 = [h_scr[h].astype(BF16) for h in hs]
    y = [_dg(r2[h], h0[h]) + y0[h] for h in hs]
    for h in hs:
        h_scr[h] = _dg(gmat[h], h0[h]) + h_add[h]
    for h in hs:
        sl = sls[h]
        mean = jnp.mean(y[h], axis=-1, keepdims=True)
        yc = y[h] - mean
        var = jnp.mean(yc * yc, axis=-1, keepdims=True)
        yn = yc * lax.rsqrt(var + GN_EPS)
        yn = yn * lnw_ref[:, sl] + lnb_ref[:, sl]
        bonus = jnp.sum(rkr[:, sl], axis=-1, keepdims=True) * v_all[:, sl]
        o_ref[:, sl] = ((yn + bonus) * g_ref[:, sl]).astype(o_ref.dtype)

    @pl.when(c_idx == nchunks - 1)
    def _():
        if xpose:
            for h in range(H):
                hT_ref[0, h] = _transpose_exact(h_scr[h], eye_bf)
        else:
            hT_ref[0] = h_scr[...]


def _wkv(r, lw, k, v, kn, b, g, r_k, lnx_w, lnx_b, h0, *, nseq, T, C):
    M = r.shape[0]
    nchunks = T // C
    HN = MIX_A
    xpose = nchunks > 1
    h_in = h0 if xpose else jnp.swapaxes(h0, -1, -2)
    row = pl.BlockSpec((C, HN), lambda s, c: (s * nchunks + c, 0))
    vec = pl.BlockSpec((1, HN), lambda s, c: (0, 0))
    st = pl.BlockSpec((1, H_A, HEAD_A, HEAD_A), lambda s, c: (s, 0, 0, 0))
    o, h_new = pl.pallas_call(
        functools.partial(_wkv_kernel, C=C, H=H_A, N=HEAD_A, nchunks=nchunks, xpose=xpose),
        out_shape=(jax.ShapeDtypeStruct((M, HN), BF16),
                   jax.ShapeDtypeStruct((nseq, H_A, HEAD_A, HEAD_A), F32)),
        grid=(nseq, nchunks),
        in_specs=[row] * 7 + [vec] * 3 + [st],
        out_specs=(row, st),
        scratch_shapes=[pltpu.VMEM((H_A, HEAD_A, HEAD_A), F32)],
        compiler_params=_cparams(2),
        name="wkv7",
    )(r, lw, k, v, kn, b, g, r_k.reshape(1, HN), lnx_w.reshape(1, HN), lnx_b.reshape(1, HN), h_in)
    return o, (h_new if xpose else jnp.swapaxes(h_new, -1, -2))


def _mem_attn_kernel(q_ref, k_ref, v_ref, o_ref, *, nbb, tq):
    c = MEM_SCALE * math.log2(math.e)
    lane_head = lax.broadcasted_iota(jnp.int32, (1, MEM_WIDTH), 1) >> 6
    items = [(bb, h) for bb in range(nbb) for h in range(MEM_HEADS)]
    q_all = q_ref[...].astype(F32)
    qs = [q_all[bb * tq:(bb + 1) * tq, :] for bb in range(nbb)]
    kts = [k_ref[0, bb].astype(BF16) for bb in range(nbb)]
    vts = [v_ref[0, bb].astype(BF16) for bb in range(nbb)]
    ss = [_dg(jnp.where(lane_head == h, qs[bb], 0.0).astype(BF16), kts[bb]) for bb, h in items]
    ms = [jnp.max(s, axis=-1, keepdims=True) for s in ss]
    ps = [jnp.exp2((s - m) * c) for s, m in zip(ss, ms)]
    inv_l = [1.0 / jnp.sum(p, axis=-1, keepdims=True) for p in ps]
    pv = [_dg(p.astype(BF16), vts[bb], NT) for p, (bb, h) in zip(ps, items)]
    outs = []
    for bb in range(nbb):
        o = jnp.zeros((tq, MEM_WIDTH), F32)
        for h in range(MEM_HEADS):
            idx = bb * MEM_HEADS + h
            o = jnp.where(lane_head == h, pv[idx] * inv_l[idx], o)
        outs.append(o)
    o_ref[...] = (outs[0] if nbb == 1 else jnp.concatenate(outs, axis=0)).astype(o_ref.dtype)


def _mem_attn(q, mem_k, mem_v, layer, *, nb, T):
    M = q.shape[0]
    if T >= ROW_TILES[0]:
        tq, nbb = ROW_TILES[0], 1
    else:
        tq, nbb = T, _pick_tile(nb, (16, 8, 4, 2, 1))
    nq = T // tq
    n_mem = mem_k.shape[3]
    return pl.pallas_call(
        functools.partial(_mem_attn_kernel, nbb=nbb, tq=tq),
        out_shape=jax.ShapeDtypeStruct((M, MEM_WIDTH), BF16),
        grid=(nb // nbb, nq),
        in_specs=[pl.BlockSpec((nbb * tq, MEM_WIDTH), lambda b, i: (b * nq + i, 0)),
                  pl.BlockSpec((1, nbb, MEM_WIDTH, n_mem), lambda b, i: (layer, b, 0, 0)),
                  pl.BlockSpec((1, nbb, MEM_WIDTH, n_mem), lambda b, i: (layer, b, 0, 0))],
        out_specs=pl.BlockSpec((nbb * tq, MEM_WIDTH), lambda b, i: (b * nq + i, 0)),
        compiler_params=_cparams(2),
        name="mem_attn",
    )(q, mem_k, mem_v)


def _ffn_kernel(*refs, tm, T, long_mode, col_chunks, final_norm):
    (x_ref, ot_ref, om_ref, wot_ref, wom_ref, g_ref, wg_ref, wv_ref, cw_ref, cb_ref, wd_ref, cp_ref) = refs[:12]
    k = 12
    fin_ref = refs[k] if final_norm else None
    k += int(final_norm)
    o_ref, aux_ref = refs[k:k + 2]
    k += 2
    y_ref = refs[k] if final_norm else None
    k += int(final_norm)
    xn_scr, h_scr, x1_scr = refs[k:k + 3]
    if long_mode:
        carry_scr = refs[k + 3]
        tail_ref = aux_ref
    else:
        gate_ref = aux_ref
    x1_scr[...] = x_ref[...] + (_dg(ot_ref[...], wot_ref[...]) + _dg(om_ref[...], wom_ref[...]))
    xn_scr[...] = _rms(x1_scr[...], g_ref[...]).astype(BF16)
    xn = xn_scr[...]
    row = lax.broadcasted_iota(jnp.int32, (tm, 1), 0)
    if long_mode:
        nblk = T // tm

        @pl.when(pl.program_id(0) % nblk == 0)
        def _():
            carry_scr[...] = cp_ref[0]
    else:
        t_in_seq = row % T
        ncp = cp_ref.shape[0]
        rr = lax.broadcasted_iota(jnp.int32, (tm, ncp), 0)
        cc = lax.broadcasted_iota(jnp.int32, (tm, ncp), 1)
        same_seq = (rr // T) == (cc >> 1)
        sel1 = jnp.where(same_seq & ((rr % T) + 1 == (cc & 1)), 1.0, 0.0).astype(BF16)
        sel2 = jnp.where(same_seq & ((rr % T) == (cc & 1)), 1.0, 0.0).astype(BF16)
        sel = jnp.concatenate([sel1, sel2], axis=0)
    for (lo, w) in col_chunks:
        cs = slice(lo, lo + w)
        gate = _dg(xn, wg_ref[:, cs])
        val = _dg(xn, wv_ref[:, cs])
        r1 = pltpu.roll(gate, 1, 0)
        r2 = pltpu.roll(gate, 2, 0)
        if long_mode:
            c0 = carry_scr[6:7, cs]
            c1 = carry_scr[7:8, cs]
            p1 = jnp.where(row == 0, c1, r1)
            p2 = jnp.where(row == 0, c0, jnp.where(row == 1, c1, r2))
            tail = gate[tm - 8:tm, :]
            carry_scr[:, cs] = tail
            tail_ref[0, :, cs] = tail
        else:
            cpc = cp_ref[:, cs]
            hi = cpc.astype(BF16)
            r_hi = cpc - hi.astype(F32)
            mid = r_hi.astype(BF16)
            lo = (r_hi - mid.astype(F32)).astype(BF16)
            edge = (_dg(sel, hi) + _dg(sel, mid)) + _dg(sel, lo)
            p1 = jnp.where(t_in_seq >= 1, r1, 0.0) + edge[:tm]
            p2 = jnp.where(t_in_seq >= 2, r2, 0.0) + edge[tm:]
            gate_ref[:, cs] = gate
        conv = cb_ref[:, cs] + p2 * cw_ref[0:1, cs]
        conv = conv + p1 * cw_ref[1:2, cs]
        conv = conv + gate * cw_ref[2:3, cs]
        h_scr[:, cs] = (conv * jax.nn.sigmoid(conv) * val).astype(BF16)
    out = x1_scr[...] + _dg(h_scr[...], wd_ref[...])
    o_ref[...] = out
    if final_norm:
        y_ref[...] = _rms(out, fin_ref[...])


def _ffn(x, o_tok, o_mem, w_o_tok, w_o_mem, g, w_gate, w_val, conv_w, conv_b, w_down, conv_prev, final_g=None, *,
         nseq, T):
    M = x.shape[0]
    long_mode = T >= 512
    tm = 512 if long_mode else 256
    assert (T % tm == 0) if long_mode else (tm % T == 0 and M % tm == 0)
    col_chunks = tuple((lo, min(256, D_FF - lo)) for lo in range(0, D_FF, 256))
    const = lambda a: pl.BlockSpec(a.shape, lambda i: (0,) * a.ndim, pipeline_mode=pl.Buffered(1))
    small = lambda a: pl.BlockSpec(a.shape, lambda i: (0,) * a.ndim)
    g2 = g.reshape(1, D_MODEL).astype(F32)
    cb2 = conv_b.reshape(1, D_FF).astype(F32)
    rows = lambda w: pl.BlockSpec((tm, w), lambda i: (i, 0))
    common_specs = [rows(D_MODEL), rows(MIX_A), rows(MEM_WIDTH), const(w_o_tok), const(w_o_mem), small(g2),
                    const(w_gate), const(w_val), small(conv_w), small(cb2), const(w_down)]
    common_args = [x, o_tok, o_mem, w_o_tok, w_o_mem, g2, w_gate, w_val, conv_w, cb2, w_down]
    final_norm = final_g is not None
    fin_args = [final_g.reshape(1, D_MODEL).astype(F32)] if final_norm else []
    fin_specs = [pl.BlockSpec((1, D_MODEL), lambda i: (0, 0))] if final_norm else []
    fin_out = [jax.ShapeDtypeStruct((M, D_MODEL), F32)] if final_norm else []
    fin_out_specs = [rows(D_MODEL)] if final_norm else []
    base_scratch = [pltpu.VMEM((tm, D_MODEL), BF16), pltpu.VMEM((tm, D_FF), BF16), pltpu.VMEM((tm, D_MODEL), F32)]
    kern = functools.partial(_ffn_kernel, tm=tm, T=T, long_mode=long_mode, col_chunks=col_chunks,
                             final_norm=final_norm)
    if long_mode:
        nblk = T // tm
        cp = jnp.concatenate([jnp.zeros((nseq, 6, D_FF), F32), conv_prev.astype(F32)], axis=1)
        res = pl.pallas_call(
            kern,
            out_shape=[jax.ShapeDtypeStruct((M, D_MODEL), F32), jax.ShapeDtypeStruct((M // tm, 8, D_FF), F32)] + fin_out,
            grid=(M // tm,),
            in_specs=common_specs + [pl.BlockSpec((1, 8, D_FF), lambda i: (i // nblk, 0, 0))] + fin_specs,
            out_specs=[rows(D_MODEL), pl.BlockSpec((1, 8, D_FF), lambda i: (i, 0, 0))] + fin_out_specs,
            scratch_shapes=base_scratch + [pltpu.VMEM((8, D_FF), F32)],
            compiler_params=_cparams(1),
            name="convglu_long",
        )(*common_args, cp, *fin_args)
        new_conv = res[1].reshape(nseq, nblk, 8, D_FF)[:, -1, 6:8, :]
        return res[0], new_conv, (res[2] if final_norm else None)
    ncp = (CONV_W - 1) * (tm // T)
    res = pl.pallas_call(
        kern,
        out_shape=[jax.ShapeDtypeStruct((M, D_MODEL), F32), jax.ShapeDtypeStruct((M, D_FF), F32)] + fin_out,
        grid=(M // tm,),
        in_specs=common_specs + [pl.BlockSpec((ncp, D_FF), lambda i: (i, 0))] + fin_specs,
        out_specs=[rows(D_MODEL), rows(D_FF)] + fin_out_specs,
        scratch_shapes=base_scratch,
        compiler_params=_cparams(1),
        name="convglu_short",
    )(*common_args, conv_prev.astype(F32).reshape(nseq * (CONV_W - 1), D_FF), *fin_args)
    new_conv = res[1].reshape(nseq, T, D_FF)[:, T - 2:, :]
    return res[0], new_conv, (res[2] if final_norm else None)


def _rope128(x128, cs):
    a = x128 * cs
    s = a + pltpu.roll(a, 64, 1)
    lane = lax.broadcasted_iota(jnp.int32, a.shape, 1)
    return jnp.where(lane < 64, s, 0.0)


def _latent_kernel(h_ref, g_ref, cs_ref, ckv_ref, kpe_ref, ckpe_ref):
    ckv = _rms(h_ref[:, :KV_LORA], g_ref[...])
    rp = _rope128(h_ref[:, KV_LORA:KV_LORA + 128], cs_ref[...])
    ckv_ref[...] = ckv
    kpe_ref[...] = rp[:, :ROPE_DIM]
    ckpe_ref[:, :KV_LORA] = ckv.astype(BF16)
    ckpe_ref[:, KV_LORA:] = rp.astype(BF16)


def _latent_post(h, norm_ckv, cs, *, T):
    M = h.shape[0]
    tm = _pos_tile(M, T)
    ncs = cs.shape[0] // tm
    return pl.pallas_call(
        _latent_kernel,
        out_shape=(jax.ShapeDtypeStruct((M, KV_LORA), F32), jax.ShapeDtypeStruct((M, ROPE_DIM), F32),
                   jax.ShapeDtypeStruct((M, KV_LORA + 128), BF16)),
        grid=(M // tm,),
        in_specs=[pl.BlockSpec((tm, KV_LORA + 128), lambda i: (i, 0)),
                  pl.BlockSpec((1, KV_LORA), lambda i: (0, 0)),
                  pl.BlockSpec((tm, 128), lambda i: (i % ncs, 0))],
        out_specs=(pl.BlockSpec((tm, KV_LORA), lambda i: (i, 0)), pl.BlockSpec((tm, ROPE_DIM), lambda i: (i, 0)),
                   pl.BlockSpec((tm, KV_LORA + 128), lambda i: (i, 0))),
        compiler_params=_cparams(1),
        name="latent_post",
    )(h, norm_ckv.reshape(1, KV_LORA).astype(F32), cs)


def _qpost_kernel(q_ref, cs_ref, o_ref):
    cs = cs_ref[...]
    for h in range(H_B):
        base = h * 256
        o_ref[:, base:base + 128] = q_ref[:, base:base + 128].astype(o_ref.dtype)
        o_ref[:, base + 128:base + 256] = _rope128(q_ref[:, base + 128:base + 256], cs).astype(o_ref.dtype)


def _q_post(q, cs, *, T, out_dtype):
    M = q.shape[0]
    tm = _pos_tile(M, T)
    ncs = cs.shape[0] // tm
    return pl.pallas_call(
        _qpost_kernel,
        out_shape=jax.ShapeDtypeStruct((M, H_B * 256), out_dtype),
        grid=(M // tm,),
        in_specs=[pl.BlockSpec((tm, H_B * 256), lambda i: (i, 0)),
                  pl.BlockSpec((tm, 128), lambda i: (i % ncs, 0))],
        out_specs=pl.BlockSpec((tm, H_B * 256), lambda i: (i, 0)),
        compiler_params=_cparams(1),
        name="q_post",
    )(q, cs)


def _flash_kernel(q_ref, k_ref, v_ref, o_ref, m_scr, l_scr, acc_scr, *, tq, tk):
    i = pl.program_id(1)
    j = pl.program_id(2)

    @pl.when(j == 0)
    def _():
        m_scr[...] = jnp.full(m_scr.shape, -jnp.inf, F32)
        l_scr[...] = jnp.zeros(l_scr.shape, F32)
        acc_scr[...] = jnp.zeros(acc_scr.shape, F32)

    hs = range(H_B)

    def block(masked):
        c = ATTN_SCALE * math.log2(math.e)
        ss = [_dg(q_ref[:, h * 256:(h + 1) * 256], k_ref[:, h * 256:(h + 1) * 256], NT) for h in hs]
        if masked:
            keep = lax.broadcasted_iota(jnp.int32, (tq, tk), 1) <= lax.broadcasted_iota(jnp.int32, (tq, tk), 0)
            ss = [jnp.where(keep, s, -jnp.inf) for s in ss]
        nkt = tk // 128
        tiles = [[ss[h][:, t * 128:(t + 1) * 128] for t in range(nkt)] for h in hs]
        m_prev = [m_scr[h] for h in hs]
        m_new = []
        for h in hs:
            tmax = tiles[h][0]
            for t in range(1, nkt):
                tmax = jnp.maximum(tmax, tiles[h][t])
            m_new.append(jnp.maximum(m_prev[h], jnp.max(tmax, axis=-1, keepdims=True)))
        alpha = [jnp.exp2((m_prev[h] - m_new[h]) * c) for h in hs]
        ps = [[jnp.exp2((tiles[h][t] - m_new[h]) * c) for t in range(nkt)] for h in hs]
        pv = [_dg(jnp.concatenate([p.astype(BF16) for p in ps[h]], axis=1), v_ref[:, h * V_DIM:(h + 1) * V_DIM])
              for h in hs]
        for h in hs:
            psum = ps[h][0]
            for t in range(1, nkt):
                psum = psum + ps[h][t]
            l_scr[h] = alpha[h] * l_scr[h] + jnp.sum(psum, axis=-1, keepdims=True)
            acc_scr[:, h * V_DIM:(h + 1) * V_DIM] = alpha[h] * acc_scr[:, h * V_DIM:(h + 1) * V_DIM] + pv[h]
            m_scr[h] = m_new[h]

    @pl.when(j < i)
    def _():
        block(False)

    @pl.when(j == i)
    def _():
        block(True)
        for h in hs:
            sl = slice(h * V_DIM, (h + 1) * V_DIM)
            o_ref[:, sl] = (acc_scr[:, sl] / l_scr[h]).astype(o_ref.dtype)


def _flash_causal(q, kv, *, nb, T):
    M = q.shape[0]
    tq = tk = 512
    nq = T // tq
    kw = H_B * 256
    vw = H_B * V_DIM
    return pl.pallas_call(
        functools.partial(_flash_kernel, tq=tq, tk=tk),
        out_shape=jax.ShapeDtypeStruct((M, vw), BF16),
        grid=(nb, nq, nq),
        in_specs=[pl.BlockSpec((tq, kw), lambda b, i, j: (b * nq + i, 0)),
                  pl.BlockSpec((tk, kw), lambda b, i, j: (b * nq + jnp.minimum(i, j), 0)),
                  pl.BlockSpec((tk, vw), lambda b, i, j: (b * nq + jnp.minimum(i, j), kw // vw))],
        out_specs=pl.BlockSpec((tq, vw), lambda b, i, j: (b * nq + i, 0)),
        scratch_shapes=[pltpu.VMEM((H_B, tq, 128), F32), pltpu.VMEM((H_B, tq, 128), F32), pltpu.VMEM((tq, vw), F32)],
        compiler_params=_cparams(3),
        name="mla_causal",
    )(q, kv, kv)


def _decode_kernel(pt_ref, q_ref, ckn_ref, kpn_ref, wuv_ref, ck_hbm, kp_hbm, o_ref,
                   ck_buf, kp_buf, sem, m_scr, l_scr, acc_scr, *, PP, n_steps, n_total, TS):
    j = pl.program_id(1)
    n = pl.program_id(0) * n_steps + j
    slot = lax.rem(n, 2)
    R = q_ref.shape[1]

    def group_copies(step, slot_, for_wait):
        cps = []
        for p in range(PP):
            page = 0 if for_wait else pt_ref[step * PP + p]
            cps.append(pltpu.make_async_copy(ck_hbm.at[page], ck_buf.at[slot_, p], sem.at[slot_, 0]))
            cps.append(pltpu.make_async_copy(kp_hbm.at[page], kp_buf.at[slot_, p], sem.at[slot_, 1]))
        return cps

    @pl.when(n == 0)
    def _():
        for cp in group_copies(0, 0, False):
            cp.start()

    nxt = jnp.minimum(n + 1, n_total - 1)
    for cp in group_copies(nxt, 1 - slot, False):
        cp.start()
    for cp in group_copies(n, slot, True):
        cp.wait()

    @pl.when(j == 0)
    def _():
        m_scr[...] = jnp.full(m_scr.shape, -jnp.inf, F32)
        l_scr[...] = jnp.zeros(l_scr.shape, F32)
        acc_scr[...] = jnp.zeros(acc_scr.shape, F32)

    q_lat = q_ref[0, :, :KV_LORA]
    q_pe = q_ref[0, :, KV_LORA:]
    c = ATTN_SCALE * math.log2(math.e)

    cks = [ck_buf[slot, p].astype(BF16) for p in range(PP)]
    kps = [kp_buf[slot, p].astype(BF16) for p in range(PP)]
    ss = [_dg(q_lat, cks[p], NT) + _dg(q_pe, kps[p]) for p in range(PP)]
    pad = PAGE_SIZE - TS
    ckn = jnp.concatenate([ckn_ref[0], jnp.zeros((pad, KV_LORA), F32)], axis=0).astype(BF16)
    kpn = jnp.concatenate([kpn_ref[0], jnp.zeros((pad, ROPE_DIM), F32)], axis=0).astype(BF16)
    t_q = lax.broadcasted_iota(jnp.int32, (R, PAGE_SIZE), 0) % TS
    col = lax.broadcasted_iota(jnp.int32, (R, PAGE_SIZE), 1)
    last_shift = jnp.where(j == n_steps - 1, 0, -PAGE_SIZE)
    s_new = jnp.where(col <= t_q + last_shift, _dg(q_lat, ckn, NT) + _dg(q_pe, kpn, NT), -jnp.inf)
    ss.append(s_new)
    cks.append(ckn)

    tile_max = ss[0]
    for s in ss[1:]:
        tile_max = jnp.maximum(tile_max, s)
    m_prev = m_scr[...]
    m_new = jnp.maximum(m_prev, jnp.max(tile_max, axis=-1, keepdims=True))
    alpha = jnp.exp2((m_prev - m_new) * c)
    ps = [jnp.exp2((s - m_new) * c) for s in ss]
    p_sum = ps[0]
    for p in ps[1:]:
        p_sum = p_sum + p
    acc = jnp.concatenate([alpha, alpha], axis=1) * acc_scr[...]
    for p, vv in zip(ps, cks):
        acc = acc + _dg(p.astype(BF16), vv)
    m_scr[...] = m_new
    l_scr[...] = alpha * l_scr[...] + jnp.sum(p_sum, axis=-1, keepdims=True)
    acc_scr[...] = acc

    @pl.when(j == n_steps - 1)
    def _():
        o_lat = acc_scr[...] / jnp.concatenate([l_scr[...], l_scr[...]], axis=1)
        for h in range(H_B):
            o_h = o_lat[h * TS:(h + 1) * TS, :].astype(BF16)
            o_ref[0, :, h * V_DIM:(h + 1) * V_DIM] = _dg(o_h, wuv_ref[h]).astype(o_ref.dtype)

    @pl.when(n == n_total - 1)
    def _():
        for cp in group_copies(n, 1 - slot, True):
            cp.wait()


def _decode_attn(page_table, qd, cache_ckv, cache_kpe, ckv_new, kpe_new, w_uv, *, PP=32):
    nb, n_pages = page_table.shape
    TS = ckv_new.shape[1]
    R = qd.shape[1]
    n_steps = n_pages // PP
    pt = page_table.reshape(-1).astype(jnp.int32)
    kpe_t = jnp.swapaxes(cache_kpe, 1, 2)
    grid_spec = pltpu.PrefetchScalarGridSpec(
        num_scalar_prefetch=1,
        grid=(nb, n_steps),
        in_specs=[pl.BlockSpec((1, R, KV_LORA + ROPE_DIM), lambda b, j, pt_ref: (b, 0, 0)),
                  pl.BlockSpec((1, TS, KV_LORA), lambda b, j, pt_ref: (b, 0, 0)),
                  pl.BlockSpec((1, TS, ROPE_DIM), lambda b, j, pt_ref: (b, 0, 0)),
                  pl.BlockSpec((H_B, KV_LORA, V_DIM), lambda b, j, pt_ref: (0, 0, 0)),
                  pl.BlockSpec(memory_space=pl.ANY),
                  pl.BlockSpec(memory_space=pl.ANY)],
        out_specs=pl.BlockSpec((1, TS, H_B * V_DIM), lambda b, j, pt_ref: (b, 0, 0)),
        scratch_shapes=[pltpu.VMEM((2, PP, PAGE_SIZE, KV_LORA), F32),
                        pltpu.VMEM((2, PP, ROPE_DIM, PAGE_SIZE), F32),
                        pltpu.SemaphoreType.DMA((2, 2)),
                        pltpu.VMEM((R, 128), F32), pltpu.VMEM((R, 128), F32), pltpu.VMEM((R, KV_LORA), F32)],
    )
    return pl.pallas_call(
        functools.partial(_decode_kernel, PP=PP, n_steps=n_steps, n_total=nb * n_steps, TS=TS),
        out_shape=jax.ShapeDtypeStruct((nb, TS, H_B * V_DIM), BF16),
        grid_spec=grid_spec,
        compiler_params=_cparams(2),
        name="mla_decode",
    )(pt, qd, ckv_new, kpe_new, w_uv, cache_ckv, kpe_t)


def _rope_table(pos):
    half = ROPE_DIM // 2
    inv = ROPE_BASE ** (-jnp.arange(half, dtype=F32) / half)
    ang = pos.astype(F32)[:, None] * inv[None, :]
    cos, sin = jnp.cos(ang), jnp.sin(ang)
    return jnp.concatenate([cos, cos, -sin, sin], axis=-1)


def _swap_halves(w):
    half = ROPE_DIM // 2
    return jnp.concatenate([w[..., half:], w[..., :half]], axis=-1)


def _prep_rwkv_weights(l, P):
    w_in = P["w_in_a"][l]
    mu = P["mu_a"][l]
    cuts = [0, MIX_A, MIX_A + DECAY_LORA, 2 * MIX_A + DECAY_LORA, 3 * MIX_A + DECAY_LORA,
            3 * MIX_A + DECAY_LORA + A_LORA, C_RWKV, C_A]
    seg = lambda a, i: a[..., cuts[i]:cuts[i + 1]]
    r_w, wl_w, k_w, v_w, al_w, gl_w, qm_w = [seg(w_in, i) for i in range(7)]
    r_m, wl_m, k_m, v_m, al_m, gl_m = [seg(mu, i) for i in range(6)]
    zc = lambda n: jnp.zeros((D_MODEL, n), F32)
    zm = lambda n: jnp.zeros((n,), F32)
    if l > 0:
        vl_w, vl_m = P["w_vres_in"][l - 1], P["mu_vres"][l - 1]
        wv_up, v0 = P["w_vres_up"][l - 1], P["v0"][l - 1]
    else:
        vl_w, vl_m = zc(VRES_LORA), zm(VRES_LORA)
        wv_up, v0 = jnp.zeros((VRES_LORA, MIX_A), F32), zm(MIX_A)
    w_full = jnp.concatenate([r_w, k_w, v_w, wl_w, al_w, gl_w, vl_w, zc(64), qm_w, zc(128)], axis=1)
    mu_full = jnp.concatenate([r_m, k_m, v_m, wl_m, al_m, gl_m, vl_m, zm(64), zm(MEM_WIDTH), zm(128)])
    zr = lambda n: jnp.zeros((n, MIX_A), F32)
    return dict(
        w_in=w_full.astype(BF16),
        mu=mu_full.reshape(1, PA_COLS),
        wd=jnp.concatenate([P["w_decay_up"][l], zr(A_LORA)], axis=0).astype(BF16),
        wa=jnp.concatenate([zr(DECAY_LORA), P["w_a_up"][l]], axis=0).astype(BF16),
        wg=jnp.concatenate([P["w_g_up"][l], zr(256 - GATE_LORA)], axis=0).astype(BF16),
        wv=jnp.concatenate([zr(GATE_LORA), wv_up, zr(256 - GATE_LORA - VRES_LORA)], axis=0).astype(BF16),
        w0=P["w0"][l].reshape(1, MIX_A), a0=P["a0"][l].reshape(1, MIX_A), v0=v0.reshape(1, MIX_A),
        k_k=P["k_k"][l].reshape(1, MIX_A), k_a=P["k_a"][l].reshape(1, MIX_A),
    )


def _prep_mla_weights(P):
    w_kv_a = P["w_kv_a"]
    w_kva = jnp.concatenate([w_kv_a, _swap_halves(w_kv_a[:, KV_LORA:])], axis=1).astype(BF16)
    w_kv_b = P["w_kv_b"]
    w_uk, w_uv = w_kv_b[..., :NOPE_DIM], w_kv_b[..., NOPE_DIM:]
    k_part = jnp.zeros((KV_LORA + 128, H_B, 256), F32)
    k_part = k_part.at[:KV_LORA, :, :NOPE_DIM].set(w_uk)
    eye = jnp.eye(ROPE_DIM, dtype=F32)
    k_part = k_part.at[KV_LORA:KV_LORA + ROPE_DIM, :, NOPE_DIM:NOPE_DIM + ROPE_DIM].set(
        jnp.broadcast_to(eye[:, None, :], (ROPE_DIM, H_B, ROPE_DIM)))
    v_part = jnp.zeros((KV_LORA + 128, H_B, V_DIM), F32).at[:KV_LORA].set(w_uv)
    w_kvx = jnp.concatenate([k_part.reshape(KV_LORA + 128, H_B * 256), v_part.reshape(KV_LORA + 128, H_B * V_DIM)],
                            axis=1).astype(BF16)
    w_q = []
    for j in range(N_B):
        wq = P["w_q_b"][j].reshape(Q_LORA, H_B, NOPE_DIM + ROPE_DIM)
        rope_w = wq[..., NOPE_DIM:]
        w_q.append(jnp.concatenate([wq[..., :NOPE_DIM], rope_w, _swap_halves(rope_w)], axis=-1)
                   .reshape(Q_LORA, H_B * 256).astype(BF16))
    return dict(
        w_kva=w_kva, w_kvx=w_kvx, w_q=w_q,
        w_ukT=jnp.transpose(w_uk, (1, 2, 0)).astype(BF16),
        w_uv=jnp.transpose(w_uv, (1, 0, 2)).astype(BF16),
    )


def _trunk(x, pos, mem_k, mem_v, wkv0, shift0, conv0, past, P, W, *, nseq, T):
    M = nseq * T
    x = x.reshape(M, D_MODEL)
    cs = _rope_table(pos)
    if T < ROW_TILES[0]:
        cs = jnp.tile(cs, (nseq, 1))
    chunk = 64 if T >= 64 else T
    new_wkv, new_shift, new_conv = [], [], []
    v_first = None
    ckv = kpe = ckpe = kvx = None
    for l in range(DEPTH):
        g_mix = P["norm_mix"][l]
        if l < N_A:
            pw = W["rwkv"][l]
            x_last = x.reshape(nseq, T, D_MODEL)[:, -1]
            new_shift.append(_rmsnorm(x_last, g_mix, name="shift_norm"))
            sp = _mm([shift0[l].astype(F32)], [pw["w_in"]], name="rwkv_in_shift")
            r, lw, k, v, kn, b, g, q_mem = _rwkv_prep(x, sp, pw, v_first if l > 0 else None, g_mix, nseq=nseq, T=T)
            if l == 0:
                v_first = v
            h0 = wkv0[l].astype(F32)
            o_tok, h_new = _wkv(r, lw, k, v, kn, b, g, P["r_k"][l], P["lnx_w"][l], P["lnx_b"][l], h0,
                                nseq=nseq, T=T, C=chunk)
            new_wkv.append(h_new)
        else:
            j = l - N_A
            if l == N_A:
                hkv = _mm([x], [W["mla"]["w_kva"]], g=P["norm_kv"], name="kv_a")
                ckv, kpe, ckpe = _latent_post(hkv, P["norm_ckv"], cs, T=T)
                if past is None:
                    kvx = _mm([ckpe], [W["mla"]["w_kvx"]], out_dtype=BF16, name="kv_expand")
            proj = _mm([x], [W["w_in_b"][j]], g=g_mix, name="mla_in")
            qf = _mm([proj], [W["mla"]["w_q"][j]], g=P["norm_q"][j], name="q_b")
            q_mem = proj[:, Q_LORA:].astype(BF16)
            if past is None:
                q = _q_post(qf, cs, T=T, out_dtype=BF16)
                o_tok = _flash_causal(q, kvx, nb=nseq, T=T)
            else:
                q = _q_post(qf, cs, T=T, out_dtype=BF16).reshape(M, H_B, 256)
                q_lat = jnp.stack([_mm([q[:, h, :NOPE_DIM]], [W["mla"]["w_ukT"][h]], out_dtype=BF16, name="q_absorb")
                                   for h in range(H_B)], axis=1)
                qd = jnp.concatenate([q_lat, q[:, :, NOPE_DIM:NOPE_DIM + ROPE_DIM]], axis=-1)
                qd = qd.reshape(nseq, T, H_B, KV_LORA + ROPE_DIM).transpose(0, 2, 1, 3).reshape(nseq, H_B * T, -1)
                o_tok = _decode_attn(past[2], qd, past[0], past[1], ckv.reshape(nseq, T, KV_LORA),
                                     kpe.reshape(nseq, T, ROPE_DIM), W["mla"]["w_uv"]).reshape(M, H_B * V_DIM)
        o_mem = _mem_attn(q_mem, mem_k, mem_v, l, nb=nseq, T=T)
        x, c_new, y = _ffn(x, o_tok, o_mem, W["w_o_tok"][l], W["w_o_mem"][l], P["norm_ffn"][l], W["w_gate"][l],
                           W["w_val"][l], P["conv_w"][l], P["conv_b"][l], W["w_down"][l], conv0[l],
                           P["final_norm"] if l == DEPTH - 1 else None, nseq=nseq, T=T)
        new_conv.append(c_new)
    return (y.reshape(nseq, T, D_MODEL), ckv.reshape(nseq, T, KV_LORA), kpe.reshape(nseq, T, ROPE_DIM),
            jnp.stack(new_wkv), jnp.stack(new_shift), jnp.stack(new_conv))


def kernel(x_prompt, x_sample, cache_ckv, cache_kpe, cache_mem_k, cache_mem_v, state_wkv, state_shift, state_conv, page_table, mem_prompt, norm_mix, norm_ffn, norm_mem, w_mem_kv, w_o, w_in_a, mu_a, w_vres_in, mu_vres, w_decay_up, w0, w_a_up, a0, w_g_up, w_vres_up, v0, k_k, k_a, r_k, lnx_w, lnx_b, norm_kv, w_kv_a, norm_ckv, w_kv_b, w_in_b, norm_q, w_q_b, w_ffn_up, conv_w, conv_b, w_ffn_down, final_norm):
    P = dict(norm_mix=norm_mix, norm_ffn=norm_ffn, w_o=w_o, w_in_a=w_in_a, mu_a=mu_a, w_vres_in=w_vres_in,
             mu_vres=mu_vres, w_decay_up=w_decay_up, w0=w0, w_a_up=w_a_up, a0=a0, w_g_up=w_g_up,
             w_vres_up=w_vres_up, v0=v0, k_k=k_k, k_a=k_a, r_k=r_k, lnx_w=lnx_w, lnx_b=lnx_b,
             norm_kv=norm_kv, w_kv_a=w_kv_a, norm_ckv=norm_ckv, w_kv_b=w_kv_b, w_in_b=w_in_b, norm_q=norm_q,
             w_q_b=w_q_b, conv_w=conv_w, conv_b=conv_b, final_norm=final_norm)
    W = dict(
        rwkv=[_prep_rwkv_weights(l, P) for l in range(N_A)],
        mla=_prep_mla_weights(P),
        w_in_b=[w_in_b[j].astype(BF16) for j in range(N_B)],
        w_o_tok=[w_o[l, :MIX_A].astype(BF16) for l in range(DEPTH)],
        w_o_mem=[w_o[l, MIX_A:].astype(BF16) for l in range(DEPTH)],
        w_gate=[w_ffn_up[l, :, :D_FF].astype(BF16) for l in range(DEPTH)],
        w_val=[w_ffn_up[l, :, D_FF:].astype(BF16) for l in range(DEPTH)],
        w_down=[w_ffn_down[l].astype(BF16) for l in range(DEPTH)],
    )
    dt = x_prompt.dtype
    B, T = x_prompt.shape[:2]
    n_mem = mem_prompt.shape[1]
    mem_k_all, mem_v_all = _mem_kv(mem_prompt, norm_mem, w_mem_kv)
    y_p, ckv_p, kpe_p, wkv_p, shift_p, conv_p = _trunk(
        x_prompt, jnp.arange(T), mem_k_all, mem_v_all,
        jnp.zeros((N_A, B, H_A, HEAD_A, HEAD_A), dt), jnp.zeros((N_A, B, D_MODEL), dt),
        jnp.zeros((DEPTH, B, CONV_W - 1, D_FF), dt), None, P, W, nseq=B, T=T)
    DB, TS = x_sample.shape[:2]
    past_len = page_table.shape[1] * PAGE_SIZE
    to_t = lambda c: jnp.transpose(c, (0, 1, 3, 4, 2)).reshape(DEPTH, DB, MEM_WIDTH, n_mem)
    y_s, ckv_s, kpe_s, wkv_s, shift_s, conv_s = _trunk(
        x_sample, past_len + jnp.arange(TS), to_t(cache_mem_k), to_t(cache_mem_v), state_wkv, state_shift, state_conv,
        (cache_ckv, cache_kpe, page_table), P, W, nseq=DB, T=TS)
    from_t = lambda a: jnp.transpose(a.reshape(DEPTH, B, MEM_HEADS, MEM_HEAD_DIM, n_mem), (0, 1, 4, 2, 3))
    mem_k_out = from_t(mem_k_all)
    mem_v_out = from_t(mem_v_all)
    return (y_p, y_s, ckv_p, kpe_p, mem_k_out, mem_v_out, wkv_p, shift_p, conv_p,
            ckv_s, kpe_s, wkv_s, shift_s, conv_s)
```

```python
import functools
import math

import jax
import jax.numpy as jnp
from jax import lax
from jax.experimental import pallas as pl
from jax.experimental.pallas import tpu as pltpu

F32 = jnp.float32
BF16 = jnp.bfloat16

D_MODEL = 1024
DEPTH = 4
N_A = 2
N_B = 2
MEM_WIDTH = 256
MIX_A = 768
HEAD_A = 64
H_A = 12
DECAY_LORA = 64
A_LORA = 64
VRES_LORA = 32
GATE_LORA = 160
GN_EPS = 64e-5
C_RWKV = 3 * MIX_A + DECAY_LORA + A_LORA + GATE_LORA
C_A = C_RWKV + MEM_WIDTH
MEM_HEADS = 4
MEM_HEAD_DIM = 64
MEM_SCALE = MEM_HEAD_DIM ** -0.5
NOPE_DIM = 128
ROPE_DIM = 64
V_DIM = 128
H_B = 6
Q_LORA = 256
KV_LORA = 256
ROPE_BASE = 10000.0
ATTN_SCALE = (NOPE_DIM + ROPE_DIM) ** -0.5
D_FF = 2816
CONV_W = 3
RMS_EPS = 1e-6
PAGE_SIZE = 128

PA_R, PA_K, PA_V = 0, 768, 1536
PA_LORA = 2304
PA_GATE = 2432
PA_QMEM = 2688
PA_COLS = 3072

VMEM_LIMIT_BYTES = 52 * 1024 * 1024

NN = ((1,), (0,))
NT = ((1,), (1,))
TN = ((0,), (0,))


def _dg(a, b, dims=NN):
    return lax.dot_general(a, b, (dims, ((), ())), preferred_element_type=F32)


def _split(x):
    hi = x.astype(BF16)
    lo = (x - hi.astype(F32)).astype(BF16)
    return hi, lo


def _dot3(a, b, dims=NN):
    ah, al = _split(a)
    bh, bl = _split(b)
    return _dg(ah, bh, dims) + (_dg(ah, bl, dims) + _dg(al, bh, dims))


def _transpose_exact(x, eye_bf16):
    hi = x.astype(BF16)
    r1 = x - hi.astype(F32)
    mid = r1.astype(BF16)
    lo = (r1 - mid.astype(F32)).astype(BF16)
    return (_dg(eye_bf16, hi, NT) + _dg(eye_bf16, mid, NT)) + _dg(eye_bf16, lo, NT)


def _cparams(n_axes):
    return pltpu.CompilerParams(dimension_semantics=("arbitrary",) * n_axes,
                                vmem_limit_bytes=VMEM_LIMIT_BYTES)


def _rms(x, g, eps=RMS_EPS):
    return x * lax.rsqrt(jnp.mean(x * x, axis=-1, keepdims=True) + eps) * g


def _pick_tile(n, candidates):
    for c in candidates:
        if n % c == 0:
            return c
    return n


ROW_TILES = (512, 256, 128, 64, 32, 16, 8)


def _pos_tile(M, T):
    return _pick_tile(T, ROW_TILES) if T >= ROW_TILES[0] else _pick_tile(M, ROW_TILES)


def _mm_kernel(*refs, n_in, has_norm, has_res, f32_in):
    xs = refs[:n_in]
    ws = refs[n_in:2 * n_in]
    k = 2 * n_in
    g_ref = refs[k] if has_norm else None
    k += int(has_norm)
    r_ref = refs[k] if has_res else None
    k += int(has_res)
    o_ref = refs[k]
    scr = list(refs[k + 1:])

    lhs = []
    si = 0
    for idx in range(n_in):
        if f32_in[idx]:
            s_ref = scr[si]
            si += 1

            @pl.when(pl.program_id(1) == 0)
            def _(x_ref=xs[idx], s_ref=s_ref, idx=idx):
                x = x_ref[...]
                if has_norm and idx == 0:
                    x = _rms(x, g_ref[...])
                s_ref[...] = x.astype(BF16)

            lhs.append(s_ref)
        else:
            lhs.append(xs[idx])
    acc = _dg(lhs[0][...], ws[0][...])
    for idx in range(1, n_in):
        acc = acc + _dg(lhs[idx][...], ws[idx][...])
    if has_res:
        acc = acc + r_ref[...]
    o_ref[...] = acc.astype(o_ref.dtype)


def _mm(xs, ws, *, g=None, res=None, out_dtype=F32, x_col_blocks=None, ks=None, name="mm"):
    n_in = len(xs)
    M = xs[0].shape[0]
    N = ws[0].shape[1]
    ks = [w.shape[0] for w in ws]
    x_col_blocks = x_col_blocks or [0] * n_in
    tm = _pick_tile(M, (512, 256, 128, 64, 32, 16, 8))
    tn = _pick_tile(N, (1024, 768, 512, 384, 256, 128))
    f32_in = tuple(x.dtype == F32 for x in xs)
    in_specs = []
    for x, kk, cb in zip(xs, ks, x_col_blocks):
        in_specs.append(pl.BlockSpec((tm, kk), lambda i, j, cb=cb: (i, cb)))
    for w, kk in zip(ws, ks):
        in_specs.append(pl.BlockSpec((kk, tn), lambda i, j: (0, j)))
    args = list(xs) + list(ws)
    if g is not None:
        in_specs.append(pl.BlockSpec((1, ks[0]), lambda i, j: (0, 0)))
        args.append(g.reshape(1, ks[0]).astype(F32))
    if res is not None:
        in_specs.append(pl.BlockSpec((tm, tn), lambda i, j: (i, j)))
        args.append(res)
    scratch = [pltpu.VMEM((tm, kk), BF16) for kk, f in zip(ks, f32_in) if f]
    return pl.pallas_call(
        functools.partial(_mm_kernel, n_in=n_in, has_norm=g is not None, has_res=res is not None,
                          f32_in=f32_in),
        out_shape=jax.ShapeDtypeStruct((M, N), out_dtype),
        grid=(M // tm, N // tn),
        in_specs=in_specs,
        out_specs=pl.BlockSpec((tm, tn), lambda i, j: (i, j)),
        scratch_shapes=scratch,
        compiler_params=_cparams(2),
        name=name,
    )(*args)


def _rmsnorm_kernel(x_ref, g_ref, o_ref):
    o_ref[...] = _rms(x_ref[...], g_ref[...])


def _rmsnorm(x, g, name="rmsnorm"):
    M, D = x.shape
    tm = _pick_tile(M, (512, 256, 128, 64, 32, 16, 8))
    return pl.pallas_call(
        _rmsnorm_kernel,
        out_shape=jax.ShapeDtypeStruct((M, D), F32),
        grid=(M // tm,),
        in_specs=[pl.BlockSpec((tm, D), lambda i: (i, 0)), pl.BlockSpec((1, D), lambda i: (0, 0))],
        out_specs=pl.BlockSpec((tm, D), lambda i: (i, 0)),
        compiler_params=_cparams(1),
        name=name,
    )(x, g.reshape(1, D).astype(F32))


def _mem_kv_kernel(x_ref, g_ref, wt_ref, k_ref, v_ref):
    kvt = _dg(wt_ref[0], _rms(x_ref[0], g_ref[0]).astype(BF16), NT)
    k_ref[0, 0] = kvt[:MEM_WIDTH]
    v_ref[0, 0] = kvt[MEM_WIDTH:]


def _mem_kv(mem, norm_mem, w_mem_kv):
    B, n_mem, _ = mem.shape
    out = jax.ShapeDtypeStruct((DEPTH, B, MEM_WIDTH, n_mem), F32)
    wt = jnp.swapaxes(w_mem_kv, 1, 2).astype(BF16)
    return pl.pallas_call(
        _mem_kv_kernel,
        out_shape=(out, out),
        grid=(DEPTH, B),
        in_specs=[pl.BlockSpec((1, n_mem, D_MODEL), lambda l, b: (b, 0, 0)),
                  pl.BlockSpec((1, 1, D_MODEL), lambda l, b: (l, 0, 0)),
                  pl.BlockSpec((1, 2 * MEM_WIDTH, D_MODEL), lambda l, b: (l, 0, 0))],
        out_specs=(pl.BlockSpec((1, 1, MEM_WIDTH, n_mem), lambda l, b: (l, b, 0, 0)),
                   pl.BlockSpec((1, 1, MEM_WIDTH, n_mem), lambda l, b: (l, b, 0, 0))),
        compiler_params=_cparams(2),
        name="mem_kv",
    )(mem, norm_mem.reshape(DEPTH, 1, D_MODEL).astype(F32), wt)


def _segsum64(x):
    r = lax.broadcasted_iota(jnp.int32, (128, 128), 0) >> 6
    c = lax.broadcasted_iota(jnp.int32, (128, 128), 1) >> 6
    ones = jnp.where(r == c, 1.0, 0.0).astype(BF16)
    outs = []
    for j in range(x.shape[1] // 128):
        hi, lo = _split(x[:, j * 128:(j + 1) * 128])
        outs.append(_dg(hi, ones) + _dg(lo, ones))
    return jnp.concatenate(outs, axis=1)


def _softplus(x):
    return jnp.maximum(x, 0.0) + jnp.log1p(jnp.exp(-jnp.abs(x)))


def _prep_kernel(*refs, has_vres, blocks_per_seq):
    if blocks_per_seq:
        x_ref, gmix_ref, win_ref = refs[:3]
        refs = refs[2:]
    (cur_ref, prev_ref, mu_ref, wd_ref, wa_ref, wg_ref, wv_ref, w0_ref, a0_ref, v0_ref, kk_ref, ka_ref) = refs[:12]
    k = 12
    vf_ref = refs[k] if has_vres else None
    k += int(has_vres)
    r_o, lw_o, k_o, v_o, kn_o, b_o, g_o, qm_o = refs[k:k + 8]

    if blocks_per_seq:
        cur_ref, carry_scr = refs[k + 8], refs[k + 9]
        xn = _rms(x_ref[...], gmix_ref[...]).astype(BF16)
        for lo in range(0, PA_COLS, 512):
            cur_ref[:, lo:lo + 512] = _dg(xn, win_ref[:, lo:lo + 512])
    tm = cur_ref.shape[0]

    if blocks_per_seq:

        @pl.when(pl.program_id(0) % blocks_per_seq == 0)
        def _():
            carry_scr[...] = prev_ref[0]

        first_row = lax.broadcasted_iota(jnp.int32, (tm, 1), 0) == 0

    def mix(lo, hi):
        c = cur_ref[:, lo:hi]
        if blocks_per_seq:
            p = jnp.where(first_row, carry_scr[:, lo:hi], pltpu.roll(c, 1, 0))
        else:
            p = prev_ref[:, lo:hi]
        return c + (p - c) * mu_ref[:, lo:hi]

    lora = mix(PA_LORA, PA_LORA + 128)
    zw = _dg(jnp.tanh(lora).astype(BF16), wd_ref[...]) + w0_ref[...]
    w_log = -_softplus(-zw) - 0.5
    lw_o[...] = -jnp.exp(w_log)
    a = jax.nn.sigmoid(_dg(lora.astype(BF16), wa_ref[...]) + a0_ref[...])
    gin = mix(PA_GATE, PA_GATE + 256)
    g_o[...] = _dg(jax.nn.sigmoid(gin).astype(BF16), wg_ref[...])
    r_o[...] = mix(PA_R, PA_R + MIX_A)
    kx = mix(PA_K, PA_K + MIX_A)
    v = mix(PA_V, PA_V + MIX_A)
    if has_vres:
        sv = jax.nn.sigmoid(_dg(gin.astype(BF16), wv_ref[...]) + v0_ref[...])
        v = v + (vf_ref[...] - v) * sv
    v_o[...] = v
    kk = kx * kk_ref[...]
    kk = kk * lax.rsqrt(jnp.maximum(_segsum64(kk * kk), 1e-24))
    kn_o[...] = kk
    b_o[...] = kk * a
    k_o[...] = kx * (1.0 + (a - 1.0) * ka_ref[...])
    qm_o[...] = cur_ref[:, PA_QMEM:PA_QMEM + MEM_WIDTH].astype(BF16)
    if blocks_per_seq:
        carry_scr[...] = cur_ref[tm - 1:tm, :]


def _rwkv_prep(x, sp, pw, v_first, g_mix, *, nseq, T):
    M = x.shape[0]
    has_vres = v_first is not None
    row = lambda w: pl.BlockSpec((tm, w), lambda i: (i, 0))
    full = lambda a: pl.BlockSpec(a.shape, lambda i: (0,) * a.ndim)
    if T >= 256:
        tm = 256
        blocks_per_seq = T // tm
        g2 = g_mix.reshape(1, D_MODEL).astype(F32)
        lead_args = [x, g2, pw["w_in"], sp.reshape(nseq, 1, PA_COLS)]
        lead_specs = [row(D_MODEL), full(g2),
                      pl.BlockSpec(pw["w_in"].shape, lambda i: (0, 0), pipeline_mode=pl.Buffered(1)),
                      pl.BlockSpec((1, 1, PA_COLS), lambda i: (i // blocks_per_seq, 0, 0))]
        scratch = [pltpu.VMEM((tm, PA_COLS), F32), pltpu.VMEM((1, PA_COLS), F32)]
    else:
        tm = _pick_tile(M, (256, 128, 64, 32, 16, 8))
        blocks_per_seq = 0
        cur = _mm([x], [pw["w_in"]], g=g_mix, name="rwkv_in")
        prev = jnp.concatenate([sp[:, None, :], cur.reshape(nseq, T, PA_COLS)[:, :-1]], axis=1).reshape(M, PA_COLS)
        lead_args = [cur, prev]
        lead_specs = [row(PA_COLS), row(PA_COLS)]
        scratch = []
    consts = [pw["mu"], pw["wd"], pw["wa"], pw["wg"], pw["wv"], pw["w0"], pw["a0"], pw["v0"], pw["k_k"], pw["k_a"]]
    args = lead_args + consts
    in_specs = lead_specs + [full(a) for a in consts]
    if has_vres:
        args.append(v_first)
        in_specs.append(row(MIX_A))
    outs = [jax.ShapeDtypeStruct((M, MIX_A), F32)] * 7 + [jax.ShapeDtypeStruct((M, MEM_WIDTH), BF16)]
    out_specs = [row(MIX_A)] * 7 + [row(MEM_WIDTH)]
    return pl.pallas_call(
        functools.partial(_prep_kernel, has_vres=has_vres, blocks_per_seq=blocks_per_seq),
        out_shape=outs,
        grid=(M // tm,),
        in_specs=in_specs,
        out_specs=out_specs,
        scratch_shapes=scratch,
        compiler_params=_cparams(1),
        name="rwkv_prep",
    )(*args)


def _wkv_kernel(r_ref, lw_ref, k_ref, v_ref, kn_ref, b_ref, g_ref, rk_ref, lnw_ref, lnb_ref, h0_ref,
                o_ref, hT_ref, h_scr, *, C, H, N, nchunks, xpose):
    c_idx = pl.program_id(1)

    eye_bf = jnp.where(lax.broadcasted_iota(jnp.int32, (N, N), 0) == lax.broadcasted_iota(jnp.int32, (N, N), 1),
                       1.0, 0.0).astype(BF16)

    @pl.when(c_idx == 0)
    def _():
        if xpose:
            for h in range(H):
                h_scr[h] = _transpose_exact(h0_ref[0, h], eye_bf)
        else:
            h_scr[...] = h0_ref[0]

    lw = lw_ref[...]
    ti = lax.broadcasted_iota(jnp.int32, (C, C), 0)
    tj = lax.broadcasted_iota(jnp.int32, (C, C), 1)
    ltri = jnp.where(tj <= ti, 1.0, 0.0).astype(BF16)
    lw_hi, lw_lo = _split(lw)
    cum = _dg(ltri, lw_hi) + _dg(ltri, lw_lo)
    cum_last = jnp.sum(lw, axis=0, keepdims=True)
    e_neg = jnp.exp(-cum)
    r_all = r_ref[...]
    k_all = k_ref[...]
    v_all = v_ref[...]
    kt_all = kn_ref[...] * jnp.exp(cum - lw)
    rt_all = r_all * jnp.exp(cum)
    kh_all = k_all * e_neg
    bh_all = b_ref[...] * e_neg
    e_rem = jnp.exp(cum_last - cum)
    kb_all = k_all * e_rem
    bb_all = b_ref[...] * e_rem
    p_last = jnp.exp(cum_last)
    rkr = r_all * k_all * rk_ref[...]

    strict = tj < ti
    incl = tj <= ti
    ni = lax.broadcasted_iota(jnp.int32, (N, N), 0)
    nj = lax.broadcasted_iota(jnp.int32, (N, N), 1)
    eye_n = ni == nj
    eye_c = jnp.where(ti == tj, 1.0, 0.0)

    hs = range(H)
    sls = [slice(h * N, (h + 1) * N) for h in hs]
    cut = lambda a: [a[:, sl].astype(BF16) for sl in sls]
    kt, rt, kh, bh, kb, bb, vv = (cut(a) for a in (kt_all, rt_all, kh_all, bh_all, kb_all, bb_all, v_all))
    a_k = [jnp.where(strict, _dg(kt[h], kh[h], NT), 0.0).astype(BF16) for h in hs]
    a_b = [jnp.where(strict, _dg(kt[h], bh[h], NT), 0.0) for h in hs]
    q_k = [jnp.where(incl, _dg(rt[h], kh[h], NT), 0.0).astype(BF16) for h in hs]
    q_b = [jnp.where(incl, _dg(rt[h], bh[h], NT), 0.0).astype(BF16) for h in hs]
    tinv = [eye_c - jnp.where(((ti >> 1) == (tj >> 1)), a_b[h], 0.0) for h in hs]
    blk = 2
    while blk < C:
        sh = blk.bit_length() - 1
        lvl = ((ti >> (sh + 1)) == (tj >> (sh + 1))) & (((ti >> sh) & 1) == 1) & (((tj >> sh) & 1) == 0)
        t16 = [tinv[h].astype(BF16) for h in hs]
        mt = [_dg(jnp.where(lvl, a_b[h], 0.0).astype(BF16), t16[h]).astype(BF16) for h in hs]
        tinv = [tinv[h] - _dg(t16[h], mt[h]) for h in hs]
        blk *= 2
    t16 = [tinv[h].astype(BF16) for h in hs]
    akv = [_dg(a_k[h], vv[h]).astype(BF16) for h in hs]
    x = [_dg(t16[h], jnp.concatenate([kt[h], akv[h]], axis=1)).astype(BF16) for h in hs]
    qw = [_dg(q_b[h], x[h]) for h in hs]
    r2 = [(rt_all[:, sls[h]] - qw[h][:, :N]).astype(BF16) for h in hs]
    y0 = [_dg(q_k[h], vv[h]) - qw[h][:, N:] for h in hs]
    bw = [_dg(bb[h], x[h], TN) for h in hs]
    gmat = [(jnp.where(eye_n, jnp.broadcast_to(p_last[:, sls[h]], (N, N)), 0.0) - bw[h][:, :N]).astype(BF16) for h in hs]
    h_add = [_dg(kb[h], vv[h], TN) - bw[h][:, N:] for h in hs]
    h0 = [h_scr[h].astype(BF16) for h in hs]
    y = [_dg(r2[h], h0[h]) + y0[h] for h in hs]
    for h in hs:
        h_scr[h] = _dg(gmat[h], h0[h]) + h_add[h]
    for h in hs:
        sl = sls[h]
        mean = jnp.mean(y[h], axis=-1, keepdims=True)
        yc = y[h] - mean
        var = jnp.mean(yc * yc, axis=-1, keepdims=True)
        yn = yc * lax.rsqrt(var + GN_EPS)
        yn = yn * lnw_ref[:, sl] + lnb_ref[:, sl]
        bonus = jnp.sum(rkr[:, sl], axis=-1, keepdims=True) * v_all[:, sl]
        o_ref[:, sl] = ((yn + bonus) * g_ref[:, sl]).astype(o_ref.dtype)

    @pl.when(c_idx == nchunks - 1)
    def _():
        if xpose:
            for h in range(H):
                hT_ref[0, h] = _transpose_exact(h_scr[h], eye_bf)
        else:
            hT_ref[0] = h_scr[...]


def _wkv(r, lw, k, v, kn, b, g, r_k, lnx_w, lnx_b, h0, *, nseq, T, C):
    M = r.shape[0]
    nchunks = T // C
    HN = MIX_A
    xpose = nchunks > 1
    h_in = h0 if xpose else jnp.swapaxes(h0, -1, -2)
    row = pl.BlockSpec((C, HN), lambda s, c: (s * nchunks + c, 0))
    vec = pl.BlockSpec((1, HN), lambda s, c: (0, 0))
    st = pl.BlockSpec((1, H_A, HEAD_A, HEAD_A), lambda s, c: (s, 0, 0, 0))
    o, h_new = pl.pallas_call(
        functools.partial(_wkv_kernel, C=C, H=H_A, N=HEAD_A, nchunks=nchunks, xpose=xpose),
        out_shape=(jax.ShapeDtypeStruct((M, HN), BF16),
                   jax.ShapeDtypeStruct((nseq, H_A, HEAD_A, HEAD_A), F32)),
        grid=(nseq, nchunks),
        in_specs=[row] * 7 + [vec] * 3 + [st],
        out_specs=(row, st),
        scratch_shapes=[pltpu.VMEM((H_A, HEAD_A, HEAD_A), F32)],
        compiler_params=_cparams(2),
        name="wkv7",
    )(r, lw, k, v, kn, b, g, r_k.reshape(1, HN), lnx_w.reshape(1, HN), lnx_b.reshape(1, HN), h_in)
    return o, (h_new if xpose else jnp.swapaxes(h_new, -1, -2))


def _mem_attn_kernel(q_ref, k_ref, v_ref, o_ref, *, nbb, tq):
    c = MEM_SCALE * math.log2(math.e)
    lane_head = lax.broadcasted_iota(jnp.int32, (1, MEM_WIDTH), 1) >> 6
    items = [(bb, h) for bb in range(nbb) for h in range(MEM_HEADS)]
    q_all = q_ref[...].astype(F32)
    qs = [q_all[bb * tq:(bb + 1) * tq, :] for bb in range(nbb)]
    kts = [k_ref[0, bb].astype(BF16) for bb in range(nbb)]
    vts = [v_ref[0, bb].astype(BF16) for bb in range(nbb)]
    ss = [_dg(jnp.where(lane_head == h, qs[bb], 0.0).astype(BF16), kts[bb]) for bb, h in items]
    ms = [jnp.max(s, axis=-1, keepdims=True) for s in ss]
    ps = [jnp.exp2((s - m) * c) for s, m in zip(ss, ms)]
    inv_l = [1.0 / jnp.sum(p, axis=-1, keepdims=True) for p in ps]
    pv = [_dg(p.astype(BF16), vts[bb], NT) for p, (bb, h) in zip(ps, items)]
    outs = []
    for bb in range(nbb):
        o = jnp.zeros((tq, MEM_WIDTH), F32)
        for h in range(MEM_HEADS):
            idx = bb * MEM_HEADS + h
            o = jnp.where(lane_head == h, pv[idx] * inv_l[idx], o)
        outs.append(o)
    o_ref[...] = (outs[0] if nbb == 1 else jnp.concatenate(outs, axis=0)).astype(o_ref.dtype)


def _mem_attn(q, mem_k, mem_v, layer, *, nb, T):
    M = q.shape[0]
    if T >= ROW_TILES[0]:
        tq, nbb = ROW_TILES[0], 1
    else:
        tq, nbb = T, _pick_tile(nb, (16, 8, 4, 2, 1))
    nq = T // tq
    n_mem = mem_k.shape[3]
    return pl.pallas_call(
        functools.partial(_mem_attn_kernel, nbb=nbb, tq=tq),
        out_shape=jax.ShapeDtypeStruct((M, MEM_WIDTH), BF16),
        grid=(nb // nbb, nq),
        in_specs=[pl.BlockSpec((nbb * tq, MEM_WIDTH), lambda b, i: (b * nq + i, 0)),
                  pl.BlockSpec((1, nbb, MEM_WIDTH, n_mem), lambda b, i: (layer, b, 0, 0)),
                  pl.BlockSpec((1, nbb, MEM_WIDTH, n_mem), lambda b, i: (layer, b, 0, 0))],
        out_specs=pl.BlockSpec((nbb * tq, MEM_WIDTH), lambda b, i: (b * nq + i, 0)),
        compiler_params=_cparams(2),
        name="mem_attn",
    )(q, mem_k, mem_v)


def _ffn_kernel(*refs, tm, T, long_mode, col_chunks, final_norm):
    (x_ref, ot_ref, om_ref, wot_ref, wom_ref, g_ref, wg_ref, wv_ref, cw_ref, cb_ref, wd_ref, cp_ref) = refs[:12]
    k = 12
    fin_ref = refs[k] if final_norm else None
    k += int(final_norm)
    o_ref, aux_ref = refs[k:k + 2]
    k += 2
    y_ref = refs[k] if final_norm else None
    k += int(final_norm)
    xn_scr, h_scr, x1_scr = refs[k:k + 3]
    if long_mode:
        carry_scr = refs[k + 3]
        tail_ref = aux_ref
    else:
        gate_ref = aux_ref
    x1_scr[...] = x_ref[...] + (_dg(ot_ref[...], wot_ref[...]) + _dg(om_ref[...], wom_ref[...]))
    xn_scr[...] = _rms(x1_scr[...], g_ref[...]).astype(BF16)
    xn = xn_scr[...]
    row = lax.broadcasted_iota(jnp.int32, (tm, 1), 0)
    if long_mode:
        nblk = T // tm

        @pl.when(pl.program_id(0) % nblk == 0)
        def _():
            carry_scr[...] = cp_ref[0]
    else:
        t_in_seq = row % T
        ncp = cp_ref.shape[0]
        rr = lax.broadcasted_iota(jnp.int32, (tm, ncp), 0)
        cc = lax.broadcasted_iota(jnp.int32, (tm, ncp), 1)
        same_seq = (rr // T) == (cc >> 1)
        sel1 = jnp.where(same_seq & ((rr % T) + 1 == (cc & 1)), 1.0, 0.0).astype(BF16)
        sel2 = jnp.where(same_seq & ((rr % T) == (cc & 1)), 1.0, 0.0).astype(BF16)
        sel = jnp.concatenate([sel1, sel2], axis=0)
    for (lo, w) in col_chunks:
        cs = slice(lo, lo + w)
        gate = _dg(xn, wg_ref[:, cs])
        val = _dg(xn, wv_ref[:, cs])
        r1 = pltpu.roll(gate, 1, 0)
        r2 = pltpu.roll(gate, 2, 0)
        if long_mode:
            c0 = carry_scr[6:7, cs]
            c1 = carry_scr[7:8, cs]
            p1 = jnp.where(row == 0, c1, r1)
            p2 = jnp.where(row == 0, c0, jnp.where(row == 1, c1, r2))
            tail = gate[tm - 8:tm, :]
            carry_scr[:, cs] = tail
            tail_ref[0, :, cs] = tail
        else:
            cpc = cp_ref[:, cs]
            hi = cpc.astype(BF16)
            r_hi = cpc - hi.astype(F32)
            mid = r_hi.astype(BF16)
            lo = (r_hi - mid.astype(F32)).astype(BF16)
            edge = (_dg(sel, hi) + _dg(sel, mid)) + _dg(sel, lo)
            p1 = jnp.where(t_in_seq >= 1, r1, 0.0) + edge[:tm]
            p2 = jnp.where(t_in_seq >= 2, r2, 0.0) + edge[tm:]
            gate_ref[:, cs] = gate
        conv = cb_ref[:, cs] + p2 * cw_ref[0:1, cs]
        conv = conv + p1 * cw_ref[1:2, cs]
        conv = conv + gate * cw_ref[2:3, cs]
        h_scr[:, cs] = (conv * jax.nn.sigmoid(conv) * val).astype(BF16)
    out = x1_scr[...] + _dg(h_scr[...], wd_ref[...])
    o_ref[...] = out
    if final_norm:
        y_ref[...] = _rms(out, fin_ref[...])


def _ffn(x, o_tok, o_mem, w_o_tok, w_o_mem, g, w_gate, w_val, conv_w, conv_b, w_down, conv_prev, final_g=None, *,
         nseq, T):
    M = x.shape[0]
    long_mode = T >= 512
    tm = 512 if long_mode else 256
    assert (T % tm == 0) if long_mode else (tm % T == 0 and M % tm == 0)
    col_chunks = tuple((lo, min(256, D_FF - lo)) for lo in range(0, D_FF, 256))
    const = lambda a: pl.BlockSpec(a.shape, lambda i: (0,) * a.ndim, pipeline_mode=pl.Buffered(1))
    small = lambda a: pl.BlockSpec(a.shape, lambda i: (0,) * a.ndim)
    g2 = g.reshape(1, D_MODEL).astype(F32)
    cb2 = conv_b.reshape(1, D_FF).astype(F32)
    rows = lambda w: pl.BlockSpec((tm, w), lambda i: (i, 0))
    common_specs = [rows(D_MODEL), rows(MIX_A), rows(MEM_WIDTH), const(w_o_tok), const(w_o_mem), small(g2),
                    const(w_gate), const(w_val), small(conv_w), small(cb2), const(w_down)]
    common_args = [x, o_tok, o_mem, w_o_tok, w_o_mem, g2, w_gate, w_val, conv_w, cb2, w_down]
    final_norm = final_g is not None
    fin_args = [final_g.reshape(1, D_MODEL).astype(F32)] if final_norm else []
    fin_specs = [pl.BlockSpec((1, D_MODEL), lambda i: (0, 0))] if final_norm else []
    fin_out = [jax.ShapeDtypeStruct((M, D_MODEL), F32)] if final_norm else []
    fin_out_specs = [rows(D_MODEL)] if final_norm else []
    base_scratch = [pltpu.VMEM((tm, D_MODEL), BF16), pltpu.VMEM((tm, D_FF), BF16), pltpu.VMEM((tm, D_MODEL), F32)]
    kern = functools.partial(_ffn_kernel, tm=tm, T=T, long_mode=long_mode, col_chunks=col_chunks,
                             final_norm=final_norm)
    if long_mode:
        nblk = T // tm
        cp = jnp.concatenate([jnp.zeros((nseq, 6, D_FF), F32), conv_prev.astype(F32)], axis=1)
        res = pl.pallas_call(
            kern,
            out_shape=[jax.ShapeDtypeStruct((M, D_MODEL), F32), jax.ShapeDtypeStruct((M // tm, 8, D_FF), F32)] + fin_out,
            grid=(M // tm,),
            in_specs=common_specs + [pl.BlockSpec((1, 8, D_FF), lambda i: (i // nblk, 0, 0))] + fin_specs,
            out_specs=[rows(D_MODEL), pl.BlockSpec((1, 8, D_FF), lambda i: (i, 0, 0))] + fin_out_specs,
            scratch_shapes=base_scratch + [pltpu.VMEM((8, D_FF), F32)],
            compiler_params=_cparams(1),
            name="convglu_long",
        )(*common_args, cp, *fin_args)
        new_conv = res[1].reshape(nseq, nblk, 8, D_FF)[:, -1, 6:8, :]
        return res[0], new_conv, (res[2] if final_norm else None)
    ncp = (CONV_W - 1) * (tm // T)
    res = pl.pallas_call(
        kern,
        out_shape=[jax.ShapeDtypeStruct((M, D_MODEL), F32), jax.ShapeDtypeStruct((M, D_FF), F32)] + fin_out,
        grid=(M // tm,),
        in_specs=common_specs + [pl.BlockSpec((ncp, D_FF), lambda i: (i, 0))] + fin_specs,
        out_specs=[rows(D_MODEL), rows(D_FF)] + fin_out_specs,
        scratch_shapes=base_scratch,
        compiler_params=_cparams(1),
        name="convglu_short",
    )(*common_args, conv_prev.astype(F32).reshape(nseq * (CONV_W - 1), D_FF), *fin_args)
    new_conv = res[1].reshape(nseq, T, D_FF)[:, T - 2:, :]
    return res[0], new_conv, (res[2] if final_norm else None)


def _rope128(x128, cs):
    a = x128 * cs
    s = a + pltpu.roll(a, 64, 1)
    lane = lax.broadcasted_iota(jnp.int32, a.shape, 1)
    return jnp.where(lane < 64, s, 0.0)


def _mla_kv_kernel(*refs, expand):
    x_ref, gk_ref, wa_ref, gc_ref, cs_ref = refs[:5]
    k = 5
    wx_ref = refs[k] if expand else None
    k += int(expand)
    ckv_ref, kpe_ref = refs[k:k + 2]
    kvx_ref = refs[k + 2] if expand else None
    xn = _rms(x_ref[...], gk_ref[...]).astype(BF16)
    h = _dg(xn, wa_ref[...])
    ckv = _rms(h[:, :KV_LORA], gc_ref[...])
    rp = _rope128(h[:, KV_LORA:], cs_ref[...])
    ckv_ref[...] = ckv
    kpe_ref[...] = rp[:, :ROPE_DIM]
    if expand:
        ckpe = jnp.concatenate([ckv, rp], axis=1).astype(BF16)
        for lo in range(0, wx_ref.shape[1], 768):
            kvx_ref[:, lo:lo + 768] = _dg(ckpe, wx_ref[:, lo:lo + 768]).astype(kvx_ref.dtype)


def _mla_kv(x, norm_kv, w_kva, norm_ckv, cs, w_kvx, *, T):
    M = x.shape[0]
    tm = _pos_tile(M, T)
    ncs = cs.shape[0] // tm
    expand = w_kvx is not None
    const = lambda a: pl.BlockSpec(a.shape, lambda i: (0,) * a.ndim)
    rows = lambda w: pl.BlockSpec((tm, w), lambda i: (i, 0))
    gk = norm_kv.reshape(1, D_MODEL).astype(F32)
    gc = norm_ckv.reshape(1, KV_LORA).astype(F32)
    args = [x, gk, w_kva, gc, cs] + ([w_kvx] if expand else [])
    in_specs = [rows(D_MODEL), const(gk), const(w_kva), const(gc),
                pl.BlockSpec((tm, 128), lambda i: (i % ncs, 0))] + ([const(w_kvx)] if expand else [])
    out_shape = [jax.ShapeDtypeStruct((M, KV_LORA), F32), jax.ShapeDtypeStruct((M, ROPE_DIM), F32)]
    out_specs = [rows(KV_LORA), rows(ROPE_DIM)]
    if expand:
        out_shape.append(jax.ShapeDtypeStruct((M, w_kvx.shape[1]), BF16))
        out_specs.append(rows(w_kvx.shape[1]))
    return pl.pallas_call(
        functools.partial(_mla_kv_kernel, expand=expand),
        out_shape=out_shape,
        grid=(M // tm,),
        in_specs=in_specs,
        out_specs=out_specs,
        compiler_params=_cparams(1),
        name="mla_kv",
    )(*args)


def _mla_q_kernel(x_ref, gm_ref, win_ref, gq_ref, wq_ref, cs_ref, q_ref, qm_ref):
    xn = _rms(x_ref[...], gm_ref[...]).astype(BF16)
    proj = _dg(xn, win_ref[...])
    qm_ref[...] = proj[:, Q_LORA:].astype(qm_ref.dtype)
    qn = _rms(proj[:, :Q_LORA], gq_ref[...]).astype(BF16)
    cs = cs_ref[...]
    for h in range(H_B):
        base = h * 256
        qh = _dg(qn, wq_ref[:, base:base + 256])
        q_ref[:, base:base + 128] = qh[:, :128].astype(q_ref.dtype)
        q_ref[:, base + 128:base + 256] = _rope128(qh[:, 128:], cs).astype(q_ref.dtype)


def _mla_q(x, g_mix, w_in, norm_q, w_q, cs, *, T):
    M = x.shape[0]
    tm = _pos_tile(M, T)
    ncs = cs.shape[0] // tm
    const = lambda a: pl.BlockSpec(a.shape, lambda i: (0,) * a.ndim)
    gm = g_mix.reshape(1, D_MODEL).astype(F32)
    gq = norm_q.reshape(1, Q_LORA).astype(F32)
    return pl.pallas_call(
        _mla_q_kernel,
        out_shape=(jax.ShapeDtypeStruct((M, H_B * 256), BF16), jax.ShapeDtypeStruct((M, MEM_WIDTH), BF16)),
        grid=(M // tm,),
        in_specs=[pl.BlockSpec((tm, D_MODEL), lambda i: (i, 0)), const(gm), const(w_in), const(gq), const(w_q),
                  pl.BlockSpec((tm, 128), lambda i: (i % ncs, 0))],
        out_specs=(pl.BlockSpec((tm, H_B * 256), lambda i: (i, 0)), pl.BlockSpec((tm, MEM_WIDTH), lambda i: (i, 0))),
        compiler_params=_cparams(1),
        name="mla_q",
    )(x, gm, w_in, gq, w_q, cs)


def _flash_kernel(q_ref, k_ref, v_ref, o_ref, m_scr, l_scr, acc_scr, *, tq, tk):
    i = pl.program_id(1)
    j = pl.program_id(2)

    @pl.when(j == 0)
    def _():
        m_scr[...] = jnp.full(m_scr.shape, -jnp.inf, F32)
        l_scr[...] = jnp.zeros(l_scr.shape, F32)
        acc_scr[...] = jnp.zeros(acc_scr.shape, F32)

    hs = range(H_B)

    def block(masked):
        c = ATTN_SCALE * math.log2(math.e)
        ss = [_dg(q_ref[:, h * 256:(h + 1) * 256], k_ref[:, h * 256:(h + 1) * 256], NT) for h in hs]
        if masked:
            keep = lax.broadcasted_iota(jnp.int32, (tq, tk), 1) <= lax.broadcasted_iota(jnp.int32, (tq, tk), 0)
            ss = [jnp.where(keep, s, -jnp.inf) for s in ss]
        nkt = tk // 128
        tiles = [[ss[h][:, t * 128:(t + 1) * 128] for t in range(nkt)] for h in hs]
        m_prev = [m_scr[h] for h in hs]
        m_new = []
        for h in hs:
            tmax = tiles[h][0]
            for t in range(1, nkt):
                tmax = jnp.maximum(tmax, tiles[h][t])
            m_new.append(jnp.maximum(m_prev[h], jnp.max(tmax, axis=-1, keepdims=True)))
        alpha = [jnp.exp2((m_prev[h] - m_new[h]) * c) for h in hs]
        ps = [[jnp.exp2((tiles[h][t] - m_new[h]) * c) for t in range(nkt)] for h in hs]
        pv = [_dg(jnp.concatenate([p.astype(BF16) for p in ps[h]], axis=1), v_ref[:, h * V_DIM:(h + 1) * V_DIM])
              for h in hs]
        for h in hs:
            psum = ps[h][0]
            for t in range(1, nkt):
                psum = psum + ps[h][t]
            l_scr[h] = alpha[h] * l_scr[h] + jnp.sum(psum, axis=-1, keepdims=True)
            acc_scr[:, h * V_DIM:(h + 1) * V_DIM] = alpha[h] * acc_scr[:, h * V_DIM:(h + 1) * V_DIM] + pv[h]
            m_scr[h] = m_new[h]

    @pl.when(j < i)
    def _():
        block(False)

    @pl.when(j == i)
    def _():
        block(True)
        for h in hs:
            sl = slice(h * V_DIM, (h + 1) * V_DIM)
            o_ref[:, sl] = (acc_scr[:, sl] / l_scr[h]).astype(o_ref.dtype)


def _flash_causal(q, kv, *, nb, T):
    M = q.shape[0]
    tq = tk = 512
    nq = T // tq
    kw = H_B * 256
    vw = H_B * V_DIM
    return pl.pallas_call(
        functools.partial(_flash_kernel, tq=tq, tk=tk),
        out_shape=jax.ShapeDtypeStruct((M, vw), BF16),
        grid=(nb, nq, nq),
        in_specs=[pl.BlockSpec((tq, kw), lambda b, i, j: (b * nq + i, 0)),
                  pl.BlockSpec((tk, kw), lambda b, i, j: (b * nq + jnp.minimum(i, j), 0)),
                  pl.BlockSpec((tk, vw), lambda b, i, j: (b * nq + jnp.minimum(i, j), kw // vw))],
        out_specs=pl.BlockSpec((tq, vw), lambda b, i, j: (b * nq + i, 0)),
        scratch_shapes=[pltpu.VMEM((H_B, tq, 128), F32), pltpu.VMEM((H_B, tq, 128), F32), pltpu.VMEM((tq, vw), F32)],
        compiler_params=_cparams(3),
        name="mla_causal",
    )(q, kv, kv)


def _decode_kernel(pt_ref, q_ref, ckn_ref, kpn_ref, wuk_ref, wuv_ref, ck_hbm, kp_hbm, o_ref,
                   ck_buf, kp_buf, sem, m_scr, l_scr, acc_scr, *, PP, n_steps, n_total, TS):
    j = pl.program_id(1)
    n = pl.program_id(0) * n_steps + j
    slot = lax.rem(n, 2)
    R = H_B * TS

    def group_copies(step, slot_, for_wait):
        cps = []
        for p in range(PP):
            page = 0 if for_wait else pt_ref[step * PP + p]
            cps.append(pltpu.make_async_copy(ck_hbm.at[page], ck_buf.at[slot_, p], sem.at[slot_, 0]))
            cps.append(pltpu.make_async_copy(kp_hbm.at[page], kp_buf.at[slot_, p], sem.at[slot_, 1]))
        return cps

    @pl.when(n == 0)
    def _():
        for cp in group_copies(0, 0, False):
            cp.start()

    nxt = jnp.minimum(n + 1, n_total - 1)
    for cp in group_copies(nxt, 1 - slot, False):
        cp.start()
    for cp in group_copies(n, slot, True):
        cp.wait()

    @pl.when(j == 0)
    def _():
        m_scr[...] = jnp.full(m_scr.shape, -jnp.inf, F32)
        l_scr[...] = jnp.zeros(l_scr.shape, F32)
        acc_scr[...] = jnp.zeros(acc_scr.shape, F32)

    q_all = q_ref[0].astype(F32)
    q_lat = jnp.concatenate(
        [_dg(q_all[:, h * 256:h * 256 + NOPE_DIM].astype(BF16), wuk_ref[h]) for h in range(H_B)], axis=0).astype(BF16)
    q_pe = jnp.concatenate(
        [q_all[:, h * 256 + NOPE_DIM:h * 256 + NOPE_DIM + ROPE_DIM] for h in range(H_B)], axis=0).astype(BF16)
    c = ATTN_SCALE * math.log2(math.e)

    cks = [ck_buf[slot, p].astype(BF16) for p in range(PP)]
    kps = [kp_buf[slot, p].astype(BF16) for p in range(PP)]
    ss = [_dg(q_lat, cks[p], NT) + _dg(q_pe, kps[p]) for p in range(PP)]
    pad = PAGE_SIZE - TS
    ckn = jnp.concatenate([ckn_ref[0], jnp.zeros((pad, KV_LORA), F32)], axis=0).astype(BF16)
    kpn = jnp.concatenate([kpn_ref[0], jnp.zeros((pad, ROPE_DIM), F32)], axis=0).astype(BF16)
    t_q = lax.broadcasted_iota(jnp.int32, (R, PAGE_SIZE), 0) % TS
    col = lax.broadcasted_iota(jnp.int32, (R, PAGE_SIZE), 1)
    last_shift = jnp.where(j == n_steps - 1, 0, -PAGE_SIZE)
    s_new = jnp.where(col <= t_q + last_shift, _dg(q_lat, ckn, NT) + _dg(q_pe, kpn, NT), -jnp.inf)
    ss.append(s_new)
    cks.append(ckn)

    tile_max = ss[0]
    for s in ss[1:]:
        tile_max = jnp.maximum(tile_max, s)
    m_prev = m_scr[...]
    m_new = jnp.maximum(m_prev, jnp.max(tile_max, axis=-1, keepdims=True))
    alpha = jnp.exp2((m_prev - m_new) * c)
    ps = [jnp.exp2((s - m_new) * c) for s in ss]
    p_sum = ps[0]
    for p in ps[1:]:
        p_sum = p_sum + p
    acc = jnp.concatenate([alpha, alpha], axis=1) * acc_scr[...]
    for p, vv in zip(ps, cks):
        acc = acc + _dg(p.astype(BF16), vv)
    m_scr[...] = m_new
    l_scr[...] = alpha * l_scr[...] + jnp.sum(p_sum, axis=-1, keepdims=True)
    acc_scr[...] = acc

    @pl.when(j == n_steps - 1)
    def _():
        o_lat = acc_scr[...] / jnp.concatenate([l_scr[...], l_scr[...]], axis=1)
        for h in range(H_B):
            o_h = o_lat[h * TS:(h + 1) * TS, :].astype(BF16)
            o_ref[0, :, h * V_DIM:(h + 1) * V_DIM] = _dg(o_h, wuv_ref[h]).astype(o_ref.dtype)

    @pl.when(n == n_total - 1)
    def _():
        for cp in group_copies(n, 1 - slot, True):
            cp.wait()


def _decode_attn(page_table, q, cache_ckv, cache_kpe, ckv_new, kpe_new, w_ukT, w_uv, *, PP=64):
    nb, n_pages = page_table.shape
    TS = ckv_new.shape[1]
    R = H_B * TS
    n_steps = n_pages // PP
    pt = page_table.reshape(-1).astype(jnp.int32)
    kpe_t = jnp.swapaxes(cache_kpe, 1, 2)
    grid_spec = pltpu.PrefetchScalarGridSpec(
        num_scalar_prefetch=1,
        grid=(nb, n_steps),
        in_specs=[pl.BlockSpec((1, TS, H_B * 256), lambda b, j, pt_ref: (b, 0, 0)),
                  pl.BlockSpec((1, TS, KV_LORA), lambda b, j, pt_ref: (b, 0, 0)),
                  pl.BlockSpec((1, TS, ROPE_DIM), lambda b, j, pt_ref: (b, 0, 0)),
                  pl.BlockSpec((H_B, NOPE_DIM, KV_LORA), lambda b, j, pt_ref: (0, 0, 0)),
                  pl.BlockSpec((H_B, KV_LORA, V_DIM), lambda b, j, pt_ref: (0, 0, 0)),
                  pl.BlockSpec(memory_space=pl.ANY),
                  pl.BlockSpec(memory_space=pl.ANY)],
        out_specs=pl.BlockSpec((1, TS, H_B * V_DIM), lambda b, j, pt_ref: (b, 0, 0)),
        scratch_shapes=[pltpu.VMEM((2, PP, PAGE_SIZE, KV_LORA), F32),
                        pltpu.VMEM((2, PP, ROPE_DIM, PAGE_SIZE), F32),
                        pltpu.SemaphoreType.DMA((2, 2)),
                        pltpu.VMEM((R, 128), F32), pltpu.VMEM((R, 128), F32), pltpu.VMEM((R, KV_LORA), F32)],
    )
    return pl.pallas_call(
        functools.partial(_decode_kernel, PP=PP, n_steps=n_steps, n_total=nb * n_steps, TS=TS),
        out_shape=jax.ShapeDtypeStruct((nb, TS, H_B * V_DIM), BF16),
        grid_spec=grid_spec,
        compiler_params=_cparams(2),
        name="mla_decode",
    )(pt, q, ckv_new, kpe_new, w_ukT, w_uv, cache_ckv, kpe_t)


def _rope_table(pos):
    half = ROPE_DIM // 2
    inv = ROPE_BASE ** (-jnp.arange(half, dtype=F32) / half)
    ang = pos.astype(F32)[:, None] * inv[None, :]
    cos, sin = jnp.cos(ang), jnp.sin(ang)
    return jnp.concatenate([cos, cos, -sin, sin], axis=-1)


def _swap_halves(w):
    half = ROPE_DIM // 2
    return jnp.concatenate([w[..., half:], w[..., :half]], axis=-1)


def _prep_rwkv_weights(l, P):
    w_in = P["w_in_a"][l]
    mu = P["mu_a"][l]
    cuts = [0, MIX_A, MIX_A + DECAY_LORA, 2 * MIX_A + DECAY_LORA, 3 * MIX_A + DECAY_LORA,
            3 * MIX_A + DECAY_LORA + A_LORA, C_RWKV, C_A]
    seg = lambda a, i: a[..., cuts[i]:cuts[i + 1]]
    r_w, wl_w, k_w, v_w, al_w, gl_w, qm_w = [seg(w_in, i) for i in range(7)]
    r_m, wl_m, k_m, v_m, al_m, gl_m = [seg(mu, i) for i in range(6)]
    zc = lambda n: jnp.zeros((D_MODEL, n), F32)
    zm = lambda n: jnp.zeros((n,), F32)
    if l > 0:
        vl_w, vl_m = P["w_vres_in"][l - 1], P["mu_vres"][l - 1]
        wv_up, v0 = P["w_vres_up"][l - 1], P["v0"][l - 1]
    else:
        vl_w, vl_m = zc(VRES_LORA), zm(VRES_LORA)
        wv_up, v0 = jnp.zeros((VRES_LORA, MIX_A), F32), zm(MIX_A)
    w_full = jnp.concatenate([r_w, k_w, v_w, wl_w, al_w, gl_w, vl_w, zc(64), qm_w, zc(128)], axis=1)
    mu_full = jnp.concatenate([r_m, k_m, v_m, wl_m, al_m, gl_m, vl_m, zm(64), zm(MEM_WIDTH), zm(128)])
    zr = lambda n: jnp.zeros((n, MIX_A), F32)
    return dict(
        w_in=w_full.astype(BF16),
        mu=mu_full.reshape(1, PA_COLS),
        wd=jnp.concatenate([P["w_decay_up"][l], zr(A_LORA)], axis=0).astype(BF16),
        wa=jnp.concatenate([zr(DECAY_LORA), P["w_a_up"][l]], axis=0).astype(BF16),
        wg=jnp.concatenate([P["w_g_up"][l], zr(256 - GATE_LORA)], axis=0).astype(BF16),
        wv=jnp.concatenate([zr(GATE_LORA), wv_up, zr(256 - GATE_LORA - VRES_LORA)], axis=0).astype(BF16),
        w0=P["w0"][l].reshape(1, MIX_A), a0=P["a0"][l].reshape(1, MIX_A), v0=v0.reshape(1, MIX_A),
        k_k=P["k_k"][l].reshape(1, MIX_A), k_a=P["k_a"][l].reshape(1, MIX_A),
    )


def _prep_mla_weights(P):
    w_kv_a = P["w_kv_a"]
    w_kva = jnp.concatenate([w_kv_a, _swap_halves(w_kv_a[:, KV_LORA:])], axis=1).astype(BF16)
    w_kv_b = P["w_kv_b"]
    w_uk, w_uv = w_kv_b[..., :NOPE_DIM], w_kv_b[..., NOPE_DIM:]
    k_part = jnp.zeros((KV_LORA + 128, H_B, 256), F32)
    k_part = k_part.at[:KV_LORA, :, :NOPE_DIM].set(w_uk)
    eye = jnp.eye(ROPE_DIM, dtype=F32)
    k_part = k_part.at[KV_LORA:KV_LORA + ROPE_DIM, :, NOPE_DIM:NOPE_DIM + ROPE_DIM].set(
        jnp.broadcast_to(eye[:, None, :], (ROPE_DIM, H_B, ROPE_DIM)))
    v_part = jnp.zeros((KV_LORA + 128, H_B, V_DIM), F32).at[:KV_LORA].set(w_uv)
    w_kvx = jnp.concatenate([k_part.reshape(KV_LORA + 128, H_B * 256), v_part.reshape(KV_LORA + 128, H_B * V_DIM)],
                            axis=1).astype(BF16)
    w_q = []
    for j in range(N_B):
        wq = P["w_q_b"][j].reshape(Q_LORA, H_B, NOPE_DIM + ROPE_DIM)
        rope_w = wq[..., NOPE_DIM:]
        w_q.append(jnp.concatenate([wq[..., :NOPE_DIM], rope_w, _swap_halves(rope_w)], axis=-1)
                   .reshape(Q_LORA, H_B * 256).astype(BF16))
    return dict(
        w_kva=w_kva, w_kvx=w_kvx, w_q=w_q,
        w_ukT=jnp.transpose(w_uk, (1, 2, 0)).astype(BF16),
        w_uv=jnp.transpose(w_uv, (1, 0, 2)).astype(BF16),
    )


def _trunk(x, pos, mem_k, mem_v, wkv0, shift0, conv0, past, P, W, *, nseq, T):
    M = nseq * T
    x = x.reshape(M, D_MODEL)
    cs = _rope_table(pos)
    if T < ROW_TILES[0]:
        cs = jnp.tile(cs, (nseq, 1))
    chunk = 64 if T >= 64 else T
    new_wkv, new_shift, new_conv = [], [], []
    v_first = None
    ckv = kpe = ckpe = kvx = None
    for l in range(DEPTH):
        g_mix = P["norm_mix"][l]
        if l < N_A:
            pw = W["rwkv"][l]
            x_last = x.reshape(nseq, T, D_MODEL)[:, -1]
            new_shift.append(_rmsnorm(x_last, g_mix, name="shift_norm"))
            sp = _mm([shift0[l].astype(F32)], [pw["w_in"]], name="rwkv_in_shift")
            r, lw, k, v, kn, b, g, q_mem = _rwkv_prep(x, sp, pw, v_first if l > 0 else None, g_mix, nseq=nseq, T=T)
            if l == 0:
                v_first = v
            h0 = wkv0[l].astype(F32)
            o_tok, h_new = _wkv(r, lw, k, v, kn, b, g, P["r_k"][l], P["lnx_w"][l], P["lnx_b"][l], h0,
                                nseq=nseq, T=T, C=chunk)
            new_wkv.append(h_new)
        else:
            j = l - N_A
            if l == N_A:
                res = _mla_kv(x, P["norm_kv"], W["mla"]["w_kva"], P["norm_ckv"], cs,
                              W["mla"]["w_kvx"] if past is None else None, T=T)
                ckv, kpe = res[0], res[1]
                kvx = res[2] if past is None else None
            q, q_mem = _mla_q(x, g_mix, W["w_in_b"][j], P["norm_q"][j], W["mla"]["w_q"][j], cs, T=T)
            if past is None:
                o_tok = _flash_causal(q, kvx, nb=nseq, T=T)
            else:
                o_tok = _decode_attn(past[2], q.reshape(nseq, T, H_B * 256), past[0], past[1],
                                     ckv.reshape(nseq, T, KV_LORA), kpe.reshape(nseq, T, ROPE_DIM),
                                     W["mla"]["w_ukT"], W["mla"]["w_uv"]).reshape(M, H_B * V_DIM)
        o_mem = _mem_attn(q_mem, mem_k, mem_v, l, nb=nseq, T=T)
        x, c_new, y = _ffn(x, o_tok, o_mem, W["w_o_tok"][l], W["w_o_mem"][l], P["norm_ffn"][l], W["w_gate"][l],
                           W["w_val"][l], P["conv_w"][l], P["conv_b"][l], W["w_down"][l], conv0[l],
                           P["final_norm"] if l == DEPTH - 1 else None, nseq=nseq, T=T)
        new_conv.append(c_new)
    return (y.reshape(nseq, T, D_MODEL), ckv.reshape(nseq, T, KV_LORA), kpe.reshape(nseq, T, ROPE_DIM),
            jnp.stack(new_wkv), jnp.stack(new_shift), jnp.stack(new_conv))


def kernel(x_prompt, x_sample, cache_ckv, cache_kpe, cache_mem_k, cache_mem_v, state_wkv, state_shift, state_conv, page_table, mem_prompt, norm_mix, norm_ffn, norm_mem, w_mem_kv, w_o, w_in_a, mu_a, w_vres_in, mu_vres, w_decay_up, w0, w_a_up, a0, w_g_up, w_vres_up, v0, k_k, k_a, r_k, lnx_w, lnx_b, norm_kv, w_kv_a, norm_ckv, w_kv_b, w_in_b, norm_q, w_q_b, w_ffn_up, conv_w, conv_b, w_ffn_down, final_norm):
    P = dict(norm_mix=norm_mix, norm_ffn=norm_ffn, w_o=w_o, w_in_a=w_in_a, mu_a=mu_a, w_vres_in=w_vres_in,
             mu_vres=mu_vres, w_decay_up=w_decay_up, w0=w0, w_a_up=w_a_up, a0=a0, w_g_up=w_g_up,
             w_vres_up=w_vres_up, v0=v0, k_k=k_k, k_a=k_a, r_k=r_k, lnx_w=lnx_w, lnx_b=lnx_b,
             norm_kv=norm_kv, w_kv_a=w_kv_a, norm_ckv=norm_ckv, w_kv_b=w_kv_b, w_in_b=w_in_b, norm_q=norm_q,
             w_q_b=w_q_b, conv_w=conv_w, conv_b=conv_b, final_norm=final_norm)
    W = dict(
        rwkv=[_prep_rwkv_weights(l, P) for l in range(N_A)],
        mla=_prep_mla_weights(P),
        w_in_b=[w_in_b[j].astype(BF16) for j in range(N_B)],
        w_o_tok=[w_o[l, :MIX_A].astype(BF16) for l in range(DEPTH)],
        w_o_mem=[w_o[l, MIX_A:].astype(BF16) for l in range(DEPTH)],
        w_gate=[w_ffn_up[l, :, :D_FF].astype(BF16) for l in range(DEPTH)],
        w_val=[w_ffn_up[l, :, D_FF:].astype(BF16) for l in range(DEPTH)],
        w_down=[w_ffn_down[l].astype(BF16) for l in range(DEPTH)],
    )
    dt = x_prompt.dtype
    B, T = x_prompt.shape[:2]
    n_mem = mem_prompt.shape[1]
    mem_k_all, mem_v_all = _mem_kv(mem_prompt, norm_mem, w_mem_kv)
    y_p, ckv_p, kpe_p, wkv_p, shift_p, conv_p = _trunk(
        x_prompt, jnp.arange(T), mem_k_all, mem_v_all,
        jnp.zeros((N_A, B, H_A, HEAD_A, HEAD_A), dt), jnp.zeros((N_A, B, D_MODEL), dt),
        jnp.zeros((DEPTH, B, CONV_W - 1, D_FF), dt), None, P, W, nseq=B, T=T)
    DB, TS = x_sample.shape[:2]
    past_len = page_table.shape[1] * PAGE_SIZE
    to_t = lambda c: jnp.transpose(c, (0, 1, 3, 4, 2)).reshape(DEPTH, DB, MEM_WIDTH, n_mem)
    y_s, ckv_s, kpe_s, wkv_s, shift_s, conv_s = _trunk(
        x_sample, past_len + jnp.arange(TS), to_t(cache_mem_k), to_t(cache_mem_v), state_wkv, state_shift, state_conv,
        (cache_ckv, cache_kpe, page_table), P, W, nseq=DB, T=TS)
    from_t = lambda a: jnp.transpose(a.reshape(DEPTH, B, MEM_HEADS, MEM_HEAD_DIM, n_mem), (0, 1, 4, 2, 3))
    mem_k_out = from_t(mem_k_all)
    mem_v_out = from_t(mem_v_all)
    return (y_p, y_s, ckv_p, kpe_p, mem_k_out, mem_v_out, wkv_p, shift_p, conv_p,
            ckv_s, kpe_s, wkv_s, shift_s, conv_s)
```

```python
import functools
import math

import jax
import jax.numpy as jnp
from jax import lax
from jax.experimental import pallas as pl
from jax.experimental.pallas import tpu as pltpu

F32 = jnp.float32
BF16 = jnp.bfloat16

D_MODEL = 1024
DEPTH = 4
N_A = 2
N_B = 2
MEM_WIDTH = 256
MIX_A = 768
HEAD_A = 64
H_A = 12
DECAY_LORA = 64
A_LORA = 64
VRES_LORA = 32
GATE_LORA = 160
GN_EPS = 64e-5
C_RWKV = 3 * MIX_A + DECAY_LORA + A_LORA + GATE_LORA
C_A = C_RWKV + MEM_WIDTH
MEM_HEADS = 4
MEM_HEAD_DIM = 64
MEM_SCALE = MEM_HEAD_DIM ** -0.5
NOPE_DIM = 128
ROPE_DIM = 64
V_DIM = 128
H_B = 6
Q_LORA = 256
KV_LORA = 256
ROPE_BASE = 10000.0
ATTN_SCALE = (NOPE_DIM + ROPE_DIM) ** -0.5
D_FF = 2816
CONV_W = 3
RMS_EPS = 1e-6
PAGE_SIZE = 128

PA_R, PA_K, PA_V = 0, 768, 1536
PA_LORA = 2304
PA_GATE = 2432
PA_QMEM = 2688
PA_COLS = 3072

VMEM_LIMIT_BYTES = 52 * 1024 * 1024

NN = ((1,), (0,))
NT = ((1,), (1,))
TN = ((0,), (0,))


def _dg(a, b, dims=NN):
    return lax.dot_general(a, b, (dims, ((), ())), preferred_element_type=F32)


def _split(x):
    hi = x.astype(BF16)
    lo = (x - hi.astype(F32)).astype(BF16)
    return hi, lo


def _dot3(a, b, dims=NN):
    ah, al = _split(a)
    bh, bl = _split(b)
    return _dg(ah, bh, dims) + (_dg(ah, bl, dims) + _dg(al, bh, dims))


def _transpose_exact(x, eye_bf16):
    hi = x.astype(BF16)
    r1 = x - hi.astype(F32)
    mid = r1.astype(BF16)
    lo = (r1 - mid.astype(F32)).astype(BF16)
    return (_dg(eye_bf16, hi, NT) + _dg(eye_bf16, mid, NT)) + _dg(eye_bf16, lo, NT)


def _cparams(n_axes):
    return pltpu.CompilerParams(dimension_semantics=("arbitrary",) * n_axes,
                                vmem_limit_bytes=VMEM_LIMIT_BYTES)


def _rms(x, g, eps=RMS_EPS):
    return x * lax.rsqrt(jnp.mean(x * x, axis=-1, keepdims=True) + eps) * g


def _pick_tile(n, candidates):
    for c in candidates:
        if n % c == 0:
            return c
    return n


ROW_TILES = (512, 256, 128, 64, 32, 16, 8)


def _pos_tile(M, T):
    return _pick_tile(T, ROW_TILES) if T >= ROW_TILES[0] else _pick_tile(M, ROW_TILES)


def _mm_kernel(*refs, n_in, has_norm, has_res, f32_in):
    xs = refs[:n_in]
    ws = refs[n_in:2 * n_in]
    k = 2 * n_in
    g_ref = refs[k] if has_norm else None
    k += int(has_norm)
    r_ref = refs[k] if has_res else None
    k += int(has_res)
    o_ref = refs[k]
    scr = list(refs[k + 1:])

    lhs = []
    si = 0
    for idx in range(n_in):
        if f32_in[idx]:
            s_ref = scr[si]
            si += 1

            @pl.when(pl.program_id(1) == 0)
            def _(x_ref=xs[idx], s_ref=s_ref, idx=idx):
                x = x_ref[...]
                if has_norm and idx == 0:
                    x = _rms(x, g_ref[...])
                s_ref[...] = x.astype(BF16)

            lhs.append(s_ref)
        else:
            lhs.append(xs[idx])
    acc = _dg(lhs[0][...], ws[0][...])
    for idx in range(1, n_in):
        acc = acc + _dg(lhs[idx][...], ws[idx][...])
    if has_res:
        acc = acc + r_ref[...]
    o_ref[...] = acc.astype(o_ref.dtype)


def _mm(xs, ws, *, g=None, res=None, out_dtype=F32, x_col_blocks=None, ks=None, name="mm"):
    n_in = len(xs)
    M = xs[0].shape[0]
    N = ws[0].shape[1]
    ks = [w.shape[0] for w in ws]
    x_col_blocks = x_col_blocks or [0] * n_in
    tm = _pick_tile(M, (512, 256, 128, 64, 32, 16, 8))
    tn = _pick_tile(N, (1024, 768, 512, 384, 256, 128))
    f32_in = tuple(x.dtype == F32 for x in xs)
    in_specs = []
    for x, kk, cb in zip(xs, ks, x_col_blocks):
        in_specs.append(pl.BlockSpec((tm, kk), lambda i, j, cb=cb: (i, cb)))
    for w, kk in zip(ws, ks):
        in_specs.append(pl.BlockSpec((kk, tn), lambda i, j: (0, j)))
    args = list(xs) + list(ws)
    if g is not None:
        in_specs.append(pl.BlockSpec((1, ks[0]), lambda i, j: (0, 0)))
        args.append(g.reshape(1, ks[0]).astype(F32))
    if res is not None:
        in_specs.append(pl.BlockSpec((tm, tn), lambda i, j: (i, j)))
        args.append(res)
    scratch = [pltpu.VMEM((tm, kk), BF16) for kk, f in zip(ks, f32_in) if f]
    return pl.pallas_call(
        functools.partial(_mm_kernel, n_in=n_in, has_norm=g is not None, has_res=res is not None,
                          f32_in=f32_in),
        out_shape=jax.ShapeDtypeStruct((M, N), out_dtype),
        grid=(M // tm, N // tn),
        in_specs=in_specs,
        out_specs=pl.BlockSpec((tm, tn), lambda i, j: (i, j)),
        scratch_shapes=scratch,
        compiler_params=_cparams(2),
        name=name,
    )(*args)


def _rmsnorm_kernel(x_ref, g_ref, o_ref):
    o_ref[...] = _rms(x_ref[...], g_ref[...])


def _rmsnorm(x, g, name="rmsnorm"):
    M, D = x.shape
    tm = _pick_tile(M, (512, 256, 128, 64, 32, 16, 8))
    return pl.pallas_call(
        _rmsnorm_kernel,
        out_shape=jax.ShapeDtypeStruct((M, D), F32),
        grid=(M // tm,),
        in_specs=[pl.BlockSpec((tm, D), lambda i: (i, 0)), pl.BlockSpec((1, D), lambda i: (0, 0))],
        out_specs=pl.BlockSpec((tm, D), lambda i: (i, 0)),
        compiler_params=_cparams(1),
        name=name,
    )(x, g.reshape(1, D).astype(F32))


def _mem_kv_kernel(x_ref, g_ref, wt_ref, k_ref, v_ref):
    kvt = _dg(wt_ref[0], _rms(x_ref[0], g_ref[0]).astype(BF16), NT)
    k_ref[0, 0] = kvt[:MEM_WIDTH]
    v_ref[0, 0] = kvt[MEM_WIDTH:]


def _mem_kv(mem, norm_mem, w_mem_kv):
    B, n_mem, _ = mem.shape
    out = jax.ShapeDtypeStruct((DEPTH, B, MEM_WIDTH, n_mem), F32)
    wt = jnp.swapaxes(w_mem_kv, 1, 2).astype(BF16)
    return pl.pallas_call(
        _mem_kv_kernel,
        out_shape=(out, out),
        grid=(DEPTH, B),
        in_specs=[pl.BlockSpec((1, n_mem, D_MODEL), lambda l, b: (b, 0, 0)),
                  pl.BlockSpec((1, 1, D_MODEL), lambda l, b: (l, 0, 0)),
                  pl.BlockSpec((1, 2 * MEM_WIDTH, D_MODEL), lambda l, b: (l, 0, 0))],
        out_specs=(pl.BlockSpec((1, 1, MEM_WIDTH, n_mem), lambda l, b: (l, b, 0, 0)),
                   pl.BlockSpec((1, 1, MEM_WIDTH, n_mem), lambda l, b: (l, b, 0, 0))),
        compiler_params=_cparams(2),
        name="mem_kv",
    )(mem, norm_mem.reshape(DEPTH, 1, D_MODEL).astype(F32), wt)


def _segsum64(x):
    r = lax.broadcasted_iota(jnp.int32, (128, 128), 0) >> 6
    c = lax.broadcasted_iota(jnp.int32, (128, 128), 1) >> 6
    ones = jnp.where(r == c, 1.0, 0.0).astype(BF16)
    outs = []
    for j in range(x.shape[1] // 128):
        hi, lo = _split(x[:, j * 128:(j + 1) * 128])
        outs.append(_dg(hi, ones) + _dg(lo, ones))
    return jnp.concatenate(outs, axis=1)


def _softplus(x):
    return jnp.maximum(x, 0.0) + jnp.log1p(jnp.exp(-jnp.abs(x)))


def _prep_kernel(*refs, has_vres, blocks_per_seq):
    if blocks_per_seq:
        x_ref, gmix_ref, win_ref = refs[:3]
        refs = refs[2:]
    (cur_ref, prev_ref, mu_ref, wd_ref, wa_ref, wg_ref, wv_ref, w0_ref, a0_ref, v0_ref, kk_ref, ka_ref) = refs[:12]
    k = 12
    vf_ref = refs[k] if has_vres else None
    k += int(has_vres)
    r_o, lw_o, k_o, v_o, kn_o, b_o, g_o, qm_o = refs[k:k + 8]

    if blocks_per_seq:
        cur_ref, carry_scr = refs[k + 8], refs[k + 9]
        xn = _rms(x_ref[...], gmix_ref[...]).astype(BF16)
        for lo in range(0, PA_COLS, 512):
            cur_ref[:, lo:lo + 512] = _dg(xn, win_ref[:, lo:lo + 512])
    tm = cur_ref.shape[0]

    if blocks_per_seq:

        @pl.when(pl.program_id(0) % blocks_per_seq == 0)
        def _():
            carry_scr[...] = prev_ref[0]

        first_row = lax.broadcasted_iota(jnp.int32, (tm, 1), 0) == 0

    def mix(lo, hi):
        c = cur_ref[:, lo:hi]
        if blocks_per_seq:
            p = jnp.where(first_row, carry_scr[:, lo:hi], pltpu.roll(c, 1, 0))
        else:
            p = prev_ref[:, lo:hi]
        return c + (p - c) * mu_ref[:, lo:hi]

    lora = mix(PA_LORA, PA_LORA + 128)
    zw = _dg(jnp.tanh(lora).astype(BF16), wd_ref[...]) + w0_ref[...]
    w_log = -_softplus(-zw) - 0.5
    lw_o[...] = -jnp.exp(w_log)
    a = jax.nn.sigmoid(_dg(lora.astype(BF16), wa_ref[...]) + a0_ref[...])
    gin = mix(PA_GATE, PA_GATE + 256)
    g_o[...] = _dg(jax.nn.sigmoid(gin).astype(BF16), wg_ref[...])
    r_o[...] = mix(PA_R, PA_R + MIX_A)
    kx = mix(PA_K, PA_K + MIX_A)
    v = mix(PA_V, PA_V + MIX_A)
    if has_vres:
        sv = jax.nn.sigmoid(_dg(gin.astype(BF16), wv_ref[...]) + v0_ref[...])
        v = v + (vf_ref[...] - v) * sv
    v_o[...] = v
    kk = kx * kk_ref[...]
    kk = kk * lax.rsqrt(jnp.maximum(_segsum64(kk * kk), 1e-24))
    kn_o[...] = kk
    b_o[...] = kk * a
    k_o[...] = kx * (1.0 + (a - 1.0) * ka_ref[...])
    qm_o[...] = cur_ref[:, PA_QMEM:PA_QMEM + MEM_WIDTH].astype(BF16)
    if blocks_per_seq:
        carry_scr[...] = cur_ref[tm - 1:tm, :]


def _rwkv_prep(x, sp, pw, v_first, g_mix, *, nseq, T):
    M = x.shape[0]
    has_vres = v_first is not None
    row = lambda w: pl.BlockSpec((tm, w), lambda i: (i, 0))
    full = lambda a: pl.BlockSpec(a.shape, lambda i: (0,) * a.ndim)
    if T >= 256:
        tm = 256
        blocks_per_seq = T // tm
        g2 = g_mix.reshape(1, D_MODEL).astype(F32)
        lead_args = [x, g2, pw["w_in"], sp.reshape(nseq, 1, PA_COLS)]
        lead_specs = [row(D_MODEL), full(g2),
                      pl.BlockSpec(pw["w_in"].shape, lambda i: (0, 0), pipeline_mode=pl.Buffered(1)),
                      pl.BlockSpec((1, 1, PA_COLS), lambda i: (i // blocks_per_seq, 0, 0))]
        scratch = [pltpu.VMEM((tm, PA_COLS), F32), pltpu.VMEM((1, PA_COLS), F32)]
    else:
        tm = _pick_tile(M, (256, 128, 64, 32, 16, 8))
        blocks_per_seq = 0
        cur = _mm([x], [pw["w_in"]], g=g_mix, name="rwkv_in")
        prev = jnp.concatenate([sp[:, None, :], cur.reshape(nseq, T, PA_COLS)[:, :-1]], axis=1).reshape(M, PA_COLS)
        lead_args = [cur, prev]
        lead_specs = [row(PA_COLS), row(PA_COLS)]
        scratch = []
    consts = [pw["mu"], pw["wd"], pw["wa"], pw["wg"], pw["wv"], pw["w0"], pw["a0"], pw["v0"], pw["k_k"], pw["k_a"]]
    args = lead_args + consts
    in_specs = lead_specs + [full(a) for a in consts]
    if has_vres:
        args.append(v_first)
        in_specs.append(row(MIX_A))
    outs = [jax.ShapeDtypeStruct((M, MIX_A), F32)] * 7 + [jax.ShapeDtypeStruct((M, MEM_WIDTH), BF16)]
    out_specs = [row(MIX_A)] * 7 + [row(MEM_WIDTH)]
    return pl.pallas_call(
        functools.partial(_prep_kernel, has_vres=has_vres, blocks_per_seq=blocks_per_seq),
        out_shape=outs,
        grid=(M // tm,),
        in_specs=in_specs,
        out_specs=out_specs,
        scratch_shapes=scratch,
        compiler_params=_cparams(1),
        name="rwkv_prep",
    )(*args)


def _wkv_kernel(r_ref, lw_ref, k_ref, v_ref, kn_ref, b_ref, g_ref, rk_ref, lnw_ref, lnb_ref, h0_ref,
                o_ref, hT_ref, h_scr, *, C, G, H, N, nsteps, chain, xpose):
    c_idx = pl.program_id(1)
    S = 1 if chain else G
    GC = G * C
    log2c = C.bit_length() - 1

    eye_bf = jnp.where(lax.broadcasted_iota(jnp.int32, (N, N), 0) == lax.broadcasted_iota(jnp.int32, (N, N), 1),
                       1.0, 0.0).astype(BF16)

    @pl.when(c_idx == 0)
    def _():
        for s in range(S):
            for h in range(H):
                h_scr[s * H + h] = _transpose_exact(h0_ref[s, h], eye_bf) if xpose else h0_ref[s, h]

    lw = lw_ref[...]
    gi = lax.broadcasted_iota(jnp.int32, (GC, GC), 0)
    gj = lax.broadcasted_iota(jnp.int32, (GC, GC), 1)
    same = (gi >> log2c) == (gj >> log2c)
    ltri = jnp.where(same & (gj <= gi), 1.0, 0.0).astype(BF16)
    ones_g = jnp.where(same, 1.0, 0.0).astype(BF16)
    lw_hi, lw_lo = _split(lw)
    cum = _dg(ltri, lw_hi) + _dg(ltri, lw_lo)
    tot = _dg(ones_g, lw_hi) + _dg(ones_g, lw_lo)
    e_neg = jnp.exp(-cum)
    r_all = r_ref[...]
    k_all = k_ref[...]
    v_all = v_ref[...]
    kt_all = kn_ref[...] * jnp.exp(cum - lw)
    rt_all = r_all * jnp.exp(cum)
    kh_all = k_all * e_neg
    bh_all = b_ref[...] * e_neg
    e_rem = jnp.exp(tot - cum)
    kb_all = k_all * e_rem
    bb_all = b_ref[...] * e_rem
    p_last = jnp.exp(tot)
    rkr = r_all * k_all * rk_ref[...]

    ti = lax.broadcasted_iota(jnp.int32, (C, C), 0)
    tj = lax.broadcasted_iota(jnp.int32, (C, C), 1)
    strict = tj < ti
    incl = tj <= ti
    ni = lax.broadcasted_iota(jnp.int32, (N, N), 0)
    nj = lax.broadcasted_iota(jnp.int32, (N, N), 1)
    eye_n = ni == nj
    eye_c = jnp.where(ti == tj, 1.0, 0.0)

    items = [(g, h) for g in range(G) for h in range(H)]
    it = range(len(items))
    rows = [slice(g * C, (g + 1) * C) for g, h in items]
    sls = [slice(h * N, (h + 1) * N) for g, h in items]
    cut = lambda a: [a[rows[i], sls[i]].astype(BF16) for i in it]
    kt, rt, kh, bh, kb, bb, vv = (cut(a) for a in (kt_all, rt_all, kh_all, bh_all, kb_all, bb_all, v_all))
    a_k = [jnp.where(strict, _dg(kt[i], kh[i], NT), 0.0).astype(BF16) for i in it]
    a_b = [jnp.where(strict, _dg(kt[i], bh[i], NT), 0.0) for i in it]
    q_k = [jnp.where(incl, _dg(rt[i], kh[i], NT), 0.0).astype(BF16) for i in it]
    q_b = [jnp.where(incl, _dg(rt[i], bh[i], NT), 0.0).astype(BF16) for i in it]
    tinv = [eye_c - jnp.where(((ti >> 1) == (tj >> 1)), a_b[i], 0.0) for i in it]
    blk = 2
    while blk < C:
        sh = blk.bit_length() - 1
        lvl = ((ti >> (sh + 1)) == (tj >> (sh + 1))) & (((ti >> sh) & 1) == 1) & (((tj >> sh) & 1) == 0)
        t16 = [tinv[i].astype(BF16) for i in it]
        mt = [_dg(jnp.where(lvl, a_b[i], 0.0).astype(BF16), t16[i]).astype(BF16) for i in it]
        tinv = [tinv[i] - _dg(t16[i], mt[i]) for i in it]
        blk *= 2
    t16 = [tinv[i].astype(BF16) for i in it]
    akv = [_dg(a_k[i], vv[i]).astype(BF16) for i in it]
    x = [_dg(t16[i], jnp.concatenate([kt[i], akv[i]], axis=1)).astype(BF16) for i in it]
    qw = [_dg(q_b[i], x[i]) for i in it]
    r2 = [(rt_all[rows[i], sls[i]] - qw[i][:, :N]).astype(BF16) for i in it]
    y0 = [_dg(q_k[i], vv[i]) - qw[i][:, N:] for i in it]
    bw = [_dg(bb[i], x[i], TN) for i in it]
    gmat = [(jnp.where(eye_n, jnp.broadcast_to(p_last[g * C:g * C + 1, sls[i]], (N, N)), 0.0)
             - bw[i][:, :N]).astype(BF16) for i, (g, h) in enumerate(items)]
    h_add = [_dg(kb[i], vv[i], TN) - bw[i][:, N:] for i in it]
    state = [h_scr[s * H + h] for s in range(S) for h in range(H)]
    y = [None] * len(items)
    for i, (g, h) in enumerate(items):
        si = h if chain else g * H + h
        h0 = state[si].astype(BF16)
        y[i] = _dg(r2[i], h0) + y0[i]
        state[si] = _dg(gmat[i], h0) + h_add[i]
    for si in range(S * H):
        h_scr[si] = state[si]
    for i in it:
        rs, sl = rows[i], sls[i]
        mean = jnp.mean(y[i], axis=-1, keepdims=True)
        yc = y[i] - mean
        var = jnp.mean(yc * yc, axis=-1, keepdims=True)
        yn = yc * lax.rsqrt(var + GN_EPS)
        yn = yn * lnw_ref[:, sl] + lnb_ref[:, sl]
        bonus = jnp.sum(rkr[rs, sl], axis=-1, keepdims=True) * v_all[rs, sl]
        o_ref[rs, sl] = ((yn + bonus) * g_ref[rs, sl]).astype(o_ref.dtype)

    @pl.when(c_idx == nsteps - 1)
    def _():
        for s in range(S):
            for h in range(H):
                hT_ref[s, h] = _transpose_exact(h_scr[s * H + h], eye_bf) if xpose else h_scr[s * H + h]


def _wkv(r, lw, k, v, kn, b, g, r_k, lnx_w, lnx_b, h0, *, nseq, T, C, G=None):
    M = r.shape[0]
    nchunks = T // C
    HN = MIX_A
    chain = nchunks > 1
    if G is None:
        G = 2 if chain else 4
    if chain:
        nsteps, S, grid = nchunks // G, 1, (nseq, nchunks // G)
        row = pl.BlockSpec((G * C, HN), lambda s, c: (s * nsteps + c, 0))
        h_in, out_dtype = h0, BF16
    else:
        nsteps, S, grid = 1, G, (nseq // G, 1)
        row = pl.BlockSpec((G * C, HN), lambda s, c: (s, 0))
        h_in, out_dtype = jnp.swapaxes(h0, -1, -2), F32
    vec = pl.BlockSpec((1, HN), lambda s, c: (0, 0))
    st = pl.BlockSpec((S, H_A, HEAD_A, HEAD_A), lambda s, c: (s, 0, 0, 0))
    o, h_new = pl.pallas_call(
        functools.partial(_wkv_kernel, C=C, G=G, H=H_A, N=HEAD_A, nsteps=nsteps, chain=chain, xpose=chain),
        out_shape=(jax.ShapeDtypeStruct((M, HN), out_dtype),
                   jax.ShapeDtypeStruct((nseq, H_A, HEAD_A, HEAD_A), F32)),
        grid=grid,
        in_specs=[row] * 7 + [vec] * 3 + [st],
        out_specs=(row, st),
        scratch_shapes=[pltpu.VMEM((S * H_A, HEAD_A, HEAD_A), F32)],
        compiler_params=_cparams(2),
        name="wkv7",
    )(r, lw, k, v, kn, b, g, r_k.reshape(1, HN), lnx_w.reshape(1, HN), lnx_b.reshape(1, HN), h_in)
    return o, (h_new if chain else jnp.swapaxes(h_new, -1, -2))


def _mem_attn_kernel(q_ref, k_ref, v_ref, o_ref, *, nbb, tq):
    c = MEM_SCALE * math.log2(math.e)
    lane_head = lax.broadcasted_iota(jnp.int32, (1, MEM_WIDTH), 1) >> 6
    items = [(bb, h) for bb in range(nbb) for h in range(MEM_HEADS)]
    q_all = q_ref[...].astype(F32)
    qs = [q_all[bb * tq:(bb + 1) * tq, :] for bb in range(nbb)]
    kts = [k_ref[0, bb].astype(BF16) for bb in range(nbb)]
    vts = [v_ref[0, bb].astype(BF16) for bb in range(nbb)]
    ss = [_dg(jnp.where(lane_head == h, qs[bb], 0.0).astype(BF16), kts[bb]) for bb, h in items]
    ms = [jnp.max(s, axis=-1, keepdims=True) for s in ss]
    ps = [jnp.exp2((s - m) * c) for s, m in zip(ss, ms)]
    inv_l = [1.0 / jnp.sum(p, axis=-1, keepdims=True) for p in ps]
    pv = [_dg(p.astype(BF16), vts[bb], NT) for p, (bb, h) in zip(ps, items)]
    outs = []
    for bb in range(nbb):
        o = jnp.zeros((tq, MEM_WIDTH), F32)
        for h in range(MEM_HEADS):
            idx = bb * MEM_HEADS + h
            o = jnp.where(lane_head == h, pv[idx] * inv_l[idx], o)
        outs.append(o)
    o_ref[...] = (outs[0] if nbb == 1 else jnp.concatenate(outs, axis=0)).astype(o_ref.dtype)


def _mem_attn(q, mem_k, mem_v, layer, *, nb, T):
    M = q.shape[0]
    if T >= ROW_TILES[0]:
        tq, nbb = ROW_TILES[0], 1
    else:
        tq, nbb = T, _pick_tile(nb, (16, 8, 4, 2, 1))
    nq = T // tq
    n_mem = mem_k.shape[3]
    return pl.pallas_call(
        functools.partial(_mem_attn_kernel, nbb=nbb, tq=tq),
        out_shape=jax.ShapeDtypeStruct((M, MEM_WIDTH), BF16),
        grid=(nb // nbb, nq),
        in_specs=[pl.BlockSpec((nbb * tq, MEM_WIDTH), lambda b, i: (b * nq + i, 0)),
                  pl.BlockSpec((1, nbb, MEM_WIDTH, n_mem), lambda b, i: (layer, b, 0, 0)),
                  pl.BlockSpec((1, nbb, MEM_WIDTH, n_mem), lambda b, i: (layer, b, 0, 0))],
        out_specs=pl.BlockSpec((nbb * tq, MEM_WIDTH), lambda b, i: (b * nq + i, 0)),
        compiler_params=_cparams(2),
        name="mem_attn",
    )(q, mem_k, mem_v)


def _ffn_kernel(*refs, tm, T, long_mode, col_chunks, final_norm):
    (x_ref, ot_ref, om_ref, wot_ref, wom_ref, g_ref, wg_ref, wv_ref, cw_ref, cb_ref, wd_ref, cp_ref) = refs[:12]
    k = 12
    fin_ref = refs[k] if final_norm else None
    k += int(final_norm)
    o_ref, aux_ref = refs[k:k + 2]
    k += 2
    y_ref = refs[k] if final_norm else None
    k += int(final_norm)
    xn_scr, h_scr, x1_scr = refs[k:k + 3]
    if long_mode:
        carry_scr = refs[k + 3]
        tail_ref = aux_ref
    else:
        gate_ref = aux_ref
    x1_scr[...] = x_ref[...] + (_dg(ot_ref[...].astype(BF16), wot_ref[...]) + _dg(om_ref[...], wom_ref[...]))
    xn_scr[...] = _rms(x1_scr[...], g_ref[...]).astype(BF16)
    xn = xn_scr[...]
    row = lax.broadcasted_iota(jnp.int32, (tm, 1), 0)
    if long_mode:
        nblk = T // tm

        @pl.when(pl.program_id(0) % nblk == 0)
        def _():
            carry_scr[...] = cp_ref[0]
    else:
        t_in_seq = row % T
        ncp = cp_ref.shape[0]
        rr = lax.broadcasted_iota(jnp.int32, (tm, ncp), 0)
        cc = lax.broadcasted_iota(jnp.int32, (tm, ncp), 1)
        same_seq = (rr // T) == (cc >> 1)
        sel1 = jnp.where(same_seq & ((rr % T) + 1 == (cc & 1)), 1.0, 0.0).astype(BF16)
        sel2 = jnp.where(same_seq & ((rr % T) == (cc & 1)), 1.0, 0.0).astype(BF16)
        sel = jnp.concatenate([sel1, sel2], axis=0)
    for (lo, w) in col_chunks:
        cs = slice(lo, lo + w)
        gate = _dg(xn, wg_ref[:, cs])
        val = _dg(xn, wv_ref[:, cs])
        r1 = pltpu.roll(gate, 1, 0)
        r2 = pltpu.roll(gate, 2, 0)
        if long_mode:
            c0 = carry_scr[6:7, cs]
            c1 = carry_scr[7:8, cs]
            p1 = jnp.where(row == 0, c1, r1)
            p2 = jnp.where(row == 0, c0, jnp.where(row == 1, c1, r2))
            tail = gate[tm - 8:tm, :]
            carry_scr[:, cs] = tail
            tail_ref[0, :, cs] = tail
        else:
            cpc = cp_ref[:, cs]
            hi = cpc.astype(BF16)
            r_hi = cpc - hi.astype(F32)
            mid = r_hi.astype(BF16)
            lo = (r_hi - mid.astype(F32)).astype(BF16)
            edge = (_dg(sel, hi) + _dg(sel, mid)) + _dg(sel, lo)
            p1 = jnp.where(t_in_seq >= 1, r1, 0.0) + edge[:tm]
            p2 = jnp.where(t_in_seq >= 2, r2, 0.0) + edge[tm:]
            gate_ref[:, cs] = gate
        conv = cb_ref[:, cs] + p2 * cw_ref[0:1, cs]
        conv = conv + p1 * cw_ref[1:2, cs]
        conv = conv + gate * cw_ref[2:3, cs]
        h_scr[:, cs] = (conv * jax.nn.sigmoid(conv) * val).astype(BF16)
    out = x1_scr[...] + _dg(h_scr[...], wd_ref[...])
    o_ref[...] = out
    if final_norm:
        y_ref[...] = _rms(out, fin_ref[...])


def _ffn(x, o_tok, o_mem, w_o_tok, w_o_mem, g, w_gate, w_val, conv_w, conv_b, w_down, conv_prev, final_g=None, *,
         nseq, T):
    M = x.shape[0]
    long_mode = T >= 512
    tm = 512 if long_mode else 256
    assert (T % tm == 0) if long_mode else (tm % T == 0 and M % tm == 0)
    col_chunks = tuple((lo, min(256, D_FF - lo)) for lo in range(0, D_FF, 256))
    const = lambda a: pl.BlockSpec(a.shape, lambda i: (0,) * a.ndim, pipeline_mode=pl.Buffered(1))
    small = lambda a: pl.BlockSpec(a.shape, lambda i: (0,) * a.ndim)
    g2 = g.reshape(1, D_MODEL).astype(F32)
    cb2 = conv_b.reshape(1, D_FF).astype(F32)
    rows = lambda w: pl.BlockSpec((tm, w), lambda i: (i, 0))
    common_specs = [rows(D_MODEL), rows(MIX_A), rows(MEM_WIDTH), const(w_o_tok), const(w_o_mem), small(g2),
                    const(w_gate), const(w_val), small(conv_w), small(cb2), const(w_down)]
    common_args = [x, o_tok, o_mem, w_o_tok, w_o_mem, g2, w_gate, w_val, conv_w, cb2, w_down]
    final_norm = final_g is not None
    fin_args = [final_g.reshape(1, D_MODEL).astype(F32)] if final_norm else []
    fin_specs = [pl.BlockSpec((1, D_MODEL), lambda i: (0, 0))] if final_norm else []
    fin_out = [jax.ShapeDtypeStruct((M, D_MODEL), F32)] if final_norm else []
    fin_out_specs = [rows(D_MODEL)] if final_norm else []
    base_scratch = [pltpu.VMEM((tm, D_MODEL), BF16), pltpu.VMEM((tm, D_FF), BF16), pltpu.VMEM((tm, D_MODEL), F32)]
    kern = functools.partial(_ffn_kernel, tm=tm, T=T, long_mode=long_mode, col_chunks=col_chunks,
                             final_norm=final_norm)
    if long_mode:
        nblk = T // tm
        cp = jnp.concatenate([jnp.zeros((nseq, 6, D_FF), F32), conv_prev.astype(F32)], axis=1)
        res = pl.pallas_call(
            kern,
            out_shape=[jax.ShapeDtypeStruct((M, D_MODEL), F32), jax.ShapeDtypeStruct((M // tm, 8, D_FF), F32)] + fin_out,
            grid=(M // tm,),
            in_specs=common_specs + [pl.BlockSpec((1, 8, D_FF), lambda i: (i // nblk, 0, 0))] + fin_specs,
            out_specs=[rows(D_MODEL), pl.BlockSpec((1, 8, D_FF), lambda i: (i, 0, 0))] + fin_out_specs,
            scratch_shapes=base_scratch + [pltpu.VMEM((8, D_FF), F32)],
            compiler_params=_cparams(1),
            name="convglu_long",
        )(*common_args, cp, *fin_args)
        new_conv = res[1].reshape(nseq, nblk, 8, D_FF)[:, -1, 6:8, :]
        return res[0], new_conv, (res[2] if final_norm else None)
    ncp = (CONV_W - 1) * (tm // T)
    res = pl.pallas_call(
        kern,
        out_shape=[jax.ShapeDtypeStruct((M, D_MODEL), F32), jax.ShapeDtypeStruct((M, D_FF), F32)] + fin_out,
        grid=(M // tm,),
        in_specs=common_specs + [pl.BlockSpec((ncp, D_FF), lambda i: (i, 0))] + fin_specs,
        out_specs=[rows(D_MODEL), rows(D_FF)] + fin_out_specs,
        scratch_shapes=base_scratch,
        compiler_params=_cparams(1),
        name="convglu_short",
    )(*common_args, conv_prev.astype(F32).reshape(nseq * (CONV_W - 1), D_FF), *fin_args)
    new_conv = res[1].reshape(nseq, T, D_FF)[:, T - 2:, :]
    return res[0], new_conv, (res[2] if final_norm else None)


def _rope128(x128, cs):
    a = x128 * cs
    s = a + pltpu.roll(a, 64, 1)
    lane = lax.broadcasted_iota(jnp.int32, a.shape, 1)
    return jnp.where(lane < 64, s, 0.0)


def _mla_kv_kernel(*refs, expand):
    x_ref, gk_ref, wa_ref, gc_ref, cs_ref = refs[:5]
    k = 5
    wx_ref = refs[k] if expand else None
    k += int(expand)
    ckv_ref, kpe_ref = refs[k:k + 2]
    kvx_ref = refs[k + 2] if expand else None
    xn = _rms(x_ref[...], gk_ref[...]).astype(BF16)
    h = _dg(xn, wa_ref[...])
    ckv = _rms(h[:, :KV_LORA], gc_ref[...])
    rp = _rope128(h[:, KV_LORA:], cs_ref[...])
    ckv_ref[...] = ckv
    kpe_ref[...] = rp[:, :ROPE_DIM]
    if expand:
        ckpe = jnp.concatenate([ckv, rp], axis=1).astype(BF16)
        for lo in range(0, wx_ref.shape[1], 768):
            kvx_ref[:, lo:lo + 768] = _dg(ckpe, wx_ref[:, lo:lo + 768]).astype(kvx_ref.dtype)


def _mla_kv(x, norm_kv, w_kva, norm_ckv, cs, w_kvx, *, T):
    M = x.shape[0]
    tm = _pos_tile(M, T)
    ncs = cs.shape[0] // tm
    expand = w_kvx is not None
    const = lambda a: pl.BlockSpec(a.shape, lambda i: (0,) * a.ndim)
    rows = lambda w: pl.BlockSpec((tm, w), lambda i: (i, 0))
    gk = norm_kv.reshape(1, D_MODEL).astype(F32)
    gc = norm_ckv.reshape(1, KV_LORA).astype(F32)
    args = [x, gk, w_kva, gc, cs] + ([w_kvx] if expand else [])
    in_specs = [rows(D_MODEL), const(gk), const(w_kva), const(gc),
                pl.BlockSpec((tm, 128), lambda i: (i % ncs, 0))] + ([const(w_kvx)] if expand else [])
    out_shape = [jax.ShapeDtypeStruct((M, KV_LORA), F32), jax.ShapeDtypeStruct((M, ROPE_DIM), F32)]
    out_specs = [rows(KV_LORA), rows(ROPE_DIM)]
    if expand:
        out_shape.append(jax.ShapeDtypeStruct((M, w_kvx.shape[1]), BF16))
        out_specs.append(rows(w_kvx.shape[1]))
    return pl.pallas_call(
        functools.partial(_mla_kv_kernel, expand=expand),
        out_shape=out_shape,
        grid=(M // tm,),
        in_specs=in_specs,
        out_specs=out_specs,
        compiler_params=_cparams(1),
        name="mla_kv",
    )(*args)


def _mla_q_kernel(x_ref, gm_ref, win_ref, gq_ref, wq_ref, cs_ref, q_ref, qm_ref):
    xn = _rms(x_ref[...], gm_ref[...]).astype(BF16)
    proj = _dg(xn, win_ref[...])
    qm_ref[...] = proj[:, Q_LORA:].astype(qm_ref.dtype)
    qn = _rms(proj[:, :Q_LORA], gq_ref[...]).astype(BF16)
    cs = cs_ref[...]
    for h in range(H_B):
        base = h * 256
        qh = _dg(qn, wq_ref[:, base:base + 256])
        q_ref[:, base:base + 128] = qh[:, :128].astype(q_ref.dtype)
        q_ref[:, base + 128:base + 256] = _rope128(qh[:, 128:], cs).astype(q_ref.dtype)


def _mla_q(x, g_mix, w_in, norm_q, w_q, cs, *, T):
    M = x.shape[0]
    tm = _pos_tile(M, T)
    ncs = cs.shape[0] // tm
    const = lambda a: pl.BlockSpec(a.shape, lambda i: (0,) * a.ndim)
    gm = g_mix.reshape(1, D_MODEL).astype(F32)
    gq = norm_q.reshape(1, Q_LORA).astype(F32)
    return pl.pallas_call(
        _mla_q_kernel,
        out_shape=(jax.ShapeDtypeStruct((M, H_B * 256), BF16), jax.ShapeDtypeStruct((M, MEM_WIDTH), BF16)),
        grid=(M // tm,),
        in_specs=[pl.BlockSpec((tm, D_MODEL), lambda i: (i, 0)), const(gm), const(w_in), const(gq), const(w_q),
                  pl.BlockSpec((tm, 128), lambda i: (i % ncs, 0))],
        out_specs=(pl.BlockSpec((tm, H_B * 256), lambda i: (i, 0)), pl.BlockSpec((tm, MEM_WIDTH), lambda i: (i, 0))),
        compiler_params=_cparams(1),
        name="mla_q",
    )(x, gm, w_in, gq, w_q, cs)


def _flash_kernel(q_ref, k_ref, v_ref, o_ref, m_scr, l_scr, acc_scr, *, tq, tk):
    i = pl.program_id(1)
    j = pl.program_id(2)

    @pl.when(j == 0)
    def _():
        m_scr[...] = jnp.full(m_scr.shape, -jnp.inf, F32)
        l_scr[...] = jnp.zeros(l_scr.shape, F32)
        acc_scr[...] = jnp.zeros(acc_scr.shape, F32)

    hs = range(H_B)

    def block(masked):
        c = ATTN_SCALE * math.log2(math.e)
        ss = [_dg(q_ref[:, h * 256:(h + 1) * 256], k_ref[:, h * 256:(h + 1) * 256], NT) for h in hs]
        if masked:
            keep = lax.broadcasted_iota(jnp.int32, (tq, tk), 1) <= lax.broadcasted_iota(jnp.int32, (tq, tk), 0)
            ss = [jnp.where(keep, s, -jnp.inf) for s in ss]
        nkt = tk // 128
        tiles = [[ss[h][:, t * 128:(t + 1) * 128] for t in range(nkt)] for h in hs]
        m_prev = [m_scr[h] for h in hs]
        m_new = []
        for h in hs:
            tmax = tiles[h][0]
            for t in range(1, nkt):
                tmax = jnp.maximum(tmax, tiles[h][t])
            m_new.append(jnp.maximum(m_prev[h], jnp.max(tmax, axis=-1, keepdims=True)))
        alpha = [jnp.exp2((m_prev[h] - m_new[h]) * c) for h in hs]
        ps = [[jnp.exp2((tiles[h][t] - m_new[h]) * c) for t in range(nkt)] for h in hs]
        pv = [_dg(jnp.concatenate([p.astype(BF16) for p in ps[h]], axis=1), v_ref[:, h * V_DIM:(h + 1) * V_DIM])
              for h in hs]
        for h in hs:
            psum = ps[h][0]
            for t in range(1, nkt):
                psum = psum + ps[h][t]
            l_scr[h] = alpha[h] * l_scr[h] + jnp.sum(psum, axis=-1, keepdims=True)
            acc_scr[:, h * V_DIM:(h + 1) * V_DIM] = alpha[h] * acc_scr[:, h * V_DIM:(h + 1) * V_DIM] + pv[h]
            m_scr[h] = m_new[h]

    @pl.when(j < i)
    def _():
        block(False)

    @pl.when(j == i)
    def _():
        block(True)
        for h in hs:
            sl = slice(h * V_DIM, (h + 1) * V_DIM)
            o_ref[:, sl] = (acc_scr[:, sl] / l_scr[h]).astype(o_ref.dtype)


def _flash_causal(q, kv, *, nb, T):
    M = q.shape[0]
    tq = tk = 512
    nq = T // tq
    kw = H_B * 256
    vw = H_B * V_DIM
    return pl.pallas_call(
        functools.partial(_flash_kernel, tq=tq, tk=tk),
        out_shape=jax.ShapeDtypeStruct((M, vw), BF16),
        grid=(nb, nq, nq),
        in_specs=[pl.BlockSpec((tq, kw), lambda b, i, j: (b * nq + i, 0)),
                  pl.BlockSpec((tk, kw), lambda b, i, j: (b * nq + jnp.minimum(i, j), 0)),
                  pl.BlockSpec((tk, vw), lambda b, i, j: (b * nq + jnp.minimum(i, j), kw // vw))],
        out_specs=pl.BlockSpec((tq, vw), lambda b, i, j: (b * nq + i, 0)),
        scratch_shapes=[pltpu.VMEM((H_B, tq, 128), F32), pltpu.VMEM((H_B, tq, 128), F32), pltpu.VMEM((tq, vw), F32)],
        compiler_params=_cparams(3),
        name="mla_causal",
    )(q, kv, kv)


def _decode_kernel(pt_ref, q_ref, ckn_ref, kpn_ref, wuk_ref, wuv_ref, ck_hbm, kp_hbm, o_ref,
                   ck_buf, kp_buf, sem, m_scr, l_scr, acc_scr, *, PP, n_steps, n_total, TS):
    j = pl.program_id(1)
    n = pl.program_id(0) * n_steps + j
    slot = lax.rem(n, 2)
    R = H_B * TS

    def group_copies(step, slot_, for_wait):
        cps = []
        for p in range(PP):
            page = 0 if for_wait else pt_ref[step * PP + p]
            cps.append(pltpu.make_async_copy(ck_hbm.at[page], ck_buf.at[slot_, p], sem.at[slot_, 0]))
            cps.append(pltpu.make_async_copy(kp_hbm.at[page], kp_buf.at[slot_, p], sem.at[slot_, 1]))
        return cps

    @pl.when(n == 0)
    def _():
        for cp in group_copies(0, 0, False):
            cp.start()

    nxt = jnp.minimum(n + 1, n_total - 1)
    for cp in group_copies(nxt, 1 - slot, False):
        cp.start()
    for cp in group_copies(n, slot, True):
        cp.wait()

    @pl.when(j == 0)
    def _():
        m_scr[...] = jnp.full(m_scr.shape, -jnp.inf, F32)
        l_scr[...] = jnp.zeros(l_scr.shape, F32)
        acc_scr[...] = jnp.zeros(acc_scr.shape, F32)

    q_all = q_ref[0].astype(F32)
    q_lat = jnp.concatenate(
        [_dg(q_all[:, h * 256:h * 256 + NOPE_DIM].astype(BF16), wuk_ref[h]) for h in range(H_B)], axis=0).astype(BF16)
    q_pe = jnp.concatenate(
        [q_all[:, h * 256 + NOPE_DIM:h * 256 + NOPE_DIM + ROPE_DIM] for h in range(H_B)], axis=0).astype(BF16)
    c = ATTN_SCALE * math.log2(math.e)

    cks = [ck_buf[slot, p].astype(BF16) for p in range(PP)]
    kps = [kp_buf[slot, p].astype(BF16) for p in range(PP)]
    ss = [_dg(q_lat, cks[p], NT) + _dg(q_pe, kps[p]) for p in range(PP)]
    pad = PAGE_SIZE - TS
    ckn = jnp.concatenate([ckn_ref[0], jnp.zeros((pad, KV_LORA), F32)], axis=0).astype(BF16)
    kpn = jnp.concatenate([kpn_ref[0], jnp.zeros((pad, ROPE_DIM), F32)], axis=0).astype(BF16)
    t_q = lax.broadcasted_iota(jnp.int32, (R, PAGE_SIZE), 0) % TS
    col = lax.broadcasted_iota(jnp.int32, (R, PAGE_SIZE), 1)
    last_shift = jnp.where(j == n_steps - 1, 0, -PAGE_SIZE)
    s_new = jnp.where(col <= t_q + last_shift, _dg(q_lat, ckn, NT) + _dg(q_pe, kpn, NT), -jnp.inf)
    ss.append(s_new)
    cks.append(ckn)

    tile_max = ss[0]
    for s in ss[1:]:
        tile_max = jnp.maximum(tile_max, s)
    m_prev = m_scr[...]
    m_new = jnp.maximum(m_prev, jnp.max(tile_max, axis=-1, keepdims=True))
    alpha = jnp.exp2((m_prev - m_new) * c)
    ps = [jnp.exp2((s - m_new) * c) for s in ss]
    p_sum = ps[0]
    for p in ps[1:]:
        p_sum = p_sum + p
    acc = jnp.concatenate([alpha, alpha], axis=1) * acc_scr[...]
    for p, vv in zip(ps, cks):
        acc = acc + _dg(p.astype(BF16), vv)
    m_scr[...] = m_new
    l_scr[...] = alpha * l_scr[...] + jnp.sum(p_sum, axis=-1, keepdims=True)
    acc_scr[...] = acc

    @pl.when(j == n_steps - 1)
    def _():
        o_lat = acc_scr[...] / jnp.concatenate([l_scr[...], l_scr[...]], axis=1)
        for h in range(H_B):
            o_h = o_lat[h * TS:(h + 1) * TS, :].astype(BF16)
            o_ref[0, :, h * V_DIM:(h + 1) * V_DIM] = _dg(o_h, wuv_ref[h]).astype(o_ref.dtype)

    @pl.when(n == n_total - 1)
    def _():
        for cp in group_copies(n, 1 - slot, True):
            cp.wait()


def _decode_attn(page_table, q, cache_ckv, cache_kpe, ckv_new, kpe_new, w_ukT, w_uv, *, PP=64):
    nb, n_pages = page_table.shape
    TS = ckv_new.shape[1]
    R = H_B * TS
    n_steps = n_pages // PP
    pt = page_table.reshape(-1).astype(jnp.int32)
    kpe_t = jnp.swapaxes(cache_kpe, 1, 2)
    grid_spec = pltpu.PrefetchScalarGridSpec(
        num_scalar_prefetch=1,
        grid=(nb, n_steps),
        in_specs=[pl.BlockSpec((1, TS, H_B * 256), lambda b, j, pt_ref: (b, 0, 0)),
                  pl.BlockSpec((1, TS, KV_LORA), lambda b, j, pt_ref: (b, 0, 0)),
                  pl.BlockSpec((1, TS, ROPE_DIM), lambda b, j, pt_ref: (b, 0, 0)),
                  pl.BlockSpec((H_B, NOPE_DIM, KV_LORA), lambda b, j, pt_ref: (0, 0, 0)),
                  pl.BlockSpec((H_B, KV_LORA, V_DIM), lambda b, j, pt_ref: (0, 0, 0)),
                  pl.BlockSpec(memory_space=pl.ANY),
                  pl.BlockSpec(memory_space=pl.ANY)],
        out_specs=pl.BlockSpec((1, TS, H_B * V_DIM), lambda b, j, pt_ref: (b, 0, 0)),
        scratch_shapes=[pltpu.VMEM((2, PP, PAGE_SIZE, KV_LORA), F32),
                        pltpu.VMEM((2, PP, ROPE_DIM, PAGE_SIZE), F32),
                        pltpu.SemaphoreType.DMA((2, 2)),
                        pltpu.VMEM((R, 128), F32), pltpu.VMEM((R, 128), F32), pltpu.VMEM((R, KV_LORA), F32)],
    )
    return pl.pallas_call(
        functools.partial(_decode_kernel, PP=PP, n_steps=n_steps, n_total=nb * n_steps, TS=TS),
        out_shape=jax.ShapeDtypeStruct((nb, TS, H_B * V_DIM), BF16),
        grid_spec=grid_spec,
        compiler_params=_cparams(2),
        name="mla_decode",
    )(pt, q, ckv_new, kpe_new, w_ukT, w_uv, cache_ckv, kpe_t)


def _rope_table(pos):
    half = ROPE_DIM // 2
    inv = ROPE_BASE ** (-jnp.arange(half, dtype=F32) / half)
    ang = pos.astype(F32)[:, None] * inv[None, :]
    cos, sin = jnp.cos(ang), jnp.sin(ang)
    return jnp.concatenate([cos, cos, -sin, sin], axis=-1)


def _swap_halves(w):
    half = ROPE_DIM // 2
    return jnp.concatenate([w[..., half:], w[..., :half]], axis=-1)


def _prep_rwkv_weights(l, P):
    w_in = P["w_in_a"][l]
    mu = P["mu_a"][l]
    cuts = [0, MIX_A, MIX_A + DECAY_LORA, 2 * MIX_A + DECAY_LORA, 3 * MIX_A + DECAY_LORA,
            3 * MIX_A + DECAY_LORA + A_LORA, C_RWKV, C_A]
    seg = lambda a, i: a[..., cuts[i]:cuts[i + 1]]
    r_w, wl_w, k_w, v_w, al_w, gl_w, qm_w = [seg(w_in, i) for i in range(7)]
    r_m, wl_m, k_m, v_m, al_m, gl_m = [seg(mu, i) for i in range(6)]
    zc = lambda n: jnp.zeros((D_MODEL, n), F32)
    zm = lambda n: jnp.zeros((n,), F32)
    if l > 0:
        vl_w, vl_m = P["w_vres_in"][l - 1], P["mu_vres"][l - 1]
        wv_up, v0 = P["w_vres_up"][l - 1], P["v0"][l - 1]
    else:
        vl_w, vl_m = zc(VRES_LORA), zm(VRES_LORA)
        wv_up, v0 = jnp.zeros((VRES_LORA, MIX_A), F32), zm(MIX_A)
    w_full = jnp.concatenate([r_w, k_w, v_w, wl_w, al_w, gl_w, vl_w, zc(64), qm_w, zc(128)], axis=1)
    mu_full = jnp.concatenate([r_m, k_m, v_m, wl_m, al_m, gl_m, vl_m, zm(64), zm(MEM_WIDTH), zm(128)])
    zr = lambda n: jnp.zeros((n, MIX_A), F32)
    return dict(
        w_in=w_full.astype(BF16),
        mu=mu_full.reshape(1, PA_COLS),
        wd=jnp.concatenate([P["w_decay_up"][l], zr(A_LORA)], axis=0).astype(BF16),
        wa=jnp.concatenate([zr(DECAY_LORA), P["w_a_up"][l]], axis=0).astype(BF16),
        wg=jnp.concatenate([P["w_g_up"][l], zr(256 - GATE_LORA)], axis=0).astype(BF16),
        wv=jnp.concatenate([zr(GATE_LORA), wv_up, zr(256 - GATE_LORA - VRES_LORA)], axis=0).astype(BF16),
        w0=P["w0"][l].reshape(1, MIX_A), a0=P["a0"][l].reshape(1, MIX_A), v0=v0.reshape(1, MIX_A),
        k_k=P["k_k"][l].reshape(1, MIX_A), k_a=P["k_a"][l].reshape(1, MIX_A),
    )


def _prep_mla_weights(P):
    w_kv_a = P["w_kv_a"]
    w_kva = jnp.concatenate([w_kv_a, _swap_halves(w_kv_a[:, KV_LORA:])], axis=1).astype(BF16)
    w_kv_b = P["w_kv_b"]
    w_uk, w_uv = w_kv_b[..., :NOPE_DIM], w_kv_b[..., NOPE_DIM:]
    k_part = jnp.zeros((KV_LORA + 128, H_B, 256), F32)
    k_part = k_part.at[:KV_LORA, :, :NOPE_DIM].set(w_uk)
    eye = jnp.eye(ROPE_DIM, dtype=F32)
    k_part = k_part.at[KV_LORA:KV_LORA + ROPE_DIM, :, NOPE_DIM:NOPE_DIM + ROPE_DIM].set(
        jnp.broadcast_to(eye[:, None, :], (ROPE_DIM, H_B, ROPE_DIM)))
    v_part = jnp.zeros((KV_LORA + 128, H_B, V_DIM), F32).at[:KV_LORA].set(w_uv)
    w_kvx = jnp.concatenate([k_part.reshape(KV_LORA + 128, H_B * 256), v_part.reshape(KV_LORA + 128, H_B * V_DIM)],
                            axis=1).astype(BF16)
    w_q = []
    for j in range(N_B):
        wq = P["w_q_b"][j].reshape(Q_LORA, H_B, NOPE_DIM + ROPE_DIM)
        rope_w = wq[..., NOPE_DIM:]
        w_q.append(jnp.concatenate([wq[..., :NOPE_DIM], rope_w, _swap_halves(rope_w)], axis=-1)
                   .reshape(Q_LORA, H_B * 256).astype(BF16))
    return dict(
        w_kva=w_kva, w_kvx=w_kvx, w_q=w_q,
        w_ukT=jnp.transpose(w_uk, (1, 2, 0)).astype(BF16),
        w_uv=jnp.transpose(w_uv, (1, 0, 2)).astype(BF16),
    )


def _trunk(x, pos, mem_k, mem_v, wkv0, shift0, conv0, past, P, W, *, nseq, T):
    M = nseq * T
    x = x.reshape(M, D_MODEL)
    cs = _rope_table(pos)
    if T < ROW_TILES[0]:
        cs = jnp.tile(cs, (nseq, 1))
    chunk = 64 if T >= 64 else T
    new_wkv, new_shift, new_conv = [], [], []
    v_first = None
    ckv = kpe = ckpe = kvx = None
    for l in range(DEPTH):
        g_mix = P["norm_mix"][l]
        if l < N_A:
            pw = W["rwkv"][l]
            x_last = x.reshape(nseq, T, D_MODEL)[:, -1]
            new_shift.append(_rmsnorm(x_last, g_mix, name="shift_norm"))
            sp = _mm([shift0[l].astype(F32)], [pw["w_in"]], name="rwkv_in_shift")
            r, lw, k, v, kn, b, g, q_mem = _rwkv_prep(x, sp, pw, v_first if l > 0 else None, g_mix, nseq=nseq, T=T)
            if l == 0:
                v_first = v
            h0 = wkv0[l].astype(F32)
            o_tok, h_new = _wkv(r, lw, k, v, kn, b, g, P["r_k"][l], P["lnx_w"][l], P["lnx_b"][l], h0,
                                nseq=nseq, T=T, C=chunk)
            new_wkv.append(h_new)
        else:
            j = l - N_A
            if l == N_A:
                res = _mla_kv(x, P["norm_kv"], W["mla"]["w_kva"], P["norm_ckv"], cs,
                              W["mla"]["w_kvx"] if past is None else None, T=T)
                ckv, kpe = res[0], res[1]
                kvx = res[2] if past is None else None
            q, q_mem = _mla_q(x, g_mix, W["w_in_b"][j], P["norm_q"][j], W["mla"]["w_q"][j], cs, T=T)
            if past is None:
                o_tok = _flash_causal(q, kvx, nb=nseq, T=T)
            else:
                o_tok = _decode_attn(past[2], q.reshape(nseq, T, H_B * 256), past[0], past[1],
                                     ckv.reshape(nseq, T, KV_LORA), kpe.reshape(nseq, T, ROPE_DIM),
                                     W["mla"]["w_ukT"], W["mla"]["w_uv"]).reshape(M, H_B * V_DIM)
        o_mem = _mem_attn(q_mem, mem_k, mem_v, l, nb=nseq, T=T)
        x, c_new, y = _ffn(x, o_tok, o_mem, W["w_o_tok"][l], W["w_o_mem"][l], P["norm_ffn"][l], W["w_gate"][l],
                           W["w_val"][l], P["conv_w"][l], P["conv_b"][l], W["w_down"][l], conv0[l],
                           P["final_norm"] if l == DEPTH - 1 else None, nseq=nseq, T=T)
        new_conv.append(c_new)
    return (y.reshape(nseq, T, D_MODEL), ckv.reshape(nseq, T, KV_LORA), kpe.reshape(nseq, T, ROPE_DIM),
            jnp.stack(new_wkv), jnp.stack(new_shift), jnp.stack(new_conv))


def kernel(x_prompt, x_sample, cache_ckv, cache_kpe, cache_mem_k, cache_mem_v, state_wkv, state_shift, state_conv, page_table, mem_prompt, norm_mix, norm_ffn, norm_mem, w_mem_kv, w_o, w_in_a, mu_a, w_vres_in, mu_vres, w_decay_up, w0, w_a_up, a0, w_g_up, w_vres_up, v0, k_k, k_a, r_k, lnx_w, lnx_b, norm_kv, w_kv_a, norm_ckv, w_kv_b, w_in_b, norm_q, w_q_b, w_ffn_up, conv_w, conv_b, w_ffn_down, final_norm):
    P = dict(norm_mix=norm_mix, norm_ffn=norm_ffn, w_o=w_o, w_in_a=w_in_a, mu_a=mu_a, w_vres_in=w_vres_in,
             mu_vres=mu_vres, w_decay_up=w_decay_up, w0=w0, w_a_up=w_a_up, a0=a0, w_g_up=w_g_up,
             w_vres_up=w_vres_up, v0=v0, k_k=k_k, k_a=k_a, r_k=r_k, lnx_w=lnx_w, lnx_b=lnx_b,
             norm_kv=norm_kv, w_kv_a=w_kv_a, norm_ckv=norm_ckv, w_kv_b=w_kv_b, w_in_b=w_in_b, norm_q=norm_q,
             w_q_b=w_q_b, conv_w=conv_w, conv_b=conv_b, final_norm=final_norm)
    W = dict(
        rwkv=[_prep_rwkv_weights(l, P) for l in range(N_A)],
        mla=_prep_mla_weights(P),
        w_in_b=[w_in_b[j].astype(BF16) for j in range(N_B)],
        w_o_tok=[w_o[l, :MIX_A].astype(BF16) for l in range(DEPTH)],
        w_o_mem=[w_o[l, MIX_A:].astype(BF16) for l in range(DEPTH)],
        w_gate=[w_ffn_up[l, :, :D_FF].astype(BF16) for l in range(DEPTH)],
        w_val=[w_ffn_up[l, :, D_FF:].astype(BF16) for l in range(DEPTH)],
        w_down=[w_ffn_down[l].astype(BF16) for l in range(DEPTH)],
    )
    dt = x_prompt.dtype
    B, T = x_prompt.shape[:2]
    n_mem = mem_prompt.shape[1]
    mem_k_all, mem_v_all = _mem_kv(mem_prompt, norm_mem, w_mem_kv)
    y_p, ckv_p, kpe_p, wkv_p, shift_p, conv_p = _trunk(
        x_prompt, jnp.arange(T), mem_k_all, mem_v_all,
        jnp.zeros((N_A, B, H_A, HEAD_A, HEAD_A), dt), jnp.zeros((N_A, B, D_MODEL), dt),
        jnp.zeros((DEPTH, B, CONV_W - 1, D_FF), dt), None, P, W, nseq=B, T=T)
    DB, TS = x_sample.shape[:2]
    past_len = page_table.shape[1] * PAGE_SIZE
    to_t = lambda c: jnp.transpose(c, (0, 1, 3, 4, 2)).reshape(DEPTH, DB, MEM_WIDTH, n_mem)
    y_s, ckv_s, kpe_s, wkv_s, shift_s, conv_s = _trunk(
        x_sample, past_len + jnp.arange(TS), to_t(cache_mem_k), to_t(cache_mem_v), state_wkv, state_shift, state_conv,
        (cache_ckv, cache_kpe, page_table), P, W, nseq=DB, T=TS)
    from_t = lambda a: jnp.transpose(a.reshape(DEPTH, B, MEM_HEADS, MEM_HEAD_DIM, n_mem), (0, 1, 4, 2, 3))
    mem_k_out = from_t(mem_k_all)
    mem_v_out = from_t(mem_v_all)
    return (y_p, y_s, ckv_p, kpe_p, mem_k_out, mem_v_out, wkv_p, shift_p, conv_p,
            ckv_s, kpe_s, wkv_s, shift_s, conv_s)
```

```python
import functools
import math

import jax
import jax.numpy as jnp
from jax import lax
from jax.experimental import pallas as pl
from jax.experimental.pallas import tpu as pltpu

F32 = jnp.float32
BF16 = jnp.bfloat16

D_MODEL = 1024
DEPTH = 4
N_A = 2
N_B = 2
MEM_WIDTH = 256
MIX_A = 768
HEAD_A = 64
H_A = 12
DECAY_LORA = 64
A_LORA = 64
VRES_LORA = 32
GATE_LORA = 160
GN_EPS = 64e-5
C_RWKV = 3 * MIX_A + DECAY_LORA + A_LORA + GATE_LORA
C_A = C_RWKV + MEM_WIDTH
MEM_HEADS = 4
MEM_HEAD_DIM = 64
MEM_SCALE = MEM_HEAD_DIM ** -0.5
NOPE_DIM = 128
ROPE_DIM = 64
V_DIM = 128
H_B = 6
Q_LORA = 256
KV_LORA = 256
ROPE_BASE = 10000.0
ATTN_SCALE = (NOPE_DIM + ROPE_DIM) ** -0.5
D_FF = 2816
CONV_W = 3
RMS_EPS = 1e-6
PAGE_SIZE = 128

PA_R, PA_K, PA_V = 0, 768, 1536
PA_LORA = 2304
PA_GATE = 2432
PA_QMEM = 2688
PA_COLS = 3072

VMEM_LIMIT_BYTES = 52 * 1024 * 1024

NN = ((1,), (0,))
NT = ((1,), (1,))
TN = ((0,), (0,))


def _dg(a, b, dims=NN):
    return lax.dot_general(a, b, (dims, ((), ())), preferred_element_type=F32)


def _split(x):
    hi = x.astype(BF16)
    lo = (x - hi.astype(F32)).astype(BF16)
    return hi, lo


def _dot3(a, b, dims=NN):
    ah, al = _split(a)
    bh, bl = _split(b)
    return _dg(ah, bh, dims) + (_dg(ah, bl, dims) + _dg(al, bh, dims))


def _transpose_exact(x, eye_bf16):
    hi = x.astype(BF16)
    r1 = x - hi.astype(F32)
    mid = r1.astype(BF16)
    lo = (r1 - mid.astype(F32)).astype(BF16)
    return (_dg(eye_bf16, hi, NT) + _dg(eye_bf16, mid, NT)) + _dg(eye_bf16, lo, NT)


def _cparams(n_axes):
    return pltpu.CompilerParams(dimension_semantics=("arbitrary",) * n_axes,
                                vmem_limit_bytes=VMEM_LIMIT_BYTES)


def _rms(x, g, eps=RMS_EPS):
    return x * lax.rsqrt(jnp.mean(x * x, axis=-1, keepdims=True) + eps) * g


def _pick_tile(n, candidates):
    for c in candidates:
        if n % c == 0:
            return c
    return n


ROW_TILES = (512, 256, 128, 64, 32, 16, 8)


def _pos_tile(M, T):
    return _pick_tile(T, ROW_TILES) if T >= ROW_TILES[0] else _pick_tile(M, ROW_TILES)


def _mm_kernel(*refs, n_in, has_norm, has_res, f32_in):
    xs = refs[:n_in]
    ws = refs[n_in:2 * n_in]
    k = 2 * n_in
    g_ref = refs[k] if has_norm else None
    k += int(has_norm)
    r_ref = refs[k] if has_res else None
    k += int(has_res)
    o_ref = refs[k]
    scr = list(refs[k + 1:])

    lhs = []
    si = 0
    for idx in range(n_in):
        if f32_in[idx]:
            s_ref = scr[si]
            si += 1

            @pl.when(pl.program_id(1) == 0)
            def _(x_ref=xs[idx], s_ref=s_ref, idx=idx):
                x = x_ref[...]
                if has_norm and idx == 0:
                    x = _rms(x, g_ref[...])
                s_ref[...] = x.astype(BF16)

            lhs.append(s_ref)
        else:
            lhs.append(xs[idx])
    acc = _dg(lhs[0][...], ws[0][...])
    for idx in range(1, n_in):
        acc = acc + _dg(lhs[idx][...], ws[idx][...])
    if has_res:
        acc = acc + r_ref[...]
    o_ref[...] = acc.astype(o_ref.dtype)


def _mm(xs, ws, *, g=None, res=None, out_dtype=F32, x_col_blocks=None, ks=None, name="mm"):
    n_in = len(xs)
    M = xs[0].shape[0]
    N = ws[0].shape[1]
    ks = [w.shape[0] for w in ws]
    x_col_blocks = x_col_blocks or [0] * n_in
    tm = _pick_tile(M, (512, 256, 128, 64, 32, 16, 8))
    tn = _pick_tile(N, (1024, 768, 512, 384, 256, 128))
    f32_in = tuple(x.dtype == F32 for x in xs)
    in_specs = []
    for x, kk, cb in zip(xs, ks, x_col_blocks):
        in_specs.append(pl.BlockSpec((tm, kk), lambda i, j, cb=cb: (i, cb)))
    for w, kk in zip(ws, ks):
        in_specs.append(pl.BlockSpec((kk, tn), lambda i, j: (0, j)))
    args = list(xs) + list(ws)
    if g is not None:
        in_specs.append(pl.BlockSpec((1, ks[0]), lambda i, j: (0, 0)))
        args.append(g.reshape(1, ks[0]).astype(F32))
    if res is not None:
        in_specs.append(pl.BlockSpec((tm, tn), lambda i, j: (i, j)))
        args.append(res)
    scratch = [pltpu.VMEM((tm, kk), BF16) for kk, f in zip(ks, f32_in) if f]
    return pl.pallas_call(
        functools.partial(_mm_kernel, n_in=n_in, has_norm=g is not None, has_res=res is not None,
                          f32_in=f32_in),
        out_shape=jax.ShapeDtypeStruct((M, N), out_dtype),
        grid=(M // tm, N // tn),
        in_specs=in_specs,
        out_specs=pl.BlockSpec((tm, tn), lambda i, j: (i, j)),
        scratch_shapes=scratch,
        compiler_params=_cparams(2),
        name=name,
    )(*args)


def _rmsnorm_kernel(x_ref, g_ref, o_ref):
    o_ref[...] = _rms(x_ref[...], g_ref[...])


def _rmsnorm(x, g, name="rmsnorm"):
    M, D = x.shape
    tm = _pick_tile(M, (512, 256, 128, 64, 32, 16, 8))
    return pl.pallas_call(
        _rmsnorm_kernel,
        out_shape=jax.ShapeDtypeStruct((M, D), F32),
        grid=(M // tm,),
        in_specs=[pl.BlockSpec((tm, D), lambda i: (i, 0)), pl.BlockSpec((1, D), lambda i: (0, 0))],
        out_specs=pl.BlockSpec((tm, D), lambda i: (i, 0)),
        compiler_params=_cparams(1),
        name=name,
    )(x, g.reshape(1, D).astype(F32))


def _mem_kv_kernel(x_ref, g_ref, wt_ref, k_ref, v_ref):
    kvt = _dg(wt_ref[0], _rms(x_ref[0], g_ref[0]).astype(BF16), NT)
    k_ref[0, 0] = kvt[:MEM_WIDTH]
    v_ref[0, 0] = kvt[MEM_WIDTH:]


def _mem_kv(mem, norm_mem, w_mem_kv):
    B, n_mem, _ = mem.shape
    out = jax.ShapeDtypeStruct((DEPTH, B, MEM_WIDTH, n_mem), F32)
    wt = jnp.swapaxes(w_mem_kv, 1, 2).astype(BF16)
    return pl.pallas_call(
        _mem_kv_kernel,
        out_shape=(out, out),
        grid=(DEPTH, B),
        in_specs=[pl.BlockSpec((1, n_mem, D_MODEL), lambda l, b: (b, 0, 0)),
                  pl.BlockSpec((1, 1, D_MODEL), lambda l, b: (l, 0, 0)),
                  pl.BlockSpec((1, 2 * MEM_WIDTH, D_MODEL), lambda l, b: (l, 0, 0))],
        out_specs=(pl.BlockSpec((1, 1, MEM_WIDTH, n_mem), lambda l, b: (l, b, 0, 0)),
                   pl.BlockSpec((1, 1, MEM_WIDTH, n_mem), lambda l, b: (l, b, 0, 0))),
        compiler_params=_cparams(2),
        name="mem_kv",
    )(mem, norm_mem.reshape(DEPTH, 1, D_MODEL).astype(F32), wt)


def _segsum64(x):
    r = lax.broadcasted_iota(jnp.int32, (128, 128), 0) >> 6
    c = lax.broadcasted_iota(jnp.int32, (128, 128), 1) >> 6
    ones = jnp.where(r == c, 1.0, 0.0).astype(BF16)
    outs = []
    for j in range(x.shape[1] // 128):
        hi, lo = _split(x[:, j * 128:(j + 1) * 128])
        outs.append(_dg(hi, ones) + _dg(lo, ones))
    return jnp.concatenate(outs, axis=1)


def _softplus(x):
    return jnp.maximum(x, 0.0) + jnp.log1p(jnp.exp(-jnp.abs(x)))


def _prep_kernel(*refs, has_vres, blocks_per_seq):
    if blocks_per_seq:
        x_ref, gmix_ref, win_ref = refs[:3]
        refs = refs[2:]
    (cur_ref, prev_ref, mu_ref, wd_ref, wa_ref, wg_ref, wv_ref, w0_ref, a0_ref, v0_ref, kk_ref, ka_ref) = refs[:12]
    k = 12
    vf_ref = refs[k] if has_vres else None
    k += int(has_vres)
    r_o, lw_o, k_o, v_o, kn_o, b_o, g_o, qm_o = refs[k:k + 8]

    if blocks_per_seq:
        cur_ref, carry_scr = refs[k + 8], refs[k + 9]
        xn = _rms(x_ref[...], gmix_ref[...]).astype(BF16)
        for lo in range(0, PA_COLS, 512):
            cur_ref[:, lo:lo + 512] = _dg(xn, win_ref[:, lo:lo + 512])
    tm = cur_ref.shape[0]

    if blocks_per_seq:

        @pl.when(pl.program_id(0) % blocks_per_seq == 0)
        def _():
            carry_scr[...] = prev_ref[0]

        first_row = lax.broadcasted_iota(jnp.int32, (tm, 1), 0) == 0

    def mix(lo, hi):
        c = cur_ref[:, lo:hi]
        if blocks_per_seq:
            p = jnp.where(first_row, carry_scr[:, lo:hi], pltpu.roll(c, 1, 0))
        else:
            p = prev_ref[:, lo:hi]
        return c + (p - c) * mu_ref[:, lo:hi]

    lora = mix(PA_LORA, PA_LORA + 128)
    zw = _dg(jnp.tanh(lora).astype(BF16), wd_ref[...]) + w0_ref[...]
    w_log = -_softplus(-zw) - 0.5
    lw_o[...] = -jnp.exp(w_log)
    a = jax.nn.sigmoid(_dg(lora.astype(BF16), wa_ref[...]) + a0_ref[...])
    gin = mix(PA_GATE, PA_GATE + 256)
    g_o[...] = _dg(jax.nn.sigmoid(gin).astype(BF16), wg_ref[...])
    r_o[...] = mix(PA_R, PA_R + MIX_A)
    kx = mix(PA_K, PA_K + MIX_A)
    v = mix(PA_V, PA_V + MIX_A)
    if has_vres:
        sv = jax.nn.sigmoid(_dg(gin.astype(BF16), wv_ref[...]) + v0_ref[...])
        v = v + (vf_ref[...] - v) * sv
    v_o[...] = v
    kk = kx * kk_ref[...]
    kk = kk * lax.rsqrt(jnp.maximum(_segsum64(kk * kk), 1e-24))
    kn_o[...] = kk
    b_o[...] = kk * a
    k_o[...] = kx * (1.0 + (a - 1.0) * ka_ref[...])
    qm_o[...] = cur_ref[:, PA_QMEM:PA_QMEM + MEM_WIDTH].astype(BF16)
    if blocks_per_seq:
        carry_scr[...] = cur_ref[tm - 1:tm, :]


def _rwkv_prep(x, sp, pw, v_first, g_mix, *, nseq, T):
    M = x.shape[0]
    has_vres = v_first is not None
    row = lambda w: pl.BlockSpec((tm, w), lambda i: (i, 0))
    full = lambda a: pl.BlockSpec(a.shape, lambda i: (0,) * a.ndim)
    if T >= 256:
        tm = 256
        blocks_per_seq = T // tm
        g2 = g_mix.reshape(1, D_MODEL).astype(F32)
        lead_args = [x, g2, pw["w_in"], sp.reshape(nseq, 1, PA_COLS)]
        lead_specs = [row(D_MODEL), full(g2),
                      pl.BlockSpec(pw["w_in"].shape, lambda i: (0, 0), pipeline_mode=pl.Buffered(1)),
                      pl.BlockSpec((1, 1, PA_COLS), lambda i: (i // blocks_per_seq, 0, 0))]
        scratch = [pltpu.VMEM((tm, PA_COLS), F32), pltpu.VMEM((1, PA_COLS), F32)]
    else:
        tm = _pick_tile(M, (256, 128, 64, 32, 16, 8))
        blocks_per_seq = 0
        cur = _mm([x], [pw["w_in"]], g=g_mix, name="rwkv_in")
        prev = jnp.concatenate([sp[:, None, :], cur.reshape(nseq, T, PA_COLS)[:, :-1]], axis=1).reshape(M, PA_COLS)
        lead_args = [cur, prev]
        lead_specs = [row(PA_COLS), row(PA_COLS)]
        scratch = []
    consts = [pw["mu"], pw["wd"], pw["wa"], pw["wg"], pw["wv"], pw["w0"], pw["a0"], pw["v0"], pw["k_k"], pw["k_a"]]
    args = lead_args + consts
    in_specs = lead_specs + [full(a) for a in consts]
    if has_vres:
        args.append(v_first)
        in_specs.append(row(MIX_A))
    outs = [jax.ShapeDtypeStruct((M, MIX_A), F32)] * 7 + [jax.ShapeDtypeStruct((M, MEM_WIDTH), BF16)]
    out_specs = [row(MIX_A)] * 7 + [row(MEM_WIDTH)]
    return pl.pallas_call(
        functools.partial(_prep_kernel, has_vres=has_vres, blocks_per_seq=blocks_per_seq),
        out_shape=outs,
        grid=(M // tm,),
        in_specs=in_specs,
        out_specs=out_specs,
        scratch_shapes=scratch,
        compiler_params=_cparams(1),
        name="rwkv_prep",
    )(*args)


def _wkv_kernel(r_ref, lw_ref, k_ref, v_ref, kn_ref, b_ref, g_ref, rk_ref, lnw_ref, lnb_ref, h0_ref,
                o_ref, hT_ref, h_scr, *, C, G, H, N, nsteps, chain, xpose):
    c_idx = pl.program_id(1)
    S = 1 if chain else G
    GC = G * C
    log2c = C.bit_length() - 1

    eye_bf = jnp.where(lax.broadcasted_iota(jnp.int32, (N, N), 0) == lax.broadcasted_iota(jnp.int32, (N, N), 1),
                       1.0, 0.0).astype(BF16)

    @pl.when(c_idx == 0)
    def _():
        for s in range(S):
            for h in range(H):
                h_scr[s * H + h] = _transpose_exact(h0_ref[s, h], eye_bf) if xpose else h0_ref[s, h]

    lw = lw_ref[...]
    gi = lax.broadcasted_iota(jnp.int32, (GC, GC), 0)
    gj = lax.broadcasted_iota(jnp.int32, (GC, GC), 1)
    same = (gi >> log2c) == (gj >> log2c)
    ltri = jnp.where(same & (gj <= gi), 1.0, 0.0).astype(BF16)
    ones_g = jnp.where(same, 1.0, 0.0).astype(BF16)
    lw_hi, lw_lo = _split(lw)
    cum = _dg(ltri, lw_hi) + _dg(ltri, lw_lo)
    tot = _dg(ones_g, lw_hi) + _dg(ones_g, lw_lo)
    e_neg = jnp.exp(-cum)
    r_all = r_ref[...]
    k_all = k_ref[...]
    v_all = v_ref[...]
    kt_all = kn_ref[...] * jnp.exp(cum - lw)
    rt_all = r_all * jnp.exp(cum)
    kh_all = k_all * e_neg
    bh_all = b_ref[...] * e_neg
    e_rem = jnp.exp(tot - cum)
    kb_all = k_all * e_rem
    bb_all = b_ref[...] * e_rem
    p_last = jnp.exp(tot)
    rkr = r_all * k_all * rk_ref[...]

    ti = lax.broadcasted_iota(jnp.int32, (C, C), 0)
    tj = lax.broadcasted_iota(jnp.int32, (C, C), 1)
    strict = tj < ti
    incl = tj <= ti
    ni = lax.broadcasted_iota(jnp.int32, (N, N), 0)
    nj = lax.broadcasted_iota(jnp.int32, (N, N), 1)
    eye_n = ni == nj
    eye_c = jnp.where(ti == tj, 1.0, 0.0)

    items = [(g, h) for g in range(G) for h in range(H)]
    it = range(len(items))
    rows = [slice(g * C, (g + 1) * C) for g, h in items]
    sls = [slice(h * N, (h + 1) * N) for g, h in items]
    cut = lambda a: [a[rows[i], sls[i]].astype(BF16) for i in it]
    kt, rt, kh, bh, kb, bb, vv = (cut(a) for a in (kt_all, rt_all, kh_all, bh_all, kb_all, bb_all, v_all))
    a_k = [jnp.where(strict, _dg(kt[i], kh[i], NT), 0.0).astype(BF16) for i in it]
    a_b = [jnp.where(strict, _dg(kt[i], bh[i], NT), 0.0) for i in it]
    q_k = [jnp.where(incl, _dg(rt[i], kh[i], NT), 0.0).astype(BF16) for i in it]
    q_b = [jnp.where(incl, _dg(rt[i], bh[i], NT), 0.0).astype(BF16) for i in it]
    tinv = [eye_c - jnp.where(((ti >> 1) == (tj >> 1)), a_b[i], 0.0) for i in it]
    blk = 2
    while blk < C:
        sh = blk.bit_length() - 1
        lvl = ((ti >> (sh + 1)) == (tj >> (sh + 1))) & (((ti >> sh) & 1) == 1) & (((tj >> sh) & 1) == 0)
        t16 = [tinv[i].astype(BF16) for i in it]
        mt = [_dg(jnp.where(lvl, a_b[i], 0.0).astype(BF16), t16[i]).astype(BF16) for i in it]
        tinv = [tinv[i] - _dg(t16[i], mt[i]) for i in it]
        blk *= 2
    t16 = [tinv[i].astype(BF16) for i in it]
    akv = [_dg(a_k[i], vv[i]).astype(BF16) for i in it]
    x = [_dg(t16[i], jnp.concatenate([kt[i], akv[i]], axis=1)).astype(BF16) for i in it]
    qw = [_dg(q_b[i], x[i]) for i in it]
    r2 = [(rt_all[rows[i], sls[i]] - qw[i][:, :N]).astype(BF16) for i in it]
    y0 = [_dg(q_k[i], vv[i]) - qw[i][:, N:] for i in it]
    bw = [_dg(bb[i], x[i], TN) for i in it]
    gmat = [(jnp.where(eye_n, jnp.broadcast_to(p_last[g * C:g * C + 1, sls[i]], (N, N)), 0.0)
             - bw[i][:, :N]).astype(BF16) for i, (g, h) in enumerate(items)]
    h_add = [_dg(kb[i], vv[i], TN) - bw[i][:, N:] for i in it]
    state = [h_scr[s * H + h] for s in range(S) for h in range(H)]
    y = [None] * len(items)
    for i, (g, h) in enumerate(items):
        si = h if chain else g * H + h
        h0 = state[si].astype(BF16)
        y[i] = _dg(r2[i], h0) + y0[i]
        state[si] = _dg(gmat[i], h0) + h_add[i]
    for si in range(S * H):
        h_scr[si] = state[si]
    for i in it:
        rs, sl = rows[i], sls[i]
        mean = jnp.mean(y[i], axis=-1, keepdims=True)
        yc = y[i] - mean
        var = jnp.mean(yc * yc, axis=-1, keepdims=True)
        yn = yc * lax.rsqrt(var + GN_EPS)
        yn = yn * lnw_ref[:, sl] + lnb_ref[:, sl]
        bonus = jnp.sum(rkr[rs, sl], axis=-1, keepdims=True) * v_all[rs, sl]
        o_ref[rs, sl] = ((yn + bonus) * g_ref[rs, sl]).astype(o_ref.dtype)

    @pl.when(c_idx == nsteps - 1)
    def _():
        for s in range(S):
            for h in range(H):
                hT_ref[s, h] = _transpose_exact(h_scr[s * H + h], eye_bf) if xpose else h_scr[s * H + h]


def _wkv(r, lw, k, v, kn, b, g, r_k, lnx_w, lnx_b, h0, *, nseq, T, C, G=None):
    M = r.shape[0]
    nchunks = T // C
    HN = MIX_A
    chain = nchunks > 1
    if G is None:
        G = 4
    if chain:
        nsteps, S, grid = nchunks // G, 1, (nseq, nchunks // G)
        row = pl.BlockSpec((G * C, HN), lambda s, c: (s * nsteps + c, 0))
        h_in, out_dtype = h0, BF16
    else:
        nsteps, S, grid = 1, G, (nseq // G, 1)
        row = pl.BlockSpec((G * C, HN), lambda s, c: (s, 0))
        h_in, out_dtype = jnp.swapaxes(h0, -1, -2), F32
    vec = pl.BlockSpec((1, HN), lambda s, c: (0, 0))
    st = pl.BlockSpec((S, H_A, HEAD_A, HEAD_A), lambda s, c: (s, 0, 0, 0))
    o, h_new = pl.pallas_call(
        functools.partial(_wkv_kernel, C=C, G=G, H=H_A, N=HEAD_A, nsteps=nsteps, chain=chain, xpose=chain),
        out_shape=(jax.ShapeDtypeStruct((M, HN), out_dtype),
                   jax.ShapeDtypeStruct((nseq, H_A, HEAD_A, HEAD_A), F32)),
        grid=grid,
        in_specs=[row] * 7 + [vec] * 3 + [st],
        out_specs=(row, st),
        scratch_shapes=[pltpu.VMEM((S * H_A, HEAD_A, HEAD_A), F32)],
        compiler_params=_cparams(2),
        name="wkv7",
    )(r, lw, k, v, kn, b, g, r_k.reshape(1, HN), lnx_w.reshape(1, HN), lnx_b.reshape(1, HN), h_in)
    return o, (h_new if chain else jnp.swapaxes(h_new, -1, -2))


def _mem_attn_kernel(q_ref, k_ref, v_ref, o_ref, *, nbb, tq):
    c = MEM_SCALE * math.log2(math.e)
    lane_head = lax.broadcasted_iota(jnp.int32, (1, MEM_WIDTH), 1) >> 6
    items = [(bb, h) for bb in range(nbb) for h in range(MEM_HEADS)]
    q_all = q_ref[...].astype(F32)
    qs = [q_all[bb * tq:(bb + 1) * tq, :] for bb in range(nbb)]
    kts = [k_ref[0, bb].astype(BF16) for bb in range(nbb)]
    vts = [v_ref[0, bb].astype(BF16) for bb in range(nbb)]
    ss = [_dg(jnp.where(lane_head == h, qs[bb], 0.0).astype(BF16), kts[bb]) for bb, h in items]
    ms = [jnp.max(s, axis=-1, keepdims=True) for s in ss]
    ps = [jnp.exp2((s - m) * c) for s, m in zip(ss, ms)]
    inv_l = [1.0 / jnp.sum(p, axis=-1, keepdims=True) for p in ps]
    pv = [_dg(p.astype(BF16), vts[bb], NT) for p, (bb, h) in zip(ps, items)]
    outs = []
    for bb in range(nbb):
        o = jnp.zeros((tq, MEM_WIDTH), F32)
        for h in range(MEM_HEADS):
            idx = bb * MEM_HEADS + h
            o = jnp.where(lane_head == h, pv[idx] * inv_l[idx], o)
        outs.append(o)
    o_ref[...] = (outs[0] if nbb == 1 else jnp.concatenate(outs, axis=0)).astype(o_ref.dtype)


def _mem_attn(q, mem_k, mem_v, layer, *, nb, T):
    M = q.shape[0]
    if T >= ROW_TILES[0]:
        tq, nbb = ROW_TILES[0], 1
    else:
        tq, nbb = T, _pick_tile(nb, (16, 8, 4, 2, 1))
    nq = T // tq
    n_mem = mem_k.shape[3]
    return pl.pallas_call(
        functools.partial(_mem_attn_kernel, nbb=nbb, tq=tq),
        out_shape=jax.ShapeDtypeStruct((M, MEM_WIDTH), BF16),
        grid=(nb // nbb, nq),
        in_specs=[pl.BlockSpec((nbb * tq, MEM_WIDTH), lambda b, i: (b * nq + i, 0)),
                  pl.BlockSpec((1, nbb, MEM_WIDTH, n_mem), lambda b, i: (layer, b, 0, 0)),
                  pl.BlockSpec((1, nbb, MEM_WIDTH, n_mem), lambda b, i: (layer, b, 0, 0))],
        out_specs=pl.BlockSpec((nbb * tq, MEM_WIDTH), lambda b, i: (b * nq + i, 0)),
        compiler_params=_cparams(2),
        name="mem_attn",
    )(q, mem_k, mem_v)


def _ffn_kernel(*refs, tm, T, long_mode, col_chunks, final_norm):
    (x_ref, ot_ref, om_ref, wot_ref, wom_ref, g_ref, wg_ref, wv_ref, cw_ref, cb_ref, wd_ref, cp_ref) = refs[:12]
    k = 12
    fin_ref = refs[k] if final_norm else None
    k += int(final_norm)
    o_ref, aux_ref = refs[k:k + 2]
    k += 2
    y_ref = refs[k] if final_norm else None
    k += int(final_norm)
    xn_scr, h_scr, x1_scr = refs[k:k + 3]
    if long_mode:
        carry_scr = refs[k + 3]
        tail_ref = aux_ref
    else:
        gate_ref = aux_ref
    x1_scr[...] = x_ref[...] + (_dg(ot_ref[...].astype(BF16), wot_ref[...]) + _dg(om_ref[...], wom_ref[...]))
    xn_scr[...] = _rms(x1_scr[...], g_ref[...]).astype(BF16)
    xn = xn_scr[...]
    row = lax.broadcasted_iota(jnp.int32, (tm, 1), 0)
    if long_mode:
        nblk = T // tm

        @pl.when(pl.program_id(0) % nblk == 0)
        def _():
            carry_scr[...] = cp_ref[0]
    else:
        t_in_seq = row % T
        ncp = cp_ref.shape[0]
        rr = lax.broadcasted_iota(jnp.int32, (tm, ncp), 0)
        cc = lax.broadcasted_iota(jnp.int32, (tm, ncp), 1)
        same_seq = (rr // T) == (cc >> 1)
        sel1 = jnp.where(same_seq & ((rr % T) + 1 == (cc & 1)), 1.0, 0.0).astype(BF16)
        sel2 = jnp.where(same_seq & ((rr % T) == (cc & 1)), 1.0, 0.0).astype(BF16)
        sel = jnp.concatenate([sel1, sel2], axis=0)
    for (lo, w) in col_chunks:
        cs = slice(lo, lo + w)
        gate = _dg(xn, wg_ref[:, cs])
        val = _dg(xn, wv_ref[:, cs])
        r1 = pltpu.roll(gate, 1, 0)
        r2 = pltpu.roll(gate, 2, 0)
        if long_mode:
            c0 = carry_scr[6:7, cs]
            c1 = carry_scr[7:8, cs]
            p1 = jnp.where(row == 0, c1, r1)
            p2 = jnp.where(row == 0, c0, jnp.where(row == 1, c1, r2))
            tail = gate[tm - 8:tm, :]
            carry_scr[:, cs] = tail
            tail_ref[0, :, cs] = tail
        else:
            cpc = cp_ref[:, cs]
            hi = cpc.astype(BF16)
            r_hi = cpc - hi.astype(F32)
            mid = r_hi.astype(BF16)
            lo = (r_hi - mid.astype(F32)).astype(BF16)
            edge = (_dg(sel, hi) + _dg(sel, mid)) + _dg(sel, lo)
            p1 = jnp.where(t_in_seq >= 1, r1, 0.0) + edge[:tm]
            p2 = jnp.where(t_in_seq >= 2, r2, 0.0) + edge[tm:]
            gate_ref[:, cs] = gate
        conv = cb_ref[:, cs] + p2 * cw_ref[0:1, cs]
        conv = conv + p1 * cw_ref[1:2, cs]
        conv = conv + gate * cw_ref[2:3, cs]
        h_scr[:, cs] = (conv * jax.nn.sigmoid(conv) * val).astype(BF16)
    out = x1_scr[...] + _dg(h_scr[...], wd_ref[...])
    o_ref[...] = out
    if final_norm:
        y_ref[...] = _rms(out, fin_ref[...])


def _ffn(x, o_tok, o_mem, w_o_tok, w_o_mem, g, w_gate, w_val, conv_w, conv_b, w_down, conv_prev, final_g=None, *,
         nseq, T):
    M = x.shape[0]
    long_mode = T >= 512
    tm = 512 if long_mode else 256
    assert (T % tm == 0) if long_mode else (tm % T == 0 and M % tm == 0)
    col_chunks = tuple((lo, min(256, D_FF - lo)) for lo in range(0, D_FF, 256))
    const = lambda a: pl.BlockSpec(a.shape, lambda i: (0,) * a.ndim, pipeline_mode=pl.Buffered(1))
    small = lambda a: pl.BlockSpec(a.shape, lambda i: (0,) * a.ndim)
    g2 = g.reshape(1, D_MODEL).astype(F32)
    cb2 = conv_b.reshape(1, D_FF).astype(F32)
    rows = lambda w: pl.BlockSpec((tm, w), lambda i: (i, 0))
    common_specs = [rows(D_MODEL), rows(MIX_A), rows(MEM_WIDTH), const(w_o_tok), const(w_o_mem), small(g2),
                    const(w_gate), const(w_val), small(conv_w), small(cb2), const(w_down)]
    common_args = [x, o_tok, o_mem, w_o_tok, w_o_mem, g2, w_gate, w_val, conv_w, cb2, w_down]
    final_norm = final_g is not None
    fin_args = [final_g.reshape(1, D_MODEL).astype(F32)] if final_norm else []
    fin_specs = [pl.BlockSpec((1, D_MODEL), lambda i: (0, 0))] if final_norm else []
    fin_out = [jax.ShapeDtypeStruct((M, D_MODEL), F32)] if final_norm else []
    fin_out_specs = [rows(D_MODEL)] if final_norm else []
    base_scratch = [pltpu.VMEM((tm, D_MODEL), BF16), pltpu.VMEM((tm, D_FF), BF16), pltpu.VMEM((tm, D_MODEL), F32)]
    kern = functools.partial(_ffn_kernel, tm=tm, T=T, long_mode=long_mode, col_chunks=col_chunks,
                             final_norm=final_norm)
    if long_mode:
        nblk = T // tm
        cp = jnp.concatenate([jnp.zeros((nseq, 6, D_FF), F32), conv_prev.astype(F32)], axis=1)
        res = pl.pallas_call(
            kern,
            out_shape=[jax.ShapeDtypeStruct((M, D_MODEL), F32), jax.ShapeDtypeStruct((M // tm, 8, D_FF), F32)] + fin_out,
            grid=(M // tm,),
            in_specs=common_specs + [pl.BlockSpec((1, 8, D_FF), lambda i: (i // nblk, 0, 0))] + fin_specs,
            out_specs=[rows(D_MODEL), pl.BlockSpec((1, 8, D_FF), lambda i: (i, 0, 0))] + fin_out_specs,
            scratch_shapes=base_scratch + [pltpu.VMEM((8, D_FF), F32)],
            compiler_params=_cparams(1),
            name="convglu_long",
        )(*common_args, cp, *fin_args)
        new_conv = res[1].reshape(nseq, nblk, 8, D_FF)[:, -1, 6:8, :]
        return res[0], new_conv, (res[2] if final_norm else None)
    ncp = (CONV_W - 1) * (tm // T)
    res = pl.pallas_call(
        kern,
        out_shape=[jax.ShapeDtypeStruct((M, D_MODEL), F32), jax.ShapeDtypeStruct((M, D_FF), F32)] + fin_out,
        grid=(M // tm,),
        in_specs=common_specs + [pl.BlockSpec((ncp, D_FF), lambda i: (i, 0))] + fin_specs,
        out_specs=[rows(D_MODEL), rows(D_FF)] + fin_out_specs,
        scratch_shapes=base_scratch,
        compiler_params=_cparams(1),
        name="convglu_short",
    )(*common_args, conv_prev.astype(F32).reshape(nseq * (CONV_W - 1), D_FF), *fin_args)
    new_conv = res[1].reshape(nseq, T, D_FF)[:, T - 2:, :]
    return res[0], new_conv, (res[2] if final_norm else None)


def _rope128(x128, cs):
    a = x128 * cs
    s = a + pltpu.roll(a, 64, 1)
    lane = lax.broadcasted_iota(jnp.int32, a.shape, 1)
    return jnp.where(lane < 64, s, 0.0)


def _mla_kv_kernel(*refs, expand):
    x_ref, gk_ref, wa_ref, gc_ref, cs_ref = refs[:5]
    k = 5
    wx_ref = refs[k] if expand else None
    k += int(expand)
    ckv_ref, kpe_ref = refs[k:k + 2]
    kvx_ref = refs[k + 2] if expand else None
    xn = _rms(x_ref[...], gk_ref[...]).astype(BF16)
    h = _dg(xn, wa_ref[...])
    ckv = _rms(h[:, :KV_LORA], gc_ref[...])
    rp = _rope128(h[:, KV_LORA:], cs_ref[...])
    ckv_ref[...] = ckv
    kpe_ref[...] = rp[:, :ROPE_DIM]
    if expand:
        ckpe = jnp.concatenate([ckv, rp], axis=1).astype(BF16)
        for lo in range(0, wx_ref.shape[1], 768):
            kvx_ref[:, lo:lo + 768] = _dg(ckpe, wx_ref[:, lo:lo + 768]).astype(kvx_ref.dtype)


def _mla_kv(x, norm_kv, w_kva, norm_ckv, cs, w_kvx, *, T):
    M = x.shape[0]
    tm = _pos_tile(M, T)
    ncs = cs.shape[0] // tm
    expand = w_kvx is not None
    const = lambda a: pl.BlockSpec(a.shape, lambda i: (0,) * a.ndim)
    rows = lambda w: pl.BlockSpec((tm, w), lambda i: (i, 0))
    gk = norm_kv.reshape(1, D_MODEL).astype(F32)
    gc = norm_ckv.reshape(1, KV_LORA).astype(F32)
    args = [x, gk, w_kva, gc, cs] + ([w_kvx] if expand else [])
    in_specs = [rows(D_MODEL), const(gk), const(w_kva), const(gc),
                pl.BlockSpec((tm, 128), lambda i: (i % ncs, 0))] + ([const(w_kvx)] if expand else [])
    out_shape = [jax.ShapeDtypeStruct((M, KV_LORA), F32), jax.ShapeDtypeStruct((M, ROPE_DIM), F32)]
    out_specs = [rows(KV_LORA), rows(ROPE_DIM)]
    if expand:
        out_shape.append(jax.ShapeDtypeStruct((M, w_kvx.shape[1]), BF16))
        out_specs.append(rows(w_kvx.shape[1]))
    return pl.pallas_call(
        functools.partial(_mla_kv_kernel, expand=expand),
        out_shape=out_shape,
        grid=(M // tm,),
        in_specs=in_specs,
        out_specs=out_specs,
        compiler_params=_cparams(1),
        name="mla_kv",
    )(*args)


def _mla_q_kernel(x_ref, gm_ref, win_ref, gq_ref, wq_ref, cs_ref, q_ref, qm_ref):
    xn = _rms(x_ref[...], gm_ref[...]).astype(BF16)
    proj = _dg(xn, win_ref[...])
    qm_ref[...] = proj[:, Q_LORA:].astype(qm_ref.dtype)
    qn = _rms(proj[:, :Q_LORA], gq_ref[...]).astype(BF16)
    cs = cs_ref[...]
    for h in range(H_B):
        base = h * 256
        qh = _dg(qn, wq_ref[:, base:base + 256])
        q_ref[:, base:base + 128] = qh[:, :128].astype(q_ref.dtype)
        q_ref[:, base + 128:base + 256] = _rope128(qh[:, 128:], cs).astype(q_ref.dtype)


def _mla_q(x, g_mix, w_in, norm_q, w_q, cs, *, T):
    M = x.shape[0]
    tm = _pos_tile(M, T)
    ncs = cs.shape[0] // tm
    const = lambda a: pl.BlockSpec(a.shape, lambda i: (0,) * a.ndim)
    gm = g_mix.reshape(1, D_MODEL).astype(F32)
    gq = norm_q.reshape(1, Q_LORA).astype(F32)
    return pl.pallas_call(
        _mla_q_kernel,
        out_shape=(jax.ShapeDtypeStruct((M, H_B * 256), BF16), jax.ShapeDtypeStruct((M, MEM_WIDTH), BF16)),
        grid=(M // tm,),
        in_specs=[pl.BlockSpec((tm, D_MODEL), lambda i: (i, 0)), const(gm), const(w_in), const(gq), const(w_q),
                  pl.BlockSpec((tm, 128), lambda i: (i % ncs, 0))],
        out_specs=(pl.BlockSpec((tm, H_B * 256), lambda i: (i, 0)), pl.BlockSpec((tm, MEM_WIDTH), lambda i: (i, 0))),
        compiler_params=_cparams(1),
        name="mla_q",
    )(x, gm, w_in, gq, w_q, cs)


def _flash_kernel(q_ref, k_ref, v_ref, o_ref, m_scr, l_scr, acc_scr, *, tq, tk):
    i = pl.program_id(1)
    j = pl.program_id(2)

    @pl.when(j == 0)
    def _():
        m_scr[...] = jnp.full(m_scr.shape, -jnp.inf, F32)
        l_scr[...] = jnp.zeros(l_scr.shape, F32)
        acc_scr[...] = jnp.zeros(acc_scr.shape, F32)

    hs = range(H_B)

    def block(masked):
        c = ATTN_SCALE * math.log2(math.e)
        ss = [_dg(q_ref[:, h * 256:(h + 1) * 256], k_ref[:, h * 256:(h + 1) * 256], NT) for h in hs]
        if masked:
            keep = lax.broadcasted_iota(jnp.int32, (tq, tk), 1) <= lax.broadcasted_iota(jnp.int32, (tq, tk), 0)
            ss = [jnp.where(keep, s, -jnp.inf) for s in ss]
        nkt = tk // 128
        tiles = [[ss[h][:, t * 128:(t + 1) * 128] for t in range(nkt)] for h in hs]
        m_prev = [m_scr[h] for h in hs]
        m_new = []
        for h in hs:
            tmax = tiles[h][0]
            for t in range(1, nkt):
                tmax = jnp.maximum(tmax, tiles[h][t])
            m_new.append(jnp.maximum(m_prev[h], jnp.max(tmax, axis=-1, keepdims=True)))
        alpha = [jnp.exp2((m_prev[h] - m_new[h]) * c) for h in hs]
        ps = [[jnp.exp2((tiles[h][t] - m_new[h]) * c) for t in range(nkt)] for h in hs]
        pv = [_dg(jnp.concatenate([p.astype(BF16) for p in ps[h]], axis=1), v_ref[:, h * V_DIM:(h + 1) * V_DIM])
              for h in hs]
        for h in hs:
            psum = ps[h][0]
            for t in range(1, nkt):
                psum = psum + ps[h][t]
            l_scr[h] = alpha[h] * l_scr[h] + jnp.sum(psum, axis=-1, keepdims=True)
            acc_scr[:, h * V_DIM:(h + 1) * V_DIM] = alpha[h] * acc_scr[:, h * V_DIM:(h + 1) * V_DIM] + pv[h]
            m_scr[h] = m_new[h]

    @pl.when(j < i)
    def _():
        block(False)

    @pl.when(j == i)
    def _():
        block(True)
        for h in hs:
            sl = slice(h * V_DIM, (h + 1) * V_DIM)
            o_ref[:, sl] = (acc_scr[:, sl] / l_scr[h]).astype(o_ref.dtype)


def _flash_causal(q, kv, *, nb, T):
    M = q.shape[0]
    tq = tk = 512
    nq = T // tq
    kw = H_B * 256
    vw = H_B * V_DIM
    return pl.pallas_call(
        functools.partial(_flash_kernel, tq=tq, tk=tk),
        out_shape=jax.ShapeDtypeStruct((M, vw), BF16),
        grid=(nb, nq, nq),
        in_specs=[pl.BlockSpec((tq, kw), lambda b, i, j: (b * nq + i, 0)),
                  pl.BlockSpec((tk, kw), lambda b, i, j: (b * nq + jnp.minimum(i, j), 0)),
                  pl.BlockSpec((tk, vw), lambda b, i, j: (b * nq + jnp.minimum(i, j), kw // vw))],
        out_specs=pl.BlockSpec((tq, vw), lambda b, i, j: (b * nq + i, 0)),
        scratch_shapes=[pltpu.VMEM((H_B, tq, 128), F32), pltpu.VMEM((H_B, tq, 128), F32), pltpu.VMEM((tq, vw), F32)],
        compiler_params=_cparams(3),
        name="mla_causal",
    )(q, kv, kv)


def _decode_kernel(pt_ref, q_ref, ckn_ref, kpn_ref, wuk_ref, wuv_ref, ck_hbm, kp_hbm, o_ref,
                   ck_buf, kp_buf, sem, m_scr, l_scr, acc_scr, *, PP, n_steps, n_total, TS):
    j = pl.program_id(1)
    n = pl.program_id(0) * n_steps + j
    slot = lax.rem(n, 2)
    R = H_B * TS

    def group_copies(step, slot_, for_wait):
        cps = []
        for p in range(PP):
            page = 0 if for_wait else pt_ref[step * PP + p]
            cps.append(pltpu.make_async_copy(ck_hbm.at[page], ck_buf.at[slot_, p], sem.at[slot_, 0]))
            cps.append(pltpu.make_async_copy(kp_hbm.at[page], kp_buf.at[slot_, p], sem.at[slot_, 1]))
        return cps

    @pl.when(n == 0)
    def _():
        for cp in group_copies(0, 0, False):
            cp.start()

    nxt = jnp.minimum(n + 1, n_total - 1)
    for cp in group_copies(nxt, 1 - slot, False):
        cp.start()
    for cp in group_copies(n, slot, True):
        cp.wait()

    @pl.when(j == 0)
    def _():
        m_scr[...] = jnp.full(m_scr.shape, -jnp.inf, F32)
        l_scr[...] = jnp.zeros(l_scr.shape, F32)
        acc_scr[...] = jnp.zeros(acc_scr.shape, F32)

    q_all = q_ref[0].astype(F32)
    q_lat = jnp.concatenate(
        [_dg(q_all[:, h * 256:h * 256 + NOPE_DIM].astype(BF16), wuk_ref[h]) for h in range(H_B)], axis=0).astype(BF16)
    q_pe = jnp.concatenate(
        [q_all[:, h * 256 + NOPE_DIM:h * 256 + NOPE_DIM + ROPE_DIM] for h in range(H_B)], axis=0).astype(BF16)
    c = ATTN_SCALE * math.log2(math.e)

    cks = [ck_buf[slot, p].astype(BF16) for p in range(PP)]
    kps = [kp_buf[slot, p].astype(BF16) for p in range(PP)]
    ss = [_dg(q_lat, cks[p], NT) + _dg(q_pe, kps[p]) for p in range(PP)]
    pad = PAGE_SIZE - TS
    ckn = jnp.concatenate([ckn_ref[0], jnp.zeros((pad, KV_LORA), F32)], axis=0).astype(BF16)
    kpn = jnp.concatenate([kpn_ref[0], jnp.zeros((pad, ROPE_DIM), F32)], axis=0).astype(BF16)
    t_q = lax.broadcasted_iota(jnp.int32, (R, PAGE_SIZE), 0) % TS
    col = lax.broadcasted_iota(jnp.int32, (R, PAGE_SIZE), 1)
    last_shift = jnp.where(j == n_steps - 1, 0, -PAGE_SIZE)
    s_new = jnp.where(col <= t_q + last_shift, _dg(q_lat, ckn, NT) + _dg(q_pe, kpn, NT), -jnp.inf)
    ss.append(s_new)
    cks.append(ckn)

    tile_max = ss[0]
    for s in ss[1:]:
        tile_max = jnp.maximum(tile_max, s)
    m_prev = m_scr[...]
    m_new = jnp.maximum(m_prev, jnp.max(tile_max, axis=-1, keepdims=True))
    alpha = jnp.exp2((m_prev - m_new) * c)
    ps = [jnp.exp2((s - m_new) * c) for s in ss]
    p_sum = ps[0]
    for p in ps[1:]:
        p_sum = p_sum + p
    acc = jnp.concatenate([alpha, alpha], axis=1) * acc_scr[...]
    for p, vv in zip(ps, cks):
        acc = acc + _dg(p.astype(BF16), vv)
    m_scr[...] = m_new
    l_scr[...] = alpha * l_scr[...] + jnp.sum(p_sum, axis=-1, keepdims=True)
    acc_scr[...] = acc

    @pl.when(j == n_steps - 1)
    def _():
        o_lat = acc_scr[...] / jnp.concatenate([l_scr[...], l_scr[...]], axis=1)
        for h in range(H_B):
            o_h = o_lat[h * TS:(h + 1) * TS, :].astype(BF16)
            o_ref[0, :, h * V_DIM:(h + 1) * V_DIM] = _dg(o_h, wuv_ref[h]).astype(o_ref.dtype)

    @pl.when(n == n_total - 1)
    def _():
        for cp in group_copies(n, 1 - slot, True):
            cp.wait()


def _decode_attn(page_table, q, cache_ckv, cache_kpe, ckv_new, kpe_new, w_ukT, w_uv, *, PP=64):
    nb, n_pages = page_table.shape
    TS = ckv_new.shape[1]
    R = H_B * TS
    n_steps = n_pages // PP
    pt = page_table.reshape(-1).astype(jnp.int32)
    kpe_t = jnp.swapaxes(cache_kpe, 1, 2)
    grid_spec = pltpu.PrefetchScalarGridSpec(
        num_scalar_prefetch=1,
        grid=(nb, n_steps),
        in_specs=[pl.BlockSpec((1, TS, H_B * 256), lambda b, j, pt_ref: (b, 0, 0)),
                  pl.BlockSpec((1, TS, KV_LORA), lambda b, j, pt_ref: (b, 0, 0)),
                  pl.BlockSpec((1, TS, ROPE_DIM), lambda b, j, pt_ref: (b, 0, 0)),
                  pl.BlockSpec((H_B, NOPE_DIM, KV_LORA), lambda b, j, pt_ref: (0, 0, 0)),
                  pl.BlockSpec((H_B, KV_LORA, V_DIM), lambda b, j, pt_ref: (0, 0, 0)),
                  pl.BlockSpec(memory_space=pl.ANY),
                  pl.BlockSpec(memory_space=pl.ANY)],
        out_specs=pl.BlockSpec((1, TS, H_B * V_DIM), lambda b, j, pt_ref: (b, 0, 0)),
        scratch_shapes=[pltpu.VMEM((2, PP, PAGE_SIZE, KV_LORA), F32),
                        pltpu.VMEM((2, PP, ROPE_DIM, PAGE_SIZE), F32),
                        pltpu.SemaphoreType.DMA((2, 2)),
                        pltpu.VMEM((R, 128), F32), pltpu.VMEM((R, 128), F32), pltpu.VMEM((R, KV_LORA), F32)],
    )
    return pl.pallas_call(
        functools.partial(_decode_kernel, PP=PP, n_steps=n_steps, n_total=nb * n_steps, TS=TS),
        out_shape=jax.ShapeDtypeStruct((nb, TS, H_B * V_DIM), BF16),
        grid_spec=grid_spec,
        compiler_params=_cparams(2),
        name="mla_decode",
    )(pt, q, ckv_new, kpe_new, w_ukT, w_uv, cache_ckv, kpe_t)


def _rope_table(pos):
    half = ROPE_DIM // 2
    inv = ROPE_BASE ** (-jnp.arange(half, dtype=F32) / half)
    ang = pos.astype(F32)[:, None] * inv[None, :]
    cos, sin = jnp.cos(ang), jnp.sin(ang)
    return jnp.concatenate([cos, cos, -sin, sin], axis=-1)


def _swap_halves(w):
    half = ROPE_DIM // 2
    return jnp.concatenate([w[..., half:], w[..., :half]], axis=-1)


def _prep_rwkv_weights(l, P):
    w_in = P["w_in_a"][l]
    mu = P["mu_a"][l]
    cuts = [0, MIX_A, MIX_A + DECAY_LORA, 2 * MIX_A + DECAY_LORA, 3 * MIX_A + DECAY_LORA,
            3 * MIX_A + DECAY_LORA + A_LORA, C_RWKV, C_A]
    seg = lambda a, i: a[..., cuts[i]:cuts[i + 1]]
    r_w, wl_w, k_w, v_w, al_w, gl_w, qm_w = [seg(w_in, i) for i in range(7)]
    r_m, wl_m, k_m, v_m, al_m, gl_m = [seg(mu, i) for i in range(6)]
    zc = lambda n: jnp.zeros((D_MODEL, n), F32)
    zm = lambda n: jnp.zeros((n,), F32)
    if l > 0:
        vl_w, vl_m = P["w_vres_in"][l - 1], P["mu_vres"][l - 1]
        wv_up, v0 = P["w_vres_up"][l - 1], P["v0"][l - 1]
    else:
        vl_w, vl_m = zc(VRES_LORA), zm(VRES_LORA)
        wv_up, v0 = jnp.zeros((VRES_LORA, MIX_A), F32), zm(MIX_A)
    w_full = jnp.concatenate([r_w, k_w, v_w, wl_w, al_w, gl_w, vl_w, zc(64), qm_w, zc(128)], axis=1)
    mu_full = jnp.concatenate([r_m, k_m, v_m, wl_m, al_m, gl_m, vl_m, zm(64), zm(MEM_WIDTH), zm(128)])
    zr = lambda n: jnp.zeros((n, MIX_A), F32)
    return dict(
        w_in=w_full.astype(BF16),
        mu=mu_full.reshape(1, PA_COLS),
        wd=jnp.concatenate([P["w_decay_up"][l], zr(A_LORA)], axis=0).astype(BF16),
        wa=jnp.concatenate([zr(DECAY_LORA), P["w_a_up"][l]], axis=0).astype(BF16),
        wg=jnp.concatenate([P["w_g_up"][l], zr(256 - GATE_LORA)], axis=0).astype(BF16),
        wv=jnp.concatenate([zr(GATE_LORA), wv_up, zr(256 - GATE_LORA - VRES_LORA)], axis=0).astype(BF16),
        w0=P["w0"][l].reshape(1, MIX_A), a0=P["a0"][l].reshape(1, MIX_A), v0=v0.reshape(1, MIX_A),
        k_k=P["k_k"][l].reshape(1, MIX_A), k_a=P["k_a"][l].reshape(1, MIX_A),
    )


def _prep_mla_weights(P):
    w_kv_a = P["w_kv_a"]
    w_kva = jnp.concatenate([w_kv_a, _swap_halves(w_kv_a[:, KV_LORA:])], axis=1).astype(BF16)
    w_kv_b = P["w_kv_b"]
    w_uk, w_uv = w_kv_b[..., :NOPE_DIM], w_kv_b[..., NOPE_DIM:]
    k_part = jnp.zeros((KV_LORA + 128, H_B, 256), F32)
    k_part = k_part.at[:KV_LORA, :, :NOPE_DIM].set(w_uk)
    eye = jnp.eye(ROPE_DIM, dtype=F32)
    k_part = k_part.at[KV_LORA:KV_LORA + ROPE_DIM, :, NOPE_DIM:NOPE_DIM + ROPE_DIM].set(
        jnp.broadcast_to(eye[:, None, :], (ROPE_DIM, H_B, ROPE_DIM)))
    v_part = jnp.zeros((KV_LORA + 128, H_B, V_DIM), F32).at[:KV_LORA].set(w_uv)
    w_kvx = jnp.concatenate([k_part.reshape(KV_LORA + 128, H_B * 256), v_part.reshape(KV_LORA + 128, H_B * V_DIM)],
                            axis=1).astype(BF16)
    w_q = []
    for j in range(N_B):
        wq = P["w_q_b"][j].reshape(Q_LORA, H_B, NOPE_DIM + ROPE_DIM)
        rope_w = wq[..., NOPE_DIM:]
        w_q.append(jnp.concatenate([wq[..., :NOPE_DIM], rope_w, _swap_halves(rope_w)], axis=-1)
                   .reshape(Q_LORA, H_B * 256).astype(BF16))
    return dict(
        w_kva=w_kva, w_kvx=w_kvx, w_q=w_q,
        w_ukT=jnp.transpose(w_uk, (1, 2, 0)).astype(BF16),
        w_uv=jnp.transpose(w_uv, (1, 0, 2)).astype(BF16),
    )


def _trunk(x, pos, mem_k, mem_v, wkv0, shift0, conv0, past, P, W, *, nseq, T):
    M = nseq * T
    x = x.reshape(M, D_MODEL)
    cs = _rope_table(pos)
    if T < ROW_TILES[0]:
        cs = jnp.tile(cs, (nseq, 1))
    chunk = 64 if T >= 64 else T
    new_wkv, new_shift, new_conv = [], [], []
    v_first = None
    ckv = kpe = ckpe = kvx = None
    for l in range(DEPTH):
        g_mix = P["norm_mix"][l]
        if l < N_A:
            pw = W["rwkv"][l]
            x_last = x.reshape(nseq, T, D_MODEL)[:, -1]
            new_shift.append(_rmsnorm(x_last, g_mix, name="shift_norm"))
            sp = _mm([shift0[l].astype(F32)], [pw["w_in"]], name="rwkv_in_shift")
            r, lw, k, v, kn, b, g, q_mem = _rwkv_prep(x, sp, pw, v_first if l > 0 else None, g_mix, nseq=nseq, T=T)
            if l == 0:
                v_first = v
            h0 = wkv0[l].astype(F32)
            o_tok, h_new = _wkv(r, lw, k, v, kn, b, g, P["r_k"][l], P["lnx_w"][l], P["lnx_b"][l], h0,
                                nseq=nseq, T=T, C=chunk)
            new_wkv.append(h_new)
        else:
            j = l - N_A
            if l == N_A:
                res = _mla_kv(x, P["norm_kv"], W["mla"]["w_kva"], P["norm_ckv"], cs,
                              W["mla"]["w_kvx"] if past is None else None, T=T)
                ckv, kpe = res[0], res[1]
                kvx = res[2] if past is None else None
            q, q_mem = _mla_q(x, g_mix, W["w_in_b"][j], P["norm_q"][j], W["mla"]["w_q"][j], cs, T=T)
            if past is None:
                o_tok = _flash_causal(q, kvx, nb=nseq, T=T)
            else:
                o_tok = _decode_attn(past[2], q.reshape(nseq, T, H_B * 256), past[0], past[1],
                                     ckv.reshape(nseq, T, KV_LORA), kpe.reshape(nseq, T, ROPE_DIM),
                                     W["mla"]["w_ukT"], W["mla"]["w_uv"]).reshape(M, H_B * V_DIM)
        o_mem = _mem_attn(q_mem, mem_k, mem_v, l, nb=nseq, T=T)
        x, c_new, y = _ffn(x, o_tok, o_mem, W["w_o_tok"][l], W["w_o_mem"][l], P["norm_ffn"][l], W["w_gate"][l],
                           W["w_val"][l], P["conv_w"][l], P["conv_b"][l], W["w_down"][l], conv0[l],
                           P["final_norm"] if l == DEPTH - 1 else None, nseq=nseq, T=T)
        new_conv.append(c_new)
    return (y.reshape(nseq, T, D_MODEL), ckv.reshape(nseq, T, KV_LORA), kpe.reshape(nseq, T, ROPE_DIM),
            jnp.stack(new_wkv), jnp.stack(new_shift), jnp.stack(new_conv))


def kernel(x_prompt, x_sample, cache_ckv, cache_kpe, cache_mem_k, cache_mem_v, state_wkv, state_shift, state_conv, page_table, mem_prompt, norm_mix, norm_ffn, norm_mem, w_mem_kv, w_o, w_in_a, mu_a, w_vres_in, mu_vres, w_decay_up, w0, w_a_up, a0, w_g_up, w_vres_up, v0, k_k, k_a, r_k, lnx_w, lnx_b, norm_kv, w_kv_a, norm_ckv, w_kv_b, w_in_b, norm_q, w_q_b, w_ffn_up, conv_w, conv_b, w_ffn_down, final_norm):
    P = dict(norm_mix=norm_mix, norm_ffn=norm_ffn, w_o=w_o, w_in_a=w_in_a, mu_a=mu_a, w_vres_in=w_vres_in,
             mu_vres=mu_vres, w_decay_up=w_decay_up, w0=w0, w_a_up=w_a_up, a0=a0, w_g_up=w_g_up,
             w_vres_up=w_vres_up, v0=v0, k_k=k_k, k_a=k_a, r_k=r_k, lnx_w=lnx_w, lnx_b=lnx_b,
             norm_kv=norm_kv, w_kv_a=w_kv_a, norm_ckv=norm_ckv, w_kv_b=w_kv_b, w_in_b=w_in_b, norm_q=norm_q,
             w_q_b=w_q_b, conv_w=conv_w, conv_b=conv_b, final_norm=final_norm)
    W = dict(
        rwkv=[_prep_rwkv_weights(l, P) for l in range(N_A)],
        mla=_prep_mla_weights(P),
        w_in_b=[w_in_b[j].astype(BF16) for j in range(N_B)],
        w_o_tok=[w_o[l, :MIX_A].astype(BF16) for l in range(DEPTH)],
        w_o_mem=[w_o[l, MIX_A:].astype(BF16) for l in range(DEPTH)],
        w_gate=[w_ffn_up[l, :, :D_FF].astype(BF16) for l in range(DEPTH)],
        w_val=[w_ffn_up[l, :, D_FF:].astype(BF16) for l in range(DEPTH)],
        w_down=[w_ffn_down[l].astype(BF16) for l in range(DEPTH)],
    )
    dt = x_prompt.dtype
    B, T = x_prompt.shape[:2]
    n_mem = mem_prompt.shape[1]
    mem_k_all, mem_v_all = _mem_kv(mem_prompt, norm_mem, w_mem_kv)
    y_p, ckv_p, kpe_p, wkv_p, shift_p, conv_p = _trunk(
        x_prompt, jnp.arange(T), mem_k_all, mem_v_all,
        jnp.zeros((N_A, B, H_A, HEAD_A, HEAD_A), dt), jnp.zeros((N_A, B, D_MODEL), dt),
        jnp.zeros((DEPTH, B, CONV_W - 1, D_FF), dt), None, P, W, nseq=B, T=T)
    DB, TS = x_sample.shape[:2]
    past_len = page_table.shape[1] * PAGE_SIZE
    to_t = lambda c: jnp.transpose(c, (0, 1, 3, 4, 2)).reshape(DEPTH, DB, MEM_WIDTH, n_mem)
    y_s, ckv_s, kpe_s, wkv_s, shift_s, conv_s = _trunk(
        x_sample, past_len + jnp.arange(TS), to_t(cache_mem_k), to_t(cache_mem_v), state_wkv, state_shift, state_conv,
        (cache_ckv, cache_kpe, page_table), P, W, nseq=DB, T=TS)
    from_t = lambda a: jnp.transpose(a.reshape(DEPTH, B, MEM_HEADS, MEM_HEAD_DIM, n_mem), (0, 1, 4, 2, 3))
    mem_k_out = from_t(mem_k_all)
    mem_v_out = from_t(mem_v_all)
    return (y_p, y_s, ckv_p, kpe_p, mem_k_out, mem_v_out, wkv_p, shift_p, conv_p,
            ckv_s, kpe_s, wkv_s, shift_s, conv_s)
```

```python
import functools
import math

import jax
import jax.numpy as jnp
from jax import lax
from jax.experimental import pallas as pl
from jax.experimental.pallas import tpu as pltpu

F32 = jnp.float32
BF16 = jnp.bfloat16

D_MODEL = 1024
DEPTH = 4
N_A = 2
N_B = 2
MEM_WIDTH = 256
MIX_A = 768
HEAD_A = 64
H_A = 12
DECAY_LORA = 64
A_LORA = 64
VRES_LORA = 32
GATE_LORA = 160
GN_EPS = 64e-5
C_RWKV = 3 * MIX_A + DECAY_LORA + A_LORA + GATE_LORA
C_A = C_RWKV + MEM_WIDTH
MEM_HEADS = 4
MEM_HEAD_DIM = 64
MEM_SCALE = MEM_HEAD_DIM ** -0.5
NOPE_DIM = 128
ROPE_DIM = 64
V_DIM = 128
H_B = 6
Q_LORA = 256
KV_LORA = 256
ROPE_BASE = 10000.0
ATTN_SCALE = (NOPE_DIM + ROPE_DIM) ** -0.5
D_FF = 2816
CONV_W = 3
RMS_EPS = 1e-6
PAGE_SIZE = 128

PA_R, PA_K, PA_V = 0, 768, 1536
PA_LORA = 2304
PA_GATE = 2432
PA_QMEM = 2688
PA_COLS = 3072

VMEM_LIMIT_BYTES = 52 * 1024 * 1024

NN = ((1,), (0,))
NT = ((1,), (1,))
TN = ((0,), (0,))


def _dg(a, b, dims=NN):
    return lax.dot_general(a, b, (dims, ((), ())), preferred_element_type=F32)


def _split(x):
    hi = x.astype(BF16)
    lo = (x - hi.astype(F32)).astype(BF16)
    return hi, lo


def _dot3(a, b, dims=NN):
    ah, al = _split(a)
    bh, bl = _split(b)
    return _dg(ah, bh, dims) + (_dg(ah, bl, dims) + _dg(al, bh, dims))


def _transpose_exact(x, eye_bf16):
    hi = x.astype(BF16)
    r1 = x - hi.astype(F32)
    mid = r1.astype(BF16)
    lo = (r1 - mid.astype(F32)).astype(BF16)
    return (_dg(eye_bf16, hi, NT) + _dg(eye_bf16, mid, NT)) + _dg(eye_bf16, lo, NT)


def _cparams(n_axes):
    return pltpu.CompilerParams(dimension_semantics=("arbitrary",) * n_axes,
                                vmem_limit_bytes=VMEM_LIMIT_BYTES)


def _rms(x, g, eps=RMS_EPS):
    return x * lax.rsqrt(jnp.mean(x * x, axis=-1, keepdims=True) + eps) * g


def _pick_tile(n, candidates):
    for c in candidates:
        if n % c == 0:
            return c
    return n


ROW_TILES = (512, 256, 128, 64, 32, 16, 8)


def _pos_tile(M, T):
    return _pick_tile(T, ROW_TILES) if T >= ROW_TILES[0] else _pick_tile(M, ROW_TILES)


def _mm_kernel(*refs, n_in, has_norm, has_res, f32_in):
    xs = refs[:n_in]
    ws = refs[n_in:2 * n_in]
    k = 2 * n_in
    g_ref = refs[k] if has_norm else None
    k += int(has_norm)
    r_ref = refs[k] if has_res else None
    k += int(has_res)
    o_ref = refs[k]
    scr = list(refs[k + 1:])

    lhs = []
    si = 0
    for idx in range(n_in):
        if f32_in[idx]:
            s_ref = scr[si]
            si += 1

            @pl.when(pl.program_id(1) == 0)
            def _(x_ref=xs[idx], s_ref=s_ref, idx=idx):
                x = x_ref[...]
                if has_norm and idx == 0:
                    x = _rms(x, g_ref[...])
                s_ref[...] = x.astype(BF16)

            lhs.append(s_ref)
        else:
            lhs.append(xs[idx])
    acc = _dg(lhs[0][...], ws[0][...])
    for idx in range(1, n_in):
        acc = acc + _dg(lhs[idx][...], ws[idx][...])
    if has_res:
        acc = acc + r_ref[...]
    o_ref[...] = acc.astype(o_ref.dtype)


def _mm(xs, ws, *, g=None, res=None, out_dtype=F32, x_col_blocks=None, ks=None, name="mm"):
    n_in = len(xs)
    M = xs[0].shape[0]
    N = ws[0].shape[1]
    ks = [w.shape[0] for w in ws]
    x_col_blocks = x_col_blocks or [0] * n_in
    tm = _pick_tile(M, (512, 256, 128, 64, 32, 16, 8))
    tn = _pick_tile(N, (1024, 768, 512, 384, 256, 128))
    f32_in = tuple(x.dtype == F32 for x in xs)
    in_specs = []
    for x, kk, cb in zip(xs, ks, x_col_blocks):
        in_specs.append(pl.BlockSpec((tm, kk), lambda i, j, cb=cb: (i, cb)))
    for w, kk in zip(ws, ks):
        in_specs.append(pl.BlockSpec((kk, tn), lambda i, j: (0, j)))
    args = list(xs) + list(ws)
    if g is not None:
        in_specs.append(pl.BlockSpec((1, ks[0]), lambda i, j: (0, 0)))
        args.append(g.reshape(1, ks[0]).astype(F32))
    if res is not None:
        in_specs.append(pl.BlockSpec((tm, tn), lambda i, j: (i, j)))
        args.append(res)
    scratch = [pltpu.VMEM((tm, kk), BF16) for kk, f in zip(ks, f32_in) if f]
    return pl.pallas_call(
        functools.partial(_mm_kernel, n_in=n_in, has_norm=g is not None, has_res=res is not None,
                          f32_in=f32_in),
        out_shape=jax.ShapeDtypeStruct((M, N), out_dtype),
        grid=(M // tm, N // tn),
        in_specs=in_specs,
        out_specs=pl.BlockSpec((tm, tn), lambda i, j: (i, j)),
        scratch_shapes=scratch,
        compiler_params=_cparams(2),
        name=name,
    )(*args)


def _rmsnorm_kernel(x_ref, g_ref, o_ref):
    o_ref[...] = _rms(x_ref[...], g_ref[...])


def _rmsnorm(x, g, name="rmsnorm"):
    M, D = x.shape
    tm = _pick_tile(M, (512, 256, 128, 64, 32, 16, 8))
    return pl.pallas_call(
        _rmsnorm_kernel,
        out_shape=jax.ShapeDtypeStruct((M, D), F32),
        grid=(M // tm,),
        in_specs=[pl.BlockSpec((tm, D), lambda i: (i, 0)), pl.BlockSpec((1, D), lambda i: (0, 0))],
        out_specs=pl.BlockSpec((tm, D), lambda i: (i, 0)),
        compiler_params=_cparams(1),
        name=name,
    )(x, g.reshape(1, D).astype(F32))


def _mem_kv_kernel(x_ref, g_ref, wt_ref, k_ref, v_ref):
    kvt = _dg(wt_ref[0], _rms(x_ref[0], g_ref[0]).astype(BF16), NT)
    k_ref[0, 0] = kvt[:MEM_WIDTH]
    v_ref[0, 0] = kvt[MEM_WIDTH:]


def _mem_kv(mem, norm_mem, w_mem_kv):
    B, n_mem, _ = mem.shape
    out = jax.ShapeDtypeStruct((DEPTH, B, MEM_WIDTH, n_mem), F32)
    wt = jnp.swapaxes(w_mem_kv, 1, 2).astype(BF16)
    return pl.pallas_call(
        _mem_kv_kernel,
        out_shape=(out, out),
        grid=(DEPTH, B),
        in_specs=[pl.BlockSpec((1, n_mem, D_MODEL), lambda l, b: (b, 0, 0)),
                  pl.BlockSpec((1, 1, D_MODEL), lambda l, b: (l, 0, 0)),
                  pl.BlockSpec((1, 2 * MEM_WIDTH, D_MODEL), lambda l, b: (l, 0, 0))],
        out_specs=(pl.BlockSpec((1, 1, MEM_WIDTH, n_mem), lambda l, b: (l, b, 0, 0)),
                   pl.BlockSpec((1, 1, MEM_WIDTH, n_mem), lambda l, b: (l, b, 0, 0))),
        compiler_params=_cparams(2),
        name="mem_kv",
    )(mem, norm_mem.reshape(DEPTH, 1, D_MODEL).astype(F32), wt)


def _segsum64(x):
    r = lax.broadcasted_iota(jnp.int32, (128, 128), 0) >> 6
    c = lax.broadcasted_iota(jnp.int32, (128, 128), 1) >> 6
    ones = jnp.where(r == c, 1.0, 0.0).astype(BF16)
    outs = []
    for j in range(x.shape[1] // 128):
        hi, lo = _split(x[:, j * 128:(j + 1) * 128])
        outs.append(_dg(hi, ones) + _dg(lo, ones))
    return jnp.concatenate(outs, axis=1)


def _softplus(x):
    return jnp.maximum(x, 0.0) + jnp.log1p(jnp.exp(-jnp.abs(x)))


def _prep_kernel(*refs, has_vres, blocks_per_seq):
    if blocks_per_seq:
        x_ref, gmix_ref, win_ref = refs[:3]
        refs = refs[2:]
    (cur_ref, prev_ref, mu_ref, wd_ref, wa_ref, wg_ref, wv_ref, w0_ref, a0_ref, v0_ref, kk_ref, ka_ref) = refs[:12]
    k = 12
    vf_ref = refs[k] if has_vres else None
    k += int(has_vres)
    r_o, lw_o, k_o, v_o, kn_o, b_o, g_o, qm_o = refs[k:k + 8]

    if blocks_per_seq:
        cur_ref, carry_scr = refs[k + 8], refs[k + 9]
        xn = _rms(x_ref[...], gmix_ref[...]).astype(BF16)
        for lo in range(0, PA_COLS, 512):
            cur_ref[:, lo:lo + 512] = _dg(xn, win_ref[:, lo:lo + 512])
    tm = cur_ref.shape[0]

    if blocks_per_seq:

        @pl.when(pl.program_id(0) % blocks_per_seq == 0)
        def _():
            carry_scr[...] = prev_ref[0]

        first_row = lax.broadcasted_iota(jnp.int32, (tm, 1), 0) == 0

    def mix(lo, hi):
        c = cur_ref[:, lo:hi]
        if blocks_per_seq:
            p = jnp.where(first_row, carry_scr[:, lo:hi], pltpu.roll(c, 1, 0))
        else:
            p = prev_ref[:, lo:hi]
        return c + (p - c) * mu_ref[:, lo:hi]

    lora = mix(PA_LORA, PA_LORA + 128)
    zw = _dg(jnp.tanh(lora).astype(BF16), wd_ref[...]) + w0_ref[...]
    w_log = -_softplus(-zw) - 0.5
    lw_o[...] = -jnp.exp(w_log)
    a = jax.nn.sigmoid(_dg(lora.astype(BF16), wa_ref[...]) + a0_ref[...])
    gin = mix(PA_GATE, PA_GATE + 256)
    g_o[...] = _dg(jax.nn.sigmoid(gin).astype(BF16), wg_ref[...])
    r_o[...] = mix(PA_R, PA_R + MIX_A)
    kx = mix(PA_K, PA_K + MIX_A)
    v = mix(PA_V, PA_V + MIX_A)
    if has_vres:
        sv = jax.nn.sigmoid(_dg(gin.astype(BF16), wv_ref[...]) + v0_ref[...])
        v = v + (vf_ref[...] - v) * sv
    v_o[...] = v
    kk = kx * kk_ref[...]
    kk = kk * lax.rsqrt(jnp.maximum(_segsum64(kk * kk), 1e-24))
    kn_o[...] = kk
    b_o[...] = kk * a
    k_o[...] = kx * (1.0 + (a - 1.0) * ka_ref[...])
    qm_o[...] = cur_ref[:, PA_QMEM:PA_QMEM + MEM_WIDTH].astype(BF16)
    if blocks_per_seq:
        carry_scr[...] = cur_ref[tm - 1:tm, :]


def _rwkv_prep(x, sp, pw, v_first, g_mix, *, nseq, T):
    M = x.shape[0]
    has_vres = v_first is not None
    row = lambda w: pl.BlockSpec((tm, w), lambda i: (i, 0))
    full = lambda a: pl.BlockSpec(a.shape, lambda i: (0,) * a.ndim)
    if T >= 256:
        tm = 256
        blocks_per_seq = T // tm
        g2 = g_mix.reshape(1, D_MODEL).astype(F32)
        lead_args = [x, g2, pw["w_in"], sp.reshape(nseq, 1, PA_COLS)]
        lead_specs = [row(D_MODEL), full(g2),
                      pl.BlockSpec(pw["w_in"].shape, lambda i: (0, 0), pipeline_mode=pl.Buffered(1)),
                      pl.BlockSpec((1, 1, PA_COLS), lambda i: (i // blocks_per_seq, 0, 0))]
        scratch = [pltpu.VMEM((tm, PA_COLS), F32), pltpu.VMEM((1, PA_COLS), F32)]
    else:
        tm = _pick_tile(M, (256, 128, 64, 32, 16, 8))
        blocks_per_seq = 0
        cur = _mm([x], [pw["w_in"]], g=g_mix, name="rwkv_in")
        prev = jnp.concatenate([sp[:, None, :], cur.reshape(nseq, T, PA_COLS)[:, :-1]], axis=1).reshape(M, PA_COLS)
        lead_args = [cur, prev]
        lead_specs = [row(PA_COLS), row(PA_COLS)]
        scratch = []
    consts = [pw["mu"], pw["wd"], pw["wa"], pw["wg"], pw["wv"], pw["w0"], pw["a0"], pw["v0"], pw["k_k"], pw["k_a"]]
    args = lead_args + consts
    in_specs = lead_specs + [full(a) for a in consts]
    if has_vres:
        args.append(v_first)
        in_specs.append(row(MIX_A))
    outs = [jax.ShapeDtypeStruct((M, MIX_A), F32)] * 7 + [jax.ShapeDtypeStruct((M, MEM_WIDTH), BF16)]
    out_specs = [row(MIX_A)] * 7 + [row(MEM_WIDTH)]
    return pl.pallas_call(
        functools.partial(_prep_kernel, has_vres=has_vres, blocks_per_seq=blocks_per_seq),
        out_shape=outs,
        grid=(M // tm,),
        in_specs=in_specs,
        out_specs=out_specs,
        scratch_shapes=scratch,
        compiler_params=_cparams(1),
        name="rwkv_prep",
    )(*args)


def _wkv_kernel(r_ref, lw_ref, k_ref, v_ref, kn_ref, b_ref, g_ref, rk_ref, lnw_ref, lnb_ref, h0_ref,
                o_ref, hT_ref, h_scr, *, C, G, H, N, nsteps, chain, xpose):
    c_idx = pl.program_id(1)
    S = 1 if chain else G
    GC = G * C
    log2c = C.bit_length() - 1

    eye_bf = jnp.where(lax.broadcasted_iota(jnp.int32, (N, N), 0) == lax.broadcasted_iota(jnp.int32, (N, N), 1),
                       1.0, 0.0).astype(BF16)

    @pl.when(c_idx == 0)
    def _():
        for s in range(S):
            for h in range(H):
                h_scr[s * H + h] = _transpose_exact(h0_ref[s, h], eye_bf) if xpose else h0_ref[s, h]

    lw = lw_ref[...]
    gi = lax.broadcasted_iota(jnp.int32, (GC, GC), 0)
    gj = lax.broadcasted_iota(jnp.int32, (GC, GC), 1)
    same = (gi >> log2c) == (gj >> log2c)
    ltri = jnp.where(same & (gj <= gi), 1.0, 0.0).astype(BF16)
    ones_g = jnp.where(same, 1.0, 0.0).astype(BF16)
    lw_hi, lw_lo = _split(lw)
    cum = _dg(ltri, lw_hi) + _dg(ltri, lw_lo)
    tot = _dg(ones_g, lw_hi) + _dg(ones_g, lw_lo)
    e_neg = jnp.exp(-cum)
    r_all = r_ref[...]
    k_all = k_ref[...]
    v_all = v_ref[...]
    kt_all = kn_ref[...] * jnp.exp(cum - lw)
    rt_all = r_all * jnp.exp(cum)
    kh_all = k_all * e_neg
    bh_all = b_ref[...] * e_neg
    e_rem = jnp.exp(tot - cum)
    kb_all = k_all * e_rem
    bb_all = b_ref[...] * e_rem
    p_last = jnp.exp(tot)
    rkr = r_all * k_all * rk_ref[...]

    ti = lax.broadcasted_iota(jnp.int32, (C, C), 0)
    tj = lax.broadcasted_iota(jnp.int32, (C, C), 1)
    strict = tj < ti
    incl = tj <= ti
    ni = lax.broadcasted_iota(jnp.int32, (N, N), 0)
    nj = lax.broadcasted_iota(jnp.int32, (N, N), 1)
    eye_n = ni == nj
    eye_c = jnp.where(ti == tj, 1.0, 0.0)

    items = [(g, h) for g in range(G) for h in range(H)]
    it = range(len(items))
    rows = [slice(g * C, (g + 1) * C) for g, h in items]
    sls = [slice(h * N, (h + 1) * N) for g, h in items]
    cut = lambda a: [a[rows[i], sls[i]].astype(BF16) for i in it]
    kt, rt, kh, bh, kb, bb, vv = (cut(a) for a in (kt_all, rt_all, kh_all, bh_all, kb_all, bb_all, v_all))
    a_k = [jnp.where(strict, _dg(kt[i], kh[i], NT), 0.0).astype(BF16) for i in it]
    a_b = [jnp.where(strict, _dg(kt[i], bh[i], NT), 0.0) for i in it]
    q_k = [jnp.where(incl, _dg(rt[i], kh[i], NT), 0.0).astype(BF16) for i in it]
    q_b = [jnp.where(incl, _dg(rt[i], bh[i], NT), 0.0).astype(BF16) for i in it]
    tinv = [eye_c - jnp.where(((ti >> 1) == (tj >> 1)), a_b[i], 0.0) for i in it]
    blk = 2
    while blk < C:
        sh = blk.bit_length() - 1
        lvl = ((ti >> (sh + 1)) == (tj >> (sh + 1))) & (((ti >> sh) & 1) == 1) & (((tj >> sh) & 1) == 0)
        t16 = [tinv[i].astype(BF16) for i in it]
        mt = [_dg(jnp.where(lvl, a_b[i], 0.0).astype(BF16), t16[i]).astype(BF16) for i in it]
        tinv = [tinv[i] - _dg(t16[i], mt[i]) for i in it]
        blk *= 2
    t16 = [tinv[i].astype(BF16) for i in it]
    akv = [_dg(a_k[i], vv[i]).astype(BF16) for i in it]
    x = [_dg(t16[i], jnp.concatenate([kt[i], akv[i]], axis=1)).astype(BF16) for i in it]
    qw = [_dg(q_b[i], x[i]) for i in it]
    r2 = [(rt_all[rows[i], sls[i]] - qw[i][:, :N]).astype(BF16) for i in it]
    y0 = [_dg(q_k[i], vv[i]) - qw[i][:, N:] for i in it]
    bw = [_dg(bb[i], x[i], TN) for i in it]
    gmat = [(jnp.where(eye_n, jnp.broadcast_to(p_last[g * C:g * C + 1, sls[i]], (N, N)), 0.0)
             - bw[i][:, :N]).astype(BF16) for i, (g, h) in enumerate(items)]
    h_add = [_dg(kb[i], vv[i], TN) - bw[i][:, N:] for i in it]
    state = [h_scr[s * H + h] for s in range(S) for h in range(H)]
    y = [None] * len(items)
    for i, (g, h) in enumerate(items):
        si = h if chain else g * H + h
        h0 = state[si].astype(BF16)
        y[i] = _dg(r2[i], h0) + y0[i]
        state[si] = _dg(gmat[i], h0) + h_add[i]
    for si in range(S * H):
        h_scr[si] = state[si]
    for i in it:
        rs, sl = rows[i], sls[i]
        mean = jnp.mean(y[i], axis=-1, keepdims=True)
        yc = y[i] - mean
        var = jnp.mean(yc * yc, axis=-1, keepdims=True)
        yn = yc * lax.rsqrt(var + GN_EPS)
        yn = yn * lnw_ref[:, sl] + lnb_ref[:, sl]
        bonus = jnp.sum(rkr[rs, sl], axis=-1, keepdims=True) * v_all[rs, sl]
        o_ref[rs, sl] = ((yn + bonus) * g_ref[rs, sl]).astype(o_ref.dtype)

    @pl.when(c_idx == nsteps - 1)
    def _():
        for s in range(S):
            for h in range(H):
                hT_ref[s, h] = _transpose_exact(h_scr[s * H + h], eye_bf) if xpose else h_scr[s * H + h]


def _wkv(r, lw, k, v, kn, b, g, r_k, lnx_w, lnx_b, h0, *, nseq, T, C, G=None):
    M = r.shape[0]
    nchunks = T // C
    HN = MIX_A
    chain = nchunks > 1
    if G is None:
        G = 4
    if chain:
        nsteps, S, grid = nchunks // G, 1, (nseq, nchunks // G)
        row = pl.BlockSpec((G * C, HN), lambda s, c: (s * nsteps + c, 0))
        h_in, out_dtype = h0, BF16
    else:
        nsteps, S, grid = 1, G, (nseq // G, 1)
        row = pl.BlockSpec((G * C, HN), lambda s, c: (s, 0))
        h_in, out_dtype = jnp.swapaxes(h0, -1, -2), F32
    vec = pl.BlockSpec((1, HN), lambda s, c: (0, 0))
    st = pl.BlockSpec((S, H_A, HEAD_A, HEAD_A), lambda s, c: (s, 0, 0, 0))
    o, h_new = pl.pallas_call(
        functools.partial(_wkv_kernel, C=C, G=G, H=H_A, N=HEAD_A, nsteps=nsteps, chain=chain, xpose=chain),
        out_shape=(jax.ShapeDtypeStruct((M, HN), out_dtype),
                   jax.ShapeDtypeStruct((nseq, H_A, HEAD_A, HEAD_A), F32)),
        grid=grid,
        in_specs=[row] * 7 + [vec] * 3 + [st],
        out_specs=(row, st),
        scratch_shapes=[pltpu.VMEM((S * H_A, HEAD_A, HEAD_A), F32)],
        compiler_params=_cparams(2),
        name="wkv7",
    )(r, lw, k, v, kn, b, g, r_k.reshape(1, HN), lnx_w.reshape(1, HN), lnx_b.reshape(1, HN), h_in)
    return o, (h_new if chain else jnp.swapaxes(h_new, -1, -2))


def _mem_attn_kernel(q_ref, k_ref, v_ref, o_ref, *, nbb, tq):
    c = MEM_SCALE * math.log2(math.e)
    lane_head = lax.broadcasted_iota(jnp.int32, (1, MEM_WIDTH), 1) >> 6
    items = [(bb, h) for bb in range(nbb) for h in range(MEM_HEADS)]
    q_all = q_ref[...].astype(F32)
    qs = [q_all[bb * tq:(bb + 1) * tq, :] for bb in range(nbb)]
    kts = [k_ref[0, bb].astype(BF16) for bb in range(nbb)]
    vts = [v_ref[0, bb].astype(BF16) for bb in range(nbb)]
    ss = [_dg(jnp.where(lane_head == h, qs[bb], 0.0).astype(BF16), kts[bb]) for bb, h in items]
    ms = [jnp.max(s, axis=-1, keepdims=True) for s in ss]
    ps = [jnp.exp2((s - m) * c) for s, m in zip(ss, ms)]
    inv_l = [1.0 / jnp.sum(p, axis=-1, keepdims=True) for p in ps]
    pv = [_dg(p.astype(BF16), vts[bb], NT) for p, (bb, h) in zip(ps, items)]
    outs = []
    for bb in range(nbb):
        o = jnp.zeros((tq, MEM_WIDTH), F32)
        for h in range(MEM_HEADS):
            idx = bb * MEM_HEADS + h
            o = jnp.where(lane_head == h, pv[idx] * inv_l[idx], o)
        outs.append(o)
    o_ref[...] = (outs[0] if nbb == 1 else jnp.concatenate(outs, axis=0)).astype(o_ref.dtype)


def _mem_attn(q, mem_k, mem_v, layer, *, nb, T):
    M = q.shape[0]
    if T >= ROW_TILES[0]:
        tq, nbb = ROW_TILES[0], 1
    else:
        tq, nbb = T, _pick_tile(nb, (16, 8, 4, 2, 1))
    nq = T // tq
    n_mem = mem_k.shape[3]
    return pl.pallas_call(
        functools.partial(_mem_attn_kernel, nbb=nbb, tq=tq),
        out_shape=jax.ShapeDtypeStruct((M, MEM_WIDTH), BF16),
        grid=(nb // nbb, nq),
        in_specs=[pl.BlockSpec((nbb * tq, MEM_WIDTH), lambda b, i: (b * nq + i, 0)),
                  pl.BlockSpec((1, nbb, MEM_WIDTH, n_mem), lambda b, i: (layer, b, 0, 0)),
                  pl.BlockSpec((1, nbb, MEM_WIDTH, n_mem), lambda b, i: (layer, b, 0, 0))],
        out_specs=pl.BlockSpec((nbb * tq, MEM_WIDTH), lambda b, i: (b * nq + i, 0)),
        compiler_params=_cparams(2),
        name="mem_attn",
    )(q, mem_k, mem_v)


def _ffn_kernel(*refs, tm, T, long_mode, col_chunks, final_norm):
    (x_ref, ot_ref, om_ref, wot_ref, wom_ref, g_ref, wg_ref, wv_ref, cw_ref, cb_ref, wd_ref, cp_ref) = refs[:12]
    k = 12
    fin_ref = refs[k] if final_norm else None
    k += int(final_norm)
    o_ref, aux_ref = refs[k:k + 2]
    k += 2
    y_ref = refs[k] if final_norm else None
    k += int(final_norm)
    xn_scr, h_scr, x1_scr = refs[k:k + 3]
    if long_mode:
        carry_scr = refs[k + 3]
        tail_ref = aux_ref
    else:
        gate_ref = aux_ref
    x1_scr[...] = x_ref[...] + (_dg(ot_ref[...].astype(BF16), wot_ref[...]) + _dg(om_ref[...], wom_ref[...]))
    xn_scr[...] = _rms(x1_scr[...], g_ref[...]).astype(BF16)
    xn = xn_scr[...]
    row = lax.broadcasted_iota(jnp.int32, (tm, 1), 0)
    if long_mode:
        nblk = T // tm

        @pl.when(pl.program_id(0) % nblk == 0)
        def _():
            carry_scr[...] = cp_ref[0]
    else:
        t_in_seq = row % T
        ncp = cp_ref.shape[0]
        rr = lax.broadcasted_iota(jnp.int32, (tm, ncp), 0)
        cc = lax.broadcasted_iota(jnp.int32, (tm, ncp), 1)
        same_seq = (rr // T) == (cc >> 1)
        sel1 = jnp.where(same_seq & ((rr % T) + 1 == (cc & 1)), 1.0, 0.0).astype(BF16)
        sel2 = jnp.where(same_seq & ((rr % T) == (cc & 1)), 1.0, 0.0).astype(BF16)
        sel = jnp.concatenate([sel1, sel2], axis=0)
    for (lo, w) in col_chunks:
        cs = slice(lo, lo + w)
        gate = _dg(xn, wg_ref[:, cs])
        val = _dg(xn, wv_ref[:, cs])
        r1 = pltpu.roll(gate, 1, 0)
        r2 = pltpu.roll(gate, 2, 0)
        if long_mode:
            c0 = carry_scr[6:7, cs]
            c1 = carry_scr[7:8, cs]
            p1 = jnp.where(row == 0, c1, r1)
            p2 = jnp.where(row == 0, c0, jnp.where(row == 1, c1, r2))
            tail = gate[tm - 8:tm, :]
            carry_scr[:, cs] = tail
            tail_ref[0, :, cs] = tail
        else:
            cpc = cp_ref[:, cs]
            hi = cpc.astype(BF16)
            r_hi = cpc - hi.astype(F32)
            mid = r_hi.astype(BF16)
            lo = (r_hi - mid.astype(F32)).astype(BF16)
            edge = (_dg(sel, hi) + _dg(sel, mid)) + _dg(sel, lo)
            p1 = jnp.where(t_in_seq >= 1, r1, 0.0) + edge[:tm]
            p2 = jnp.where(t_in_seq >= 2, r2, 0.0) + edge[tm:]
            gate_ref[:, cs] = gate
        conv = cb_ref[:, cs] + p2 * cw_ref[0:1, cs]
        conv = conv + p1 * cw_ref[1:2, cs]
        conv = conv + gate * cw_ref[2:3, cs]
        h_scr[:, cs] = (conv * jax.nn.sigmoid(conv) * val).astype(BF16)
    out = x1_scr[...] + _dg(h_scr[...], wd_ref[...])
    o_ref[...] = out
    if final_norm:
        y_ref[...] = _rms(out, fin_ref[...])


def _ffn(x, o_tok, o_mem, w_o_tok, w_o_mem, g, w_gate, w_val, conv_w, conv_b, w_down, conv_prev, final_g=None, *,
         nseq, T):
    M = x.shape[0]
    long_mode = T >= 512
    tm = 512 if long_mode else 256
    assert (T % tm == 0) if long_mode else (tm % T == 0 and M % tm == 0)
    col_chunks = tuple((lo, min(256, D_FF - lo)) for lo in range(0, D_FF, 256))
    const = lambda a: pl.BlockSpec(a.shape, lambda i: (0,) * a.ndim, pipeline_mode=pl.Buffered(1))
    small = lambda a: pl.BlockSpec(a.shape, lambda i: (0,) * a.ndim)
    g2 = g.reshape(1, D_MODEL).astype(F32)
    cb2 = conv_b.reshape(1, D_FF).astype(F32)
    rows = lambda w: pl.BlockSpec((tm, w), lambda i: (i, 0))
    common_specs = [rows(D_MODEL), rows(MIX_A), rows(MEM_WIDTH), const(w_o_tok), const(w_o_mem), small(g2),
                    const(w_gate), const(w_val), small(conv_w), small(cb2), const(w_down)]
    common_args = [x, o_tok, o_mem, w_o_tok, w_o_mem, g2, w_gate, w_val, conv_w, cb2, w_down]
    final_norm = final_g is not None
    fin_args = [final_g.reshape(1, D_MODEL).astype(F32)] if final_norm else []
    fin_specs = [pl.BlockSpec((1, D_MODEL), lambda i: (0, 0))] if final_norm else []
    fin_out = [jax.ShapeDtypeStruct((M, D_MODEL), F32)] if final_norm else []
    fin_out_specs = [rows(D_MODEL)] if final_norm else []
    base_scratch = [pltpu.VMEM((tm, D_MODEL), BF16), pltpu.VMEM((tm, D_FF), BF16), pltpu.VMEM((tm, D_MODEL), F32)]
    kern = functools.partial(_ffn_kernel, tm=tm, T=T, long_mode=long_mode, col_chunks=col_chunks,
                             final_norm=final_norm)
    if long_mode:
        nblk = T // tm
        cp = jnp.concatenate([jnp.zeros((nseq, 6, D_FF), F32), conv_prev.astype(F32)], axis=1)
        res = pl.pallas_call(
            kern,
            out_shape=[jax.ShapeDtypeStruct((M, D_MODEL), F32), jax.ShapeDtypeStruct((M // tm, 8, D_FF), F32)] + fin_out,
            grid=(M // tm,),
            in_specs=common_specs + [pl.BlockSpec((1, 8, D_FF), lambda i: (i // nblk, 0, 0))] + fin_specs,
            out_specs=[rows(D_MODEL), pl.BlockSpec((1, 8, D_FF), lambda i: (i, 0, 0))] + fin_out_specs,
            scratch_shapes=base_scratch + [pltpu.VMEM((8, D_FF), F32)],
            compiler_params=_cparams(1),
            name="convglu_long",
        )(*common_args, cp, *fin_args)
        new_conv = res[1].reshape(nseq, nblk, 8, D_FF)[:, -1, 6:8, :]
        return res[0], new_conv, (res[2] if final_norm else None)
    ncp = (CONV_W - 1) * (tm // T)
    res = pl.pallas_call(
        kern,
        out_shape=[jax.ShapeDtypeStruct((M, D_MODEL), F32), jax.ShapeDtypeStruct((M, D_FF), F32)] + fin_out,
        grid=(M // tm,),
        in_specs=common_specs + [pl.BlockSpec((ncp, D_FF), lambda i: (i, 0))] + fin_specs,
        out_specs=[rows(D_MODEL), rows(D_FF)] + fin_out_specs,
        scratch_shapes=base_scratch,
        compiler_params=_cparams(1),
        name="convglu_short",
    )(*common_args, conv_prev.astype(F32).reshape(nseq * (CONV_W - 1), D_FF), *fin_args)
    new_conv = res[1].reshape(nseq, T, D_FF)[:, T - 2:, :]
    return res[0], new_conv, (res[2] if final_norm else None)


def _rope128(x128, cs):
    a = x128 * cs
    s = a + pltpu.roll(a, 64, 1)
    lane = lax.broadcasted_iota(jnp.int32, a.shape, 1)
    return jnp.where(lane < 64, s, 0.0)


def _mla_kv_kernel(*refs, expand):
    x_ref, gk_ref, wa_ref, gc_ref, cs_ref = refs[:5]
    k = 5
    wx_ref = refs[k] if expand else None
    k += int(expand)
    ckv_ref, kpe_ref = refs[k:k + 2]
    kvx_ref = refs[k + 2] if expand else None
    xn = _rms(x_ref[...], gk_ref[...]).astype(BF16)
    h = _dg(xn, wa_ref[...])
    ckv = _rms(h[:, :KV_LORA], gc_ref[...])
    rp = _rope128(h[:, KV_LORA:], cs_ref[...])
    ckv_ref[...] = ckv
    kpe_ref[...] = rp[:, :ROPE_DIM]
    if expand:
        ckpe = jnp.concatenate([ckv, rp], axis=1).astype(BF16)
        for lo in range(0, wx_ref.shape[1], 768):
            kvx_ref[:, lo:lo + 768] = _dg(ckpe, wx_ref[:, lo:lo + 768]).astype(kvx_ref.dtype)


def _mla_kv(x, norm_kv, w_kva, norm_ckv, cs, w_kvx, *, T):
    M = x.shape[0]
    tm = _pos_tile(M, T)
    ncs = cs.shape[0] // tm
    expand = w_kvx is not None
    const = lambda a: pl.BlockSpec(a.shape, lambda i: (0,) * a.ndim)
    rows = lambda w: pl.BlockSpec((tm, w), lambda i: (i, 0))
    gk = norm_kv.reshape(1, D_MODEL).astype(F32)
    gc = norm_ckv.reshape(1, KV_LORA).astype(F32)
    args = [x, gk, w_kva, gc, cs] + ([w_kvx] if expand else [])
    in_specs = [rows(D_MODEL), const(gk), const(w_kva), const(gc),
                pl.BlockSpec((tm, 128), lambda i: (i % ncs, 0))] + ([const(w_kvx)] if expand else [])
    out_shape = [jax.ShapeDtypeStruct((M, KV_LORA), F32), jax.ShapeDtypeStruct((M, ROPE_DIM), F32)]
    out_specs = [rows(KV_LORA), rows(ROPE_DIM)]
    if expand:
        out_shape.append(jax.ShapeDtypeStruct((M, w_kvx.shape[1]), BF16))
        out_specs.append(rows(w_kvx.shape[1]))
    return pl.pallas_call(
        functools.partial(_mla_kv_kernel, expand=expand),
        out_shape=out_shape,
        grid=(M // tm,),
        in_specs=in_specs,
        out_specs=out_specs,
        compiler_params=_cparams(1),
        name="mla_kv",
    )(*args)


def _mla_q_kernel(x_ref, gm_ref, win_ref, gq_ref, wq_ref, cs_ref, q_ref, qm_ref):
    xn = _rms(x_ref[...], gm_ref[...]).astype(BF16)
    proj = _dg(xn, win_ref[...])
    qm_ref[...] = proj[:, Q_LORA:].astype(qm_ref.dtype)
    qn = _rms(proj[:, :Q_LORA], gq_ref[...]).astype(BF16)
    cs = cs_ref[...]
    for h in range(H_B):
        base = h * 256
        qh = _dg(qn, wq_ref[:, base:base + 256])
        q_ref[:, base:base + 128] = qh[:, :128].astype(q_ref.dtype)
        q_ref[:, base + 128:base + 256] = _rope128(qh[:, 128:], cs).astype(q_ref.dtype)


def _mla_q(x, g_mix, w_in, norm_q, w_q, cs, *, T):
    M = x.shape[0]
    tm = _pos_tile(M, T)
    ncs = cs.shape[0] // tm
    const = lambda a: pl.BlockSpec(a.shape, lambda i: (0,) * a.ndim)
    gm = g_mix.reshape(1, D_MODEL).astype(F32)
    gq = norm_q.reshape(1, Q_LORA).astype(F32)
    return pl.pallas_call(
        _mla_q_kernel,
        out_shape=(jax.ShapeDtypeStruct((M, H_B * 256), BF16), jax.ShapeDtypeStruct((M, MEM_WIDTH), BF16)),
        grid=(M // tm,),
        in_specs=[pl.BlockSpec((tm, D_MODEL), lambda i: (i, 0)), const(gm), const(w_in), const(gq), const(w_q),
                  pl.BlockSpec((tm, 128), lambda i: (i % ncs, 0))],
        out_specs=(pl.BlockSpec((tm, H_B * 256), lambda i: (i, 0)), pl.BlockSpec((tm, MEM_WIDTH), lambda i: (i, 0))),
        compiler_params=_cparams(1),
        name="mla_q",
    )(x, gm, w_in, gq, w_q, cs)


def _flash_kernel(q_ref, k_ref, v_ref, o_ref, m_scr, l_scr, acc_scr, *, tq, tk):
    i = pl.program_id(1)
    j = pl.program_id(2)

    @pl.when(j == 0)
    def _():
        m_scr[...] = jnp.full(m_scr.shape, -jnp.inf, F32)
        l_scr[...] = jnp.zeros(l_scr.shape, F32)
        acc_scr[...] = jnp.zeros(acc_scr.shape, F32)

    hs = range(H_B)

    def block(masked):
        c = ATTN_SCALE * math.log2(math.e)
        ss = [_dg(q_ref[:, h * 256:(h + 1) * 256], k_ref[:, h * 256:(h + 1) * 256], NT) for h in hs]
        if masked:
            keep = lax.broadcasted_iota(jnp.int32, (tq, tk), 1) <= lax.broadcasted_iota(jnp.int32, (tq, tk), 0)
            ss = [jnp.where(keep, s, -jnp.inf) for s in ss]
        nkt = tk // 128
        tiles = [[ss[h][:, t * 128:(t + 1) * 128] for t in range(nkt)] for h in hs]
        m_prev = [m_scr[h] for h in hs]
        m_new = []
        for h in hs:
            tmax = tiles[h][0]
            for t in range(1, nkt):
                tmax = jnp.maximum(tmax, tiles[h][t])
            m_new.append(jnp.maximum(m_prev[h], jnp.max(tmax, axis=-1, keepdims=True)))
        alpha = [jnp.exp2((m_prev[h] - m_new[h]) * c) for h in hs]
        ps = [[jnp.exp2((tiles[h][t] - m_new[h]) * c) for t in range(nkt)] for h in hs]
        pv = [_dg(jnp.concatenate([p.astype(BF16) for p in ps[h]], axis=1), v_ref[:, h * V_DIM:(h + 1) * V_DIM])
              for h in hs]
        for h in hs:
            psum = ps[h][0]
            for t in range(1, nkt):
                psum = psum + ps[h][t]
            l_scr[h] = alpha[h] * l_scr[h] + jnp.sum(psum, axis=-1, keepdims=True)
            acc_scr[:, h * V_DIM:(h + 1) * V_DIM] = alpha[h] * acc_scr[:, h * V_DIM:(h + 1) * V_DIM] + pv[h]
            m_scr[h] = m_new[h]

    @pl.when(j < i)
    def _():
        block(False)

    @pl.when(j == i)
    def _():
        block(True)
        for h in hs:
            sl = slice(h * V_DIM, (h + 1) * V_DIM)
            o_ref[:, sl] = (acc_scr[:, sl] / l_scr[h]).astype(o_ref.dtype)


def _flash_causal(q, kv, *, nb, T):
    M = q.shape[0]
    tq = tk = 512
    nq = T // tq
    kw = H_B * 256
    vw = H_B * V_DIM
    return pl.pallas_call(
        functools.partial(_flash_kernel, tq=tq, tk=tk),
        out_shape=jax.ShapeDtypeStruct((M, vw), BF16),
        grid=(nb, nq, nq),
        in_specs=[pl.BlockSpec((tq, kw), lambda b, i, j: (b * nq + i, 0)),
                  pl.BlockSpec((tk, kw), lambda b, i, j: (b * nq + jnp.minimum(i, j), 0)),
                  pl.BlockSpec((tk, vw), lambda b, i, j: (b * nq + jnp.minimum(i, j), kw // vw))],
        out_specs=pl.BlockSpec((tq, vw), lambda b, i, j: (b * nq + i, 0)),
        scratch_shapes=[pltpu.VMEM((H_B, tq, 128), F32), pltpu.VMEM((H_B, tq, 128), F32), pltpu.VMEM((tq, vw), F32)],
        compiler_params=_cparams(3),
        name="mla_causal",
    )(q, kv, kv)


def _decode_kernel(pt_ref, q_ref, ckn_ref, kpn_ref, wuk_ref, wuv_ref, ck_hbm, kp_hbm, o_ref,
                   ck_buf, kp_buf, sem, m_scr, l_scr, acc_scr, *, PP, n_steps, n_total, TS):
    j = pl.program_id(1)
    n = pl.program_id(0) * n_steps + j
    slot = lax.rem(n, 2)
    R = H_B * TS

    def group_copies(step, slot_, for_wait):
        cps = []
        for p in range(PP):
            page = 0 if for_wait else pt_ref[step * PP + p]
            cps.append(pltpu.make_async_copy(ck_hbm.at[page], ck_buf.at[slot_, p], sem.at[slot_, 0]))
            cps.append(pltpu.make_async_copy(kp_hbm.at[page], kp_buf.at[slot_, p], sem.at[slot_, 1]))
        return cps

    @pl.when(n == 0)
    def _():
        for cp in group_copies(0, 0, False):
            cp.start()

    nxt = jnp.minimum(n + 1, n_total - 1)
    for i, cp in enumerate(group_copies(nxt, 1 - slot, False)):
        cp.start(priority=(i // 2) % 2)
    for cp in group_copies(n, slot, True):
        cp.wait()

    @pl.when(j == 0)
    def _():
        m_scr[...] = jnp.full(m_scr.shape, -jnp.inf, F32)
        l_scr[...] = jnp.zeros(l_scr.shape, F32)
        acc_scr[...] = jnp.zeros(acc_scr.shape, F32)

    q_all = q_ref[0].astype(F32)
    q_lat = jnp.concatenate(
        [_dg(q_all[:, h * 256:h * 256 + NOPE_DIM].astype(BF16), wuk_ref[h]) for h in range(H_B)], axis=0).astype(BF16)
    q_pe = jnp.concatenate(
        [q_all[:, h * 256 + NOPE_DIM:h * 256 + NOPE_DIM + ROPE_DIM] for h in range(H_B)], axis=0).astype(BF16)
    c = ATTN_SCALE * math.log2(math.e)

    cks = [ck_buf[slot, p].astype(BF16) for p in range(PP)]
    kps = [kp_buf[slot, p].astype(BF16) for p in range(PP)]
    ss = [_dg(q_lat, cks[p], NT) + _dg(q_pe, kps[p]) for p in range(PP)]
    pad = PAGE_SIZE - TS
    ckn = jnp.concatenate([ckn_ref[0], jnp.zeros((pad, KV_LORA), F32)], axis=0).astype(BF16)
    kpn = jnp.concatenate([kpn_ref[0], jnp.zeros((pad, ROPE_DIM), F32)], axis=0).astype(BF16)
    t_q = lax.broadcasted_iota(jnp.int32, (R, PAGE_SIZE), 0) % TS
    col = lax.broadcasted_iota(jnp.int32, (R, PAGE_SIZE), 1)
    last_shift = jnp.where(j == n_steps - 1, 0, -PAGE_SIZE)
    s_new = jnp.where(col <= t_q + last_shift, _dg(q_lat, ckn, NT) + _dg(q_pe, kpn, NT), -jnp.inf)
    ss.append(s_new)
    cks.append(ckn)

    tile_max = ss[0]
    for s in ss[1:]:
        tile_max = jnp.maximum(tile_max, s)
    m_prev = m_scr[...]
    m_new = jnp.maximum(m_prev, jnp.max(tile_max, axis=-1, keepdims=True))
    alpha = jnp.exp2((m_prev - m_new) * c)
    ps = [jnp.exp2((s - m_new) * c) for s in ss]
    p_sum = ps[0]
    for p in ps[1:]:
        p_sum = p_sum + p
    acc = jnp.concatenate([alpha, alpha], axis=1) * acc_scr[...]
    for p, vv in zip(ps, cks):
        acc = acc + _dg(p.astype(BF16), vv)
    m_scr[...] = m_new
    l_scr[...] = alpha * l_scr[...] + jnp.sum(p_sum, axis=-1, keepdims=True)
    acc_scr[...] = acc

    @pl.when(j == n_steps - 1)
    def _():
        o_lat = acc_scr[...] / jnp.concatenate([l_scr[...], l_scr[...]], axis=1)
        for h in range(H_B):
            o_h = o_lat[h * TS:(h + 1) * TS, :].astype(BF16)
            o_ref[0, :, h * V_DIM:(h + 1) * V_DIM] = _dg(o_h, wuv_ref[h]).astype(o_ref.dtype)

    @pl.when(n == n_total - 1)
    def _():
        for cp in group_copies(n, 1 - slot, True):
            cp.wait()


def _decode_attn(page_table, q, cache_ckv, cache_kpe, ckv_new, kpe_new, w_ukT, w_uv, *, PP=64):
    nb, n_pages = page_table.shape
    TS = ckv_new.shape[1]
    R = H_B * TS
    n_steps = n_pages // PP
    pt = page_table.reshape(-1).astype(jnp.int32)
    kpe_t = jnp.swapaxes(cache_kpe, 1, 2)
    grid_spec = pltpu.PrefetchScalarGridSpec(
        num_scalar_prefetch=1,
        grid=(nb, n_steps),
        in_specs=[pl.BlockSpec((1, TS, H_B * 256), lambda b, j, pt_ref: (b, 0, 0)),
                  pl.BlockSpec((1, TS, KV_LORA), lambda b, j, pt_ref: (b, 0, 0)),
                  pl.BlockSpec((1, TS, ROPE_DIM), lambda b, j, pt_ref: (b, 0, 0)),
                  pl.BlockSpec((H_B, NOPE_DIM, KV_LORA), lambda b, j, pt_ref: (0, 0, 0)),
                  pl.BlockSpec((H_B, KV_LORA, V_DIM), lambda b, j, pt_ref: (0, 0, 0)),
                  pl.BlockSpec(memory_space=pl.ANY),
                  pl.BlockSpec(memory_space=pl.ANY)],
        out_specs=pl.BlockSpec((1, TS, H_B * V_DIM), lambda b, j, pt_ref: (b, 0, 0)),
        scratch_shapes=[pltpu.VMEM((2, PP, PAGE_SIZE, KV_LORA), F32),
                        pltpu.VMEM((2, PP, ROPE_DIM, PAGE_SIZE), F32),
                        pltpu.SemaphoreType.DMA((2, 2)),
                        pltpu.VMEM((R, 128), F32), pltpu.VMEM((R, 128), F32), pltpu.VMEM((R, KV_LORA), F32)],
    )
    return pl.pallas_call(
        functools.partial(_decode_kernel, PP=PP, n_steps=n_steps, n_total=nb * n_steps, TS=TS),
        out_shape=jax.ShapeDtypeStruct((nb, TS, H_B * V_DIM), BF16),
        grid_spec=grid_spec,
        compiler_params=_cparams(2),
        name="mla_decode",
    )(pt, q, ckv_new, kpe_new, w_ukT, w_uv, cache_ckv, kpe_t)


def _rope_table(pos):
    half = ROPE_DIM // 2
    inv = ROPE_BASE ** (-jnp.arange(half, dtype=F32) / half)
    ang = pos.astype(F32)[:, None] * inv[None, :]
    cos, sin = jnp.cos(ang), jnp.sin(ang)
    return jnp.concatenate([cos, cos, -sin, sin], axis=-1)


def _swap_halves(w):
    half = ROPE_DIM // 2
    return jnp.concatenate([w[..., half:], w[..., :half]], axis=-1)


def _prep_rwkv_weights(l, P):
    w_in = P["w_in_a"][l]
    mu = P["mu_a"][l]
    cuts = [0, MIX_A, MIX_A + DECAY_LORA, 2 * MIX_A + DECAY_LORA, 3 * MIX_A + DECAY_LORA,
            3 * MIX_A + DECAY_LORA + A_LORA, C_RWKV, C_A]
    seg = lambda a, i: a[..., cuts[i]:cuts[i + 1]]
    r_w, wl_w, k_w, v_w, al_w, gl_w, qm_w = [seg(w_in, i) for i in range(7)]
    r_m, wl_m, k_m, v_m, al_m, gl_m = [seg(mu, i) for i in range(6)]
    zc = lambda n: jnp.zeros((D_MODEL, n), F32)
    zm = lambda n: jnp.zeros((n,), F32)
    if l > 0:
        vl_w, vl_m = P["w_vres_in"][l - 1], P["mu_vres"][l - 1]
        wv_up, v0 = P["w_vres_up"][l - 1], P["v0"][l - 1]
    else:
        vl_w, vl_m = zc(VRES_LORA), zm(VRES_LORA)
        wv_up, v0 = jnp.zeros((VRES_LORA, MIX_A), F32), zm(MIX_A)
    w_full = jnp.concatenate([r_w, k_w, v_w, wl_w, al_w, gl_w, vl_w, zc(64), qm_w, zc(128)], axis=1)
    mu_full = jnp.concatenate([r_m, k_m, v_m, wl_m, al_m, gl_m, vl_m, zm(64), zm(MEM_WIDTH), zm(128)])
    zr = lambda n: jnp.zeros((n, MIX_A), F32)
    return dict(
        w_in=w_full.astype(BF16),
        mu=mu_full.reshape(1, PA_COLS),
        wd=jnp.concatenate([P["w_decay_up"][l], zr(A_LORA)], axis=0).astype(BF16),
        wa=jnp.concatenate([zr(DECAY_LORA), P["w_a_up"][l]], axis=0).astype(BF16),
        wg=jnp.concatenate([P["w_g_up"][l], zr(256 - GATE_LORA)], axis=0).astype(BF16),
        wv=jnp.concatenate([zr(GATE_LORA), wv_up, zr(256 - GATE_LORA - VRES_LORA)], axis=0).astype(BF16),
        w0=P["w0"][l].reshape(1, MIX_A), a0=P["a0"][l].reshape(1, MIX_A), v0=v0.reshape(1, MIX_A),
        k_k=P["k_k"][l].reshape(1, MIX_A), k_a=P["k_a"][l].reshape(1, MIX_A),
    )


def _prep_mla_weights(P):
    w_kv_a = P["w_kv_a"]
    w_kva = jnp.concatenate([w_kv_a, _swap_halves(w_kv_a[:, KV_LORA:])], axis=1).astype(BF16)
    w_kv_b = P["w_kv_b"]
    w_uk, w_uv = w_kv_b[..., :NOPE_DIM], w_kv_b[..., NOPE_DIM:]
    k_part = jnp.zeros((KV_LORA + 128, H_B, 256), F32)
    k_part = k_part.at[:KV_LORA, :, :NOPE_DIM].set(w_uk)
    eye = jnp.eye(ROPE_DIM, dtype=F32)
    k_part = k_part.at[KV_LORA:KV_LORA + ROPE_DIM, :, NOPE_DIM:NOPE_DIM + ROPE_DIM].set(
        jnp.broadcast_to(eye[:, None, :], (ROPE_DIM, H_B, ROPE_DIM)))
    v_part = jnp.zeros((KV_LORA + 128, H_B, V_DIM), F32).at[:KV_LORA].set(w_uv)
    w_kvx = jnp.concatenate([k_part.reshape(KV_LORA + 128, H_B * 256), v_part.reshape(KV_LORA + 128, H_B * V_DIM)],
                            axis=1).astype(BF16)
    w_q = []
    for j in range(N_B):
        wq = P["w_q_b"][j].reshape(Q_LORA, H_B, NOPE_DIM + ROPE_DIM)
        rope_w = wq[..., NOPE_DIM:]
        w_q.append(jnp.concatenate([wq[..., :NOPE_DIM], rope_w, _swap_halves(rope_w)], axis=-1)
                   .reshape(Q_LORA, H_B * 256).astype(BF16))
    return dict(
        w_kva=w_kva, w_kvx=w_kvx, w_q=w_q,
        w_ukT=jnp.transpose(w_uk, (1, 2, 0)).astype(BF16),
        w_uv=jnp.transpose(w_uv, (1, 0, 2)).astype(BF16),
    )


def _trunk(x, pos, mem_k, mem_v, wkv0, shift0, conv0, past, P, W, *, nseq, T):
    M = nseq * T
    x = x.reshape(M, D_MODEL)
    cs = _rope_table(pos)
    if T < ROW_TILES[0]:
        cs = jnp.tile(cs, (nseq, 1))
    chunk = 64 if T >= 64 else T
    new_wkv, new_shift, new_conv = [], [], []
    v_first = None
    ckv = kpe = ckpe = kvx = None
    for l in range(DEPTH):
        g_mix = P["norm_mix"][l]
        if l < N_A:
            pw = W["rwkv"][l]
            x_last = x.reshape(nseq, T, D_MODEL)[:, -1]
            new_shift.append(_rmsnorm(x_last, g_mix, name="shift_norm"))
            sp = _mm([shift0[l].astype(F32)], [pw["w_in"]], name="rwkv_in_shift")
            r, lw, k, v, kn, b, g, q_mem = _rwkv_prep(x, sp, pw, v_first if l > 0 else None, g_mix, nseq=nseq, T=T)
            if l == 0:
                v_first = v
            h0 = wkv0[l].astype(F32)
            o_tok, h_new = _wkv(r, lw, k, v, kn, b, g, P["r_k"][l], P["lnx_w"][l], P["lnx_b"][l], h0,
                                nseq=nseq, T=T, C=chunk)
            new_wkv.append(h_new)
        else:
            j = l - N_A
            if l == N_A:
                res = _mla_kv(x, P["norm_kv"], W["mla"]["w_kva"], P["norm_ckv"], cs,
                              W["mla"]["w_kvx"] if past is None else None, T=T)
                ckv, kpe = res[0], res[1]
                kvx = res[2] if past is None else None
            q, q_mem = _mla_q(x, g_mix, W["w_in_b"][j], P["norm_q"][j], W["mla"]["w_q"][j], cs, T=T)
            if past is None:
                o_tok = _flash_causal(q, kvx, nb=nseq, T=T)
            else:
                o_tok = _decode_attn(past[2], q.reshape(nseq, T, H_B * 256), past[0], past[1],
                                     ckv.reshape(nseq, T, KV_LORA), kpe.reshape(nseq, T, ROPE_DIM),
                                     W["mla"]["w_ukT"], W["mla"]["w_uv"]).reshape(M, H_B * V_DIM)
        o_mem = _mem_attn(q_mem, mem_k, mem_v, l, nb=nseq, T=T)
        x, c_new, y = _ffn(x, o_tok, o_mem, W["w_o_tok"][l], W["w_o_mem"][l], P["norm_ffn"][l], W["w_gate"][l],
                           W["w_val"][l], P["conv_w"][l], P["conv_b"][l], W["w_down"][l], conv0[l],
                           P["final_norm"] if l == DEPTH - 1 else None, nseq=nseq, T=T)
        new_conv.append(c_new)
    return (y.reshape(nseq, T, D_MODEL), ckv.reshape(nseq, T, KV_LORA), kpe.reshape(nseq, T, ROPE_DIM),
            jnp.stack(new_wkv), jnp.stack(new_shift), jnp.stack(new_conv))


def kernel(x_prompt, x_sample, cache_ckv, cache_kpe, cache_mem_k, cache_mem_v, state_wkv, state_shift, state_conv, page_table, mem_prompt, norm_mix, norm_ffn, norm_mem, w_mem_kv, w_o, w_in_a, mu_a, w_vres_in, mu_vres, w_decay_up, w0, w_a_up, a0, w_g_up, w_vres_up, v0, k_k, k_a, r_k, lnx_w, lnx_b, norm_kv, w_kv_a, norm_ckv, w_kv_b, w_in_b, norm_q, w_q_b, w_ffn_up, conv_w, conv_b, w_ffn_down, final_norm):
    P = dict(norm_mix=norm_mix, norm_ffn=norm_ffn, w_o=w_o, w_in_a=w_in_a, mu_a=mu_a, w_vres_in=w_vres_in,
             mu_vres=mu_vres, w_decay_up=w_decay_up, w0=w0, w_a_up=w_a_up, a0=a0, w_g_up=w_g_up,
             w_vres_up=w_vres_up, v0=v0, k_k=k_k, k_a=k_a, r_k=r_k, lnx_w=lnx_w, lnx_b=lnx_b,
             norm_kv=norm_kv, w_kv_a=w_kv_a, norm_ckv=norm_ckv, w_kv_b=w_kv_b, w_in_b=w_in_b, norm_q=norm_q,
             w_q_b=w_q_b, conv_w=conv_w, conv_b=conv_b, final_norm=final_norm)
    W = dict(
        rwkv=[_prep_rwkv_weights(l, P) for l in range(N_A)],
        mla=_prep_mla_weights(P),
        w_in_b=[w_in_b[j].astype(BF16) for j in range(N_B)],
        w_o_tok=[w_o[l, :MIX_A].astype(BF16) for l in range(DEPTH)],
        w_o_mem=[w_o[l, MIX_A:].astype(BF16) for l in range(DEPTH)],
        w_gate=[w_ffn_up[l, :, :D_FF].astype(BF16) for l in range(DEPTH)],
        w_val=[w_ffn_up[l, :, D_FF:].astype(BF16) for l in range(DEPTH)],
        w_down=[w_ffn_down[l].astype(BF16) for l in range(DEPTH)],
    )
    dt = x_prompt.dtype
    B, T = x_prompt.shape[:2]
    n_mem = mem_prompt.shape[1]
    mem_k_all, mem_v_all = _mem_kv(mem_prompt, norm_mem, w_mem_kv)
    y_p, ckv_p, kpe_p, wkv_p, shift_p, conv_p = _trunk(
        x_prompt, jnp.arange(T), mem_k_all, mem_v_all,
        jnp.zeros((N_A, B, H_A, HEAD_A, HEAD_A), dt), jnp.zeros((N_A, B, D_MODEL), dt),
        jnp.zeros((DEPTH, B, CONV_W - 1, D_FF), dt), None, P, W, nseq=B, T=T)
    DB, TS = x_sample.shape[:2]
    past_len = page_table.shape[1] * PAGE_SIZE
    to_t = lambda c: jnp.transpose(c, (0, 1, 3, 4, 2)).reshape(DEPTH, DB, MEM_WIDTH, n_mem)
    y_s, ckv_s, kpe_s, wkv_s, shift_s, conv_s = _trunk(
        x_sample, past_len + jnp.arange(TS), to_t(cache_mem_k), to_t(cache_mem_v), state_wkv, state_shift, state_conv,
        (cache_ckv, cache_kpe, page_table), P, W, nseq=DB, T=TS)
    from_t = lambda a: jnp.transpose(a.reshape(DEPTH, B, MEM_HEADS, MEM_HEAD_DIM, n_mem), (0, 1, 4, 2, 3))
    mem_k_out = from_t(mem_k_all)
    mem_v_out = from_t(mem_v_all)
    return (y_p, y_s, ckv_p, kpe_p, mem_k_out, mem_v_out, wkv_p, shift_p, conv_p,
            ckv_s, kpe_s, wkv_s, shift_s, conv_s)
```
